```python
import jax, jax.numpy as jnp
from jax import lax
import numpy as np

D_MODEL = 1024
BATCH = 4
SEQ = 8192
DEPTH = 4

CTX_LEN = 256
GRID_W = 64

NA_HEADS = D_MODEL // 128
HEAD_DIM = 64
NA_WIDTH = NA_HEADS * HEAD_DIM
NA_KH = 8
NA_KW = 16
ROPE_AXIS_DIM = HEAD_DIM // 2
ROPE_THETA = 10000.0
POOL_WIDTH = D_MODEL // 4
POOL_GROUPS = 4
POOL_GROUP_DIM = POOL_WIDTH // POOL_GROUPS
POOL_WINDOWS = (2, 4, 8, 16)
FOURIER_WIDTH = D_MODEL // 4
FOURIER_GROUPS = 4
FOURIER_GROUP_DIM = FOURIER_WIDTH // FOURIER_GROUPS
CONV_WIDTH = D_MODEL // 4
CONV_K = 31
N_BRANCH = 4
D_FF = 4 * D_MODEL
EPS = 1e-6

Q_OFF = 0
K_OFF = NA_WIDTH
V_OFF = 2 * NA_WIDTH
POOL_OFF = 3 * NA_WIDTH
FOUR_OFF = POOL_OFF + POOL_WIDTH
CONV_OFF = FOUR_OFF + FOURIER_WIDTH
GATE_OFF = CONV_OFF + 2 * CONV_WIDTH
IN_WIDTH = GATE_OFF + N_BRANCH * D_MODEL

kernel_name = "hybrid_na_pool_fourier_conv_dit"


def rmsnorm(x, g):
    xf = x.astype(jnp.float32)
    y = xf * lax.rsqrt(jnp.mean(xf * xf, axis=-1, keepdims=True) + EPS)
    return (y * g.astype(jnp.float32)).astype(x.dtype)


def ada(cvec, w_mod, b_mod, n_chunks):
    m = jax.nn.silu(cvec) @ w_mod[:, :n_chunks * D_MODEL] + b_mod[:n_chunks * D_MODEL]
    m = m[..., None, :]
    return jnp.split(m, n_chunks, axis=-1)


def modulate(h, shift, scale):
    return h * (1.0 + scale) + shift


def split_qkv(p):
    hs = p.shape[:-1] + (NA_HEADS, HEAD_DIM)
    return (p[..., Q_OFF:K_OFF].reshape(hs), p[..., K_OFF:V_OFF].reshape(hs), p[..., V_OFF:POOL_OFF].reshape(hs))


def axial_rope_tables(n, dtype):
    t = jnp.arange(n)
    row = (t // GRID_W).astype(jnp.float32)
    col = (t % GRID_W).astype(jnp.float32)
    inv = ROPE_THETA ** (-jnp.arange(0, ROPE_AXIS_DIM, 2, dtype=jnp.float32) / ROPE_AXIS_DIM)
    ang = jnp.concatenate([row[:, None] * inv, col[:, None] * inv], axis=-1)
    return jnp.cos(ang).astype(dtype), jnp.sin(ang).astype(dtype)


def _rotate(xh, cos, sin):
    half = xh.shape[-1] // 2
    x1, x2 = xh[..., :half], xh[..., half:]
    return jnp.concatenate([x1 * cos - x2 * sin, x2 * cos + x1 * sin], axis=-1)


def apply_axial_rope(x, cos_t, sin_t):
    nf = ROPE_AXIS_DIM // 2
    cr, sr = cos_t[:, None, :nf], sin_t[:, None, :nf]
    cc, sc = cos_t[:, None, nf:], sin_t[:, None, nf:]
    return jnp.concatenate([_rotate(x[..., :ROPE_AXIS_DIM], cr, sr), _rotate(x[..., ROPE_AXIS_DIM:], cc, sc)], axis=-1)


def neighborhood_attention(q, k, v, k_ctx, v_ctx, rpb):
    bsz, n, nh, hd = q.shape
    rows = n // GRID_W
    kh = min(NA_KH, rows)
    scale = hd ** -0.5
    qg = q.reshape(bsz, rows, GRID_W, nh, hd).transpose(1, 0, 2, 3, 4)
    kg = k.reshape(bsz, rows, GRID_W, nh, hd)
    vg = v.reshape(bsz, rows, GRID_W, nh, hd)
    cq = jnp.arange(GRID_W)
    col_start = jnp.clip(cq - NA_KW // 2, 0, GRID_W - NA_KW)
    col_idx = col_start[:, None] + jnp.arange(NA_KW)[None, :]
    dcol = col_idx - cq[:, None] + (NA_KW - 1)
    rpb_col = rpb[:, :, dcol]
    n_win = kh * NA_KW

    def one_row(args):
        r, q_r = args
        rs = jnp.clip(r - kh // 2, 0, rows - kh)
        k_blk = lax.dynamic_slice_in_dim(kg, rs, kh, axis=1)
        v_blk = lax.dynamic_slice_in_dim(vg, rs, kh, axis=1)
        k_win = k_blk[:, :, col_idx]
        v_win = v_blk[:, :, col_idx]
        drow = rs + jnp.arange(kh) - r + (NA_KH - 1)
        bias = rpb_col[:, drow].transpose(0, 2, 1, 3)
        s_win = jnp.einsum('bqhd,brqwhd->bhqrw', q_r, k_win).astype(jnp.float32) * scale + bias.astype(jnp.float32)
        s_ctx = jnp.einsum('bqhd,bchd->bhqc', q_r, k_ctx).astype(jnp.float32) * scale
        s = jnp.concatenate([s_win.reshape(bsz, nh, GRID_W, n_win), s_ctx], axis=-1)
        p = jax.nn.softmax(s, axis=-1).astype(v.dtype)
        p_win = p[..., :n_win].reshape(bsz, nh, GRID_W, kh, NA_KW)
        p_ctx = p[..., n_win:]
        return (jnp.einsum('bhqrw,brqwhd->bqhd', p_win, v_win)
                + jnp.einsum('bhqc,bchd->bqhd', p_ctx, v_ctx))

    out = lax.map(one_row, (jnp.arange(rows), qg))
    return out.transpose(1, 0, 2, 3, 4).reshape(bsz, n, nh * hd)


def context_attention(q, k, v):
    bsz, l = q.shape[0], q.shape[1]
    s = jnp.einsum('blhd,bmhd->bhlm', q, k).astype(jnp.float32) * (HEAD_DIM ** -0.5)
    p = jax.nn.softmax(s, axis=-1).astype(v.dtype)
    return jnp.einsum('bhlm,bmhd->blhd', p, v).reshape(bsz, l, NA_WIDTH)


def pool_mix(u, w_pool, pool_scale):
    bsz, n, _ = u.shape
    uf = u.astype(jnp.float32).reshape(bsz, n, POOL_GROUPS, POOL_GROUP_DIM)
    cs = jnp.concatenate([jnp.zeros_like(uf[:, :1]), jnp.cumsum(uf, axis=1)], axis=1)
    t = jnp.arange(n)
    outs = []
    for gi, w in enumerate(POOL_WINDOWS):
        lo = w // 2
        hi = w - lo - 1
        start = jnp.clip(t - lo, 0, n)
        end = jnp.clip(t + hi + 1, 0, n)
        csg = cs[:, :, gi]
        cnt = (end - start).astype(jnp.float32)[None, :, None]
        outs.append((csg[:, end] - csg[:, start]) / cnt - uf[:, :, gi])
    d = jnp.stack(outs, axis=2)
    y = jnp.einsum('bngc,gcd->bngd', d, w_pool.astype(jnp.float32)).reshape(bsz, n, POOL_WIDTH)
    return (y * pool_scale.astype(jnp.float32)).astype(u.dtype)


def fourier_mix(u):
    bsz, n, _ = u.shape
    uf = u.astype(jnp.float32).reshape(bsz, n, FOURIER_GROUPS, FOURIER_GROUP_DIM)
    y = jnp.fft.fft2(uf, axes=(1, 3), norm='ortho').real
    return y.reshape(bsz, n, FOURIER_WIDTH).astype(u.dtype)


def conv_module(u2, w_dw, b_dw, ln_g, ln_b):
    a, g = u2[..., :CONV_WIDTH], u2[..., CONV_WIDTH:]
    z = a * jax.nn.sigmoid(g)
    z = lax.conv_general_dilated(z, w_dw.reshape(CONV_K, 1, CONV_WIDTH).astype(z.dtype), window_strides=(1,),
                                 padding=[(CONV_K // 2, CONV_K // 2)], dimension_numbers=('NWC', 'WIO', 'NWC'),
                                 feature_group_count=CONV_WIDTH) + b_dw
    zf = z.astype(jnp.float32)
    mu = jnp.mean(zf, axis=-1, keepdims=True)
    var = jnp.mean(jnp.square(zf - mu), axis=-1, keepdims=True)
    zf = (zf - mu) * lax.rsqrt(var + EPS) * ln_g.astype(jnp.float32) + ln_b.astype(jnp.float32)
    return jax.nn.silu(zf).astype(u2.dtype)


def mixer_output(attn, p, w_pool, pool_scale, w_dw, b_dw, ln_g, ln_b,
                 w_br_attn, w_br_pool, w_br_fourier, w_br_conv, w_out):
    y_attn = attn @ w_br_attn
    y_pool = pool_mix(p[..., POOL_OFF:FOUR_OFF], w_pool, pool_scale) @ w_br_pool
    y_four = fourier_mix(p[..., FOUR_OFF:CONV_OFF]) @ w_br_fourier
    y_conv = conv_module(p[..., CONV_OFF:GATE_OFF], w_dw, b_dw, ln_g, ln_b) @ w_br_conv
    g = jax.nn.sigmoid(p[..., GATE_OFF:])
    merged = (g[..., :D_MODEL] * y_attn + g[..., D_MODEL:2 * D_MODEL] * y_pool
              + g[..., 2 * D_MODEL:3 * D_MODEL] * y_four + g[..., 3 * D_MODEL:] * y_conv)
    return merged @ w_out


def sq_relu_mlp(h, w1, w2):
    return jnp.square(jax.nn.relu(h @ w1)) @ w2


def setup_inputs(seed: int = 0) -> dict:
    key = jax.random.key(seed)
    ks = jax.random.split(key, 24)
    f32 = jnp.float32
    L, D = DEPTH, D_MODEL

    def nrm(k, shape, s):
        return jax.random.normal(k, shape, f32) * s

    return {
        "x": nrm(ks[0], (BATCH, SEQ, D), 1.0),
        "c": nrm(ks[1], (BATCH, D), 1.0),
        "ctx": nrm(ks[2], (BATCH, CTX_LEN, D), 1.0),
        "c_ctx": nrm(ks[3], (D,), 1.0),
        "w_mod": nrm(ks[4], (L, D, 6 * D), 0.5 * D ** -0.5),
        "b_mod": nrm(ks[5], (L, 6 * D), 0.02),
        "g_mix": 1.0 + nrm(ks[6], (L, D), 0.02),
        "g_ff": 1.0 + nrm(ks[7], (L, D), 0.02),
        "w_in": nrm(ks[8], (L, D, IN_WIDTH), D ** -0.5),
        "rpb": nrm(ks[9], (L, NA_HEADS, 2 * NA_KH - 1, 2 * NA_KW - 1), 0.1),
        "w_pool": nrm(ks[10], (L, POOL_GROUPS, POOL_GROUP_DIM, POOL_GROUP_DIM), POOL_GROUP_DIM ** -0.5),
        "pool_scale": 1.0 + nrm(ks[11], (L, POOL_WIDTH), 0.02),
        "w_dw": nrm(ks[12], (L, CONV_K, CONV_WIDTH), CONV_K ** -0.5),
        "b_dw": nrm(ks[13], (L, CONV_WIDTH), 0.02),
        "conv_ln_g": 1.0 + nrm(ks[14], (L, CONV_WIDTH), 0.02),
        "conv_ln_b": nrm(ks[15], (L, CONV_WIDTH), 0.02),
        "w_br_attn": nrm(ks[16], (L, NA_WIDTH, D), NA_WIDTH ** -0.5),
        "w_br_pool": nrm(ks[17], (L, POOL_WIDTH, D), POOL_WIDTH ** -0.5),
        "w_br_fourier": nrm(ks[18], (L, FOURIER_WIDTH, D), FOURIER_WIDTH ** -0.5),
        "w_br_conv": nrm(ks[19], (L, CONV_WIDTH, D), CONV_WIDTH ** -0.5),
        "w_out": nrm(ks[20], (L, D, D), D ** -0.5),
        "w_ff1": nrm(ks[21], (L, D, D_FF), D ** -0.5),
        "w_ff2": nrm(ks[22], (L, D_FF, D), D_FF ** -0.5),
        "g_final": 1.0 + nrm(ks[23], (D,), 0.02),
    }


def reference(x, c, ctx, c_ctx, w_mod, b_mod, g_mix, g_ff, w_in, rpb, w_pool, pool_scale, w_dw, b_dw,
              conv_ln_g, conv_ln_b, w_br_attn, w_br_pool, w_br_fourier, w_br_conv, w_out, w_ff1, w_ff2, g_final):
    n = x.shape[1]
    cos_t, sin_t = axial_rope_tables(n, x.dtype)
    h_ctx = ctx
    for l in range(DEPTH):
        last = l == DEPTH - 1
        tails = (w_pool[l], pool_scale[l], w_dw[l], b_dw[l], conv_ln_g[l], conv_ln_b[l],
                 w_br_attn[l], w_br_pool[l], w_br_fourier[l], w_br_conv[l], w_out[l])
        if last:
            sh1c, sc1c = ada(c_ctx, w_mod[l], b_mod[l], 2)
            hc = modulate(rmsnorm(h_ctx, g_mix[l]), sh1c, sc1c)
            kv = hc @ w_in[l][:, K_OFF:POOL_OFF]
            hs = kv.shape[:-1] + (NA_HEADS, HEAD_DIM)
            k_c = kv[..., :NA_WIDTH].reshape(hs)
            v_c = kv[..., NA_WIDTH:].reshape(hs)
        else:
            sh1c, sc1c, gt1c, sh2c, sc2c, gt2c = ada(c_ctx, w_mod[l], b_mod[l], 6)
            hc = modulate(rmsnorm(h_ctx, g_mix[l]), sh1c, sc1c)
            pc = hc @ w_in[l]
            q_c, k_c, v_c = split_qkv(pc)
            ctx_next = h_ctx + gt1c * mixer_output(context_attention(q_c, k_c, v_c), pc, *tails)
            hc2 = modulate(rmsnorm(ctx_next, g_ff[l]), sh2c, sc2c)
            ctx_next = ctx_next + gt2c * sq_relu_mlp(hc2, w_ff1[l], w_ff2[l])
        sh1, sc1, gt1, sh2, sc2, gt2 = ada(c, w_mod[l], b_mod[l], 6)
        h = modulate(rmsnorm(x, g_mix[l]), sh1, sc1)
        p = h @ w_in[l]
        q, k, v = split_qkv(p)
        q = apply_axial_rope(q, cos_t, sin_t)
        k = apply_axial_rope(k, cos_t, sin_t)
        attn = neighborhood_attention(q, k, v, k_c, v_c, rpb[l])
        x = x + gt1 * mixer_output(attn, p, *tails)
        h2 = modulate(rmsnorm(x, g_ff[l]), sh2, sc2)
        x = x + gt2 * sq_relu_mlp(h2, w_ff1[l], w_ff2[l])
        if not last:
            h_ctx = ctx_next
    return rmsnorm(x, g_final)
```

```python
import functools

import numpy as np
import jax
import jax.numpy as jnp
from jax import lax
from jax.experimental import pallas as pl
from jax.experimental.pallas import tpu as pltpu

F32 = jnp.float32
BF16 = jnp.bfloat16

D_MODEL = 1024
DEPTH = 4
GRID_W = 64
NA_HEADS = 8
HEAD_DIM = 64
NA_WIDTH = NA_HEADS * HEAD_DIM
NA_KH = 8
NA_KW = 16
ROPE_AXIS_DIM = HEAD_DIM // 2
ROPE_THETA = 10000.0
POOL_WIDTH = 256
POOL_GROUPS = 4
POOL_GROUP_DIM = 64
POOL_WINDOWS = (2, 4, 8, 16)
FOURIER_WIDTH = 256
FOURIER_GROUP_DIM = 64
CONV_WIDTH = 256
CONV_K = 31
N_BRANCH = 4
D_FF = 4 * D_MODEL
EPS = 1e-6
QKV_WIDTH = 3 * NA_WIDTH
PFC_WIDTH = POOL_WIDTH + FOURIER_WIDTH + 2 * CONV_WIDTH
GATE_WIDTH = N_BRANCH * D_MODEL
IN_WIDTH = QKV_WIDTH + PFC_WIDTH + GATE_WIDTH

LANES = 128
V7X_VMEM_BYTES = 64 * 1024 * 1024
HALO = 16
MASK_VALUE = -1e30

IN_TILE = 512
N_IN_TILES = IN_WIDTH // IN_TILE
N_QKV_TILES = QKV_WIDTH // IN_TILE
N_PFC_TILES = PFC_WIDTH // IN_TILE
N_GATE_TILES = GATE_WIDTH // IN_TILE

FFT_N1 = 128
FFT_N2 = 64


def _params(sem, vmem_mb):
    return pltpu.CompilerParams(dimension_semantics=sem, vmem_limit_bytes=vmem_mb * 1024 * 1024)


def _bf16_const(a):
    return jnp.asarray(a, F32).astype(BF16)


def _row_tile(n):
    return min(n, 1024)


def _ada_kernel(c_ref, w_ref, b_ref, o_ref):
    cv = c_ref[...]
    s = cv * jax.nn.sigmoid(cv)
    o_ref[0] = jnp.dot(s.astype(BF16), w_ref[0].astype(BF16), preferred_element_type=F32) + b_ref[0]


def _ada_all(cvecs, w_mod, b_mod):
    depth, d, width = w_mod.shape
    rows = cvecs.shape[0]
    tn = 1536
    return pl.pallas_call(
        _ada_kernel,
        grid=(depth, width // tn),
        in_specs=[
            pl.BlockSpec((rows, d), lambda l, j: (0, 0)),
            pl.BlockSpec((1, d, tn), lambda l, j: (l, 0, j)),
            pl.BlockSpec((1, 1, tn), lambda l, j: (l, 0, j)),
        ],
        out_specs=pl.BlockSpec((1, rows, tn), lambda l, j: (l, 0, j)),
        out_shape=jax.ShapeDtypeStruct((depth, rows, width), F32),
        compiler_params=_params(("arbitrary", "arbitrary"), 32),
        name="ada",
    )(cvecs, w_mod, b_mod.reshape(depth, 1, width))


def _norm_modulate(x, g, shift, scale):
    ms = jnp.mean(x * x, axis=-1, keepdims=True)
    y = x * lax.rsqrt(ms + EPS) * g
    return y * (1.0 + scale) + shift


def _in_proj_kernel(x_ref, g_ref, sh_ref, sc_ref, cos_ref, sin_ref, w_ref,
                    qkv_ref, pfc_ref, gate_ref, h_ref, *, rope):
    j = pl.program_id(2)

    @pl.when(j == 0)
    def _():
        h_ref[...] = _norm_modulate(x_ref[0], g_ref[...], sh_ref[0], sc_ref[0]).astype(BF16)

    acc = jnp.dot(h_ref[...], w_ref[...], preferred_element_type=F32)

    @pl.when(j < 2)
    def _():
        mult = jnp.where(j == 0, HEAD_DIM ** -0.5, 1.0).astype(F32)
        if rope:
            cos = cos_ref[...]
            sin = sin_ref[...]
            lane = lax.broadcasted_iota(jnp.int32, cos.shape, 1)
            first = (lane % (ROPE_AXIS_DIM)) < (ROPE_AXIS_DIM // 2)
            for cc in range(IN_TILE // LANES):
                a = acc[:, cc * LANES:(cc + 1) * LANES]
                partner = jnp.where(first,
                                    pltpu.roll(a, LANES - ROPE_AXIS_DIM // 2, 1),
                                    pltpu.roll(a, ROPE_AXIS_DIM // 2, 1))
                qkv_ref[0, :, cc * LANES:(cc + 1) * LANES] = ((a * cos + partner * sin) * mult).astype(BF16)
        else:
            qkv_ref[0] = (acc * mult).astype(BF16)

    @pl.when(j == 2)
    def _():
        qkv_ref[0] = acc.astype(BF16)

    @pl.when(jnp.logical_and(j >= N_QKV_TILES, j < N_QKV_TILES + N_PFC_TILES))
    def _():
        pfc_ref[0] = acc

    @pl.when(j >= N_QKV_TILES + N_PFC_TILES)
    def _():
        gate_ref[0] = jax.nn.sigmoid(acc).astype(BF16)


def _in_proj(x, g, shift, scale, cos_t, sin_t, w_bf, *, rope):
    bsz, n, d = x.shape
    tm = _row_tile(n)
    grid = (bsz, n // tm, N_IN_TILES)
    qkv_map = lambda b, i, j: (b, i, jnp.minimum(j, N_QKV_TILES - 1))
    pfc_map = lambda b, i, j: (b, i, jnp.clip(j - N_QKV_TILES, 0, N_PFC_TILES - 1))
    gate_map = lambda b, i, j: (b, i, jnp.clip(j - N_QKV_TILES - N_PFC_TILES, 0, N_GATE_TILES - 1))
    return pl.pallas_call(
        functools.partial(_in_proj_kernel, rope=rope),
        grid=grid,
        in_specs=[
            pl.BlockSpec((1, tm, d), lambda b, i, j: (b, i, 0)),
            pl.BlockSpec((1, d), lambda b, i, j: (0, 0)),
            pl.BlockSpec((1, 1, d), lambda b, i, j: (b, 0, 0)),
            pl.BlockSpec((1, 1, d), lambda b, i, j: (b, 0, 0)),
            pl.BlockSpec((tm, LANES), lambda b, i, j: (i, 0)),
            pl.BlockSpec((tm, LANES), lambda b, i, j: (i, 0)),
            pl.BlockSpec((d, IN_TILE), lambda b, i, j: (0, j)),
        ],
        out_specs=[
            pl.BlockSpec((1, tm, IN_TILE), qkv_map),
            pl.BlockSpec((1, tm, IN_TILE), pfc_map),
            pl.BlockSpec((1, tm, IN_TILE), gate_map),
        ],
        out_shape=[
            jax.ShapeDtypeStruct((bsz, n, QKV_WIDTH), BF16),
            jax.ShapeDtypeStruct((bsz, n, PFC_WIDTH), F32),
            jax.ShapeDtypeStruct((bsz, n, GATE_WIDTH), BF16),
        ],
        scratch_shapes=[pltpu.VMEM((tm, d), BF16)],
        compiler_params=_params(("arbitrary", "arbitrary", "arbitrary"), 48),
        name="in_proj",
    )(x, g.reshape(1, d), shift, scale, cos_t, sin_t, w_bf)


def _stack_head_pair(qp):
    lane = lax.broadcasted_iota(jnp.int32, qp.shape, 1)
    zero = jnp.zeros_like(qp)
    return jnp.concatenate([jnp.where(lane < HEAD_DIM, qp, zero), jnp.where(lane < HEAD_DIM, zero, qp)], axis=0)


def _unstack_head_pair(o2):
    m = o2.shape[0] // 2
    lane = lax.broadcasted_iota(jnp.int32, (m, LANES), 1)
    return jnp.where(lane < HEAD_DIM, o2[:m], o2[m:])


_NT = (((1,), (1,)), ((), ()))


def _nattn_kernel(q_ref, k_ref, v_ref, kc_ref, vc_ref, bias_ref, o_ref, *, rows_per_step, n_rows):
    i = pl.program_id(1)
    win = NA_KH * GRID_W

    def row_body(rl, carry):
        r = i * rows_per_step + rl
        rs = jnp.clip(r - NA_KH // 2, 0, n_rows - NA_KH)
        var = rs - r + (NA_KH - 1)
        qoff = pl.multiple_of(rl * GRID_W, GRID_W)
        koff = pl.multiple_of(rs * GRID_W, GRID_W)
        for p in range(NA_HEADS // 2):
            cs = slice(p * LANES, (p + 1) * LANES)
            q2 = _stack_head_pair(q_ref[0, pl.ds(qoff, GRID_W), cs])
            kp = k_ref[0, pl.ds(koff, win), cs]
            vp = v_ref[0, pl.ds(koff, win), cs]
            s_w = lax.dot_general(q2, kp, _NT, preferred_element_type=F32) + bias_ref[var, p]
            s_c = lax.dot_general(q2, kc_ref[0, :, cs], _NT, preferred_element_type=F32)
            m = jnp.maximum(jnp.max(s_w, axis=-1, keepdims=True), jnp.max(s_c, axis=-1, keepdims=True))
            p_w = jnp.exp(s_w - m)
            p_c = jnp.exp(s_c - m)
            l = jnp.sum(p_w, axis=-1, keepdims=True) + jnp.sum(p_c, axis=-1, keepdims=True)
            o2 = (jnp.dot(p_w.astype(BF16), vp, preferred_element_type=F32)
                  + jnp.dot(p_c.astype(BF16), vc_ref[0, :, cs], preferred_element_type=F32))
            o2 = o2 / l
            o_ref[0, pl.ds(qoff, GRID_W), cs] = _unstack_head_pair(o2).astype(BF16)
        return carry

    lax.fori_loop(0, rows_per_step, row_body, 0)


def _nattn(qkv, qkv_ctx, bias_tbl):
    bsz, n, _ = qkv.shape
    n_ctx = qkv_ctx.shape[1]
    n_rows = n // GRID_W
    rows_per_step = 8
    tq = rows_per_step * GRID_W
    single = pl.Buffered(1)
    return pl.pallas_call(
        functools.partial(_nattn_kernel, rows_per_step=rows_per_step, n_rows=n_rows),
        grid=(bsz, n_rows // rows_per_step),
        in_specs=[
            pl.BlockSpec((1, tq, NA_WIDTH), lambda b, i: (b, i, 0)),
            pl.BlockSpec((1, n, NA_WIDTH), lambda b, i: (b, 0, 1), pipeline_mode=single),
            pl.BlockSpec((1, n, NA_WIDTH), lambda b, i: (b, 0, 2), pipeline_mode=single),
            pl.BlockSpec((1, n_ctx, NA_WIDTH), lambda b, i: (b, 0, 1)),
            pl.BlockSpec((1, n_ctx, NA_WIDTH), lambda b, i: (b, 0, 2)),
            pl.BlockSpec(bias_tbl.shape, lambda b, i: (0, 0, 0, 0), pipeline_mode=single),
        ],
        out_specs=pl.BlockSpec((1, tq, NA_WIDTH), lambda b, i: (b, i, 0)),
        out_shape=jax.ShapeDtypeStruct((bsz, n, NA_WIDTH), BF16),
        compiler_params=_params(("arbitrary", "arbitrary"), 48),
        name="nattn",
    )(qkv, qkv, qkv, qkv_ctx, qkv_ctx, bias_tbl)


def _ctx_attn_kernel(q_ref, k_ref, v_ref, o_ref):
    for p in range(NA_HEADS // 2):
        cs = slice(p * LANES, (p + 1) * LANES)
        q2 = _stack_head_pair(q_ref[0, :, cs])
        s = lax.dot_general(q2, k_ref[0, :, cs], _NT, preferred_element_type=F32)
        m = jnp.max(s, axis=-1, keepdims=True)
        e = jnp.exp(s - m)
        l = jnp.sum(e, axis=-1, keepdims=True)
        o2 = jnp.dot(e.astype(BF16), v_ref[0, :, cs], preferred_element_type=F32) / l
        o_ref[0, :, cs] = _unstack_head_pair(o2).astype(BF16)


def _ctx_attn(qkv_ctx):
    bsz, n_ctx, _ = qkv_ctx.shape
    return pl.pallas_call(
        _ctx_attn_kernel,
        grid=(bsz,),
        in_specs=[pl.BlockSpec((1, n_ctx, NA_WIDTH), lambda b, c=c: (b, 0, c)) for c in range(3)],
        out_specs=pl.BlockSpec((1, n_ctx, NA_WIDTH), lambda b: (b, 0, 0)),
        out_shape=jax.ShapeDtypeStruct((bsz, n_ctx, NA_WIDTH), BF16),
        compiler_params=_params(("arbitrary",), 32),
        name="ctx_attn",
    )(qkv_ctx, qkv_ctx, qkv_ctx)


def _attn_bias_table(rpb_l):
    cq = jnp.arange(GRID_W)
    kc = jnp.arange(GRID_W)
    col_start = jnp.clip(cq - NA_KW // 2, 0, GRID_W - NA_KW)
    valid = (kc[None, :] >= col_start[:, None]) & (kc[None, :] < col_start[:, None] + NA_KW)
    dcol = jnp.clip(kc[None, :] - cq[:, None] + (NA_KW - 1), 0, 2 * NA_KW - 2)
    plane = jnp.where(valid[None, None], rpb_l[:, :, dcol], MASK_VALUE)
    variants = []
    for vi in range(NA_KH):
        blk = plane[:, vi:vi + NA_KH]
        variants.append(blk.transpose(0, 2, 1, 3).reshape(NA_HEADS, GRID_W, NA_KH * GRID_W))
    tbl = jnp.stack(variants, axis=0)
    return tbl.reshape(NA_KH, NA_HEADS // 2, 2 * GRID_W, NA_KH * GRID_W).astype(F32)


def _halo_specs(tm, n, width, col_block):
    per = tm // HALO
    last = n // HALO - 1
    return [
        pl.BlockSpec((1, HALO, width), lambda b, i: (b, jnp.maximum(i * per - 1, 0), col_block)),
        pl.BlockSpec((1, tm, width), lambda b, i: (b, i, col_block)),
        pl.BlockSpec((1, HALO, width), lambda b, i: (b, jnp.minimum((i + 1) * per, last), col_block)),
    ]


def _pool_kernel(prev_ref, cur_ref, next_ref, wbd_ref, ps_ref, o_ref, ext_ref, *, tm, n_tok, rc):
    i = pl.program_id(1)
    nt = pl.num_programs(1)
    ext_ref[0:HALO] = jnp.where(i > 0, prev_ref[0], 0.0)
    ext_ref[HALO:HALO + tm] = cur_ref[0]
    ext_ref[HALO + tm:HALO + tm + HALO] = jnp.where(i < nt - 1, next_ref[0], 0.0)

    row = lax.broadcasted_iota(jnp.int32, (rc, LANES), 0)
    lane = lax.broadcasted_iota(jnp.int32, (rc, LANES), 1)
    first = lane < POOL_GROUP_DIM

    for ci in range(tm // rc):
        base = ci * rc
        t = i * tm + base + row
        halves = []
        for half in range(2):
            cs = slice(half * LANES, (half + 1) * LANES)
            w_a, w_b = POOL_WINDOWS[2 * half], POOL_WINDOWS[2 * half + 1]
            lo_a, lo_b = w_a // 2, w_b // 2
            hi_a, hi_b = w_a - lo_a - 1, w_b - lo_b - 1

            def ld(off, cs=cs):
                return ext_ref[pl.ds(base + (HALO + off), rc), cs]

            x0 = ld(0)
            s_a = x0
            for off in range(-lo_a, hi_a + 1):
                if off != 0:
                    s_a = s_a + ld(off)
            s_b = s_a
            for off in list(range(-lo_b, -lo_a)) + list(range(hi_a + 1, hi_b + 1)):
                s_b = s_b + ld(off)
            lo = jnp.where(first, lo_a, lo_b)
            hi = jnp.where(first, hi_a, hi_b)
            cnt = jnp.minimum(t + hi + 1, n_tok) - jnp.maximum(t - lo, 0)
            halves.append(jnp.where(first, s_a, s_b) / cnt.astype(F32) - x0)
        d = jnp.concatenate(halves, axis=1).astype(BF16)
        y = jnp.dot(d, wbd_ref[...], preferred_element_type=F32) * ps_ref[...]
        o_ref[0, pl.ds(base, rc), :] = y.astype(BF16)


def _pool_mix(pfc, wbd_bf, pool_scale):
    bsz, n, _ = pfc.shape
    tm = _row_tile(n)
    rc = min(tm, 128)
    return pl.pallas_call(
        functools.partial(_pool_kernel, tm=tm, n_tok=n, rc=rc),
        grid=(bsz, n // tm),
        in_specs=_halo_specs(tm, n, POOL_WIDTH, 0) + [
            pl.BlockSpec((POOL_WIDTH, POOL_WIDTH), lambda b, i: (0, 0)),
            pl.BlockSpec((1, POOL_WIDTH), lambda b, i: (0, 0)),
        ],
        out_specs=pl.BlockSpec((1, tm, POOL_WIDTH), lambda b, i: (b, i, 0)),
        out_shape=jax.ShapeDtypeStruct((bsz, n, POOL_WIDTH), BF16),
        scratch_shapes=[pltpu.VMEM((tm + 2 * HALO, POOL_WIDTH), F32)],
        compiler_params=_params(("arbitrary", "arbitrary"), 32),
        name="pool",
    )(pfc, pfc, pfc, wbd_bf, pool_scale.reshape(1, POOL_WIDTH))


def _glu(u):
    return u[:, :CONV_WIDTH] * jax.nn.sigmoid(u[:, CONV_WIDTH:])


def _conv_kernel(prev_ref, cur_ref, next_ref, wdw_ref, bdw_ref, lng_ref, lnb_ref, o_ref, z_ref, *, tm, rc):
    i = pl.program_id(1)
    nt = pl.num_programs(1)
    z_ref[0:HALO] = jnp.where(i > 0, _glu(prev_ref[0]), 0.0)
    z_ref[HALO:HALO + tm] = _glu(cur_ref[0])
    z_ref[HALO + tm:HALO + tm + HALO] = jnp.where(i < nt - 1, _glu(next_ref[0]), 0.0)
    first_tap = HALO - CONV_K // 2

    for ci in range(tm // rc):
        base = ci * rc
        acc = jnp.broadcast_to(bdw_ref[...], (rc, CONV_WIDTH))
        for j in range(CONV_K):
            acc = acc + z_ref[pl.ds(base + (first_tap + j), rc), :] * wdw_ref[j:j + 1, :]
        mu = jnp.mean(acc, axis=-1, keepdims=True)
        cen = acc - mu
        var = jnp.mean(cen * cen, axis=-1, keepdims=True)
        y = cen * lax.rsqrt(var + EPS) * lng_ref[...] + lnb_ref[...]
        o_ref[0, pl.ds(base, rc), :] = (y * jax.nn.sigmoid(y)).astype(BF16)


def _conv_mix(pfc, w_dw, b_dw, ln_g, ln_b):
    bsz, n, _ = pfc.shape
    tm = _row_tile(n)
    rc = min(tm, 64)
    vec = lambda: pl.BlockSpec((1, CONV_WIDTH), lambda b, i: (0, 0))
    return pl.pallas_call(
        functools.partial(_conv_kernel, tm=tm, rc=rc),
        grid=(bsz, n // tm),
        in_specs=_halo_specs(tm, n, 2 * CONV_WIDTH, 1) + [
            pl.BlockSpec((CONV_K, CONV_WIDTH), lambda b, i: (0, 0)), vec(), vec(), vec()],
        out_specs=pl.BlockSpec((1, tm, CONV_WIDTH), lambda b, i: (b, i, 0)),
        out_shape=jax.ShapeDtypeStruct((bsz, n, CONV_WIDTH), BF16),
        scratch_shapes=[pltpu.VMEM((tm + 2 * HALO, CONV_WIDTH), F32)],
        compiler_params=_params(("arbitrary", "arbitrary"), 32),
        name="conv",
    )(pfc, pfc, pfc, w_dw, b_dw.reshape(1, -1), ln_g.reshape(1, -1), ln_b.reshape(1, -1))


def _channel_dft_mats(n_groups):
    c = np.arange(FOURIER_GROUP_DIM)
    ang = 2.0 * np.pi * ((c[:, None] * c[None, :]) % FOURIER_GROUP_DIM) / FOURIER_GROUP_DIM
    eye = np.eye(n_groups)
    return _bf16_const(np.kron(eye, np.cos(ang))), _bf16_const(np.kron(eye, np.sin(ang)))


def _fft_tables(n):
    assert n == FFT_N1 * FFT_N2
    k1 = np.arange(FFT_N1)
    n1 = np.arange(FFT_N1)
    n2 = np.arange(FFT_N2)
    prod = (k1[None, :, None] * (FFT_N2 * n1[None, None, :] + n2[:, None, None])) % n
    th = 2.0 * np.pi * prod / n
    cs, sn = np.cos(th), np.sin(th)
    m_big = np.concatenate([np.concatenate([cs, -sn], axis=2), np.concatenate([-sn, -cs], axis=2)], axis=1)
    k2 = np.arange(FFT_N2)
    psi = 2.0 * np.pi * ((k2[:, None] * n2[None, :]) % FFT_N2) / FFT_N2
    w_int = np.stack([np.cos(psi), np.sin(psi)], axis=2).reshape(FFT_N2, 2 * FFT_N2)
    return _bf16_const(m_big), _bf16_const(w_int)


def _fourier_kernel(x_ref, cc_ref, sc_ref, mbig_ref, wint_ref, o_ref, ab_ref, t_ref, *, n):
    xb = x_ref[0].astype(BF16)
    ab_ref[0] = jnp.dot(xb, cc_ref[...], preferred_element_type=F32)
    ab_ref[1] = jnp.dot(xb, sc_ref[...], preferred_element_type=F32)

    def stage1(n2, carry):
        za = ab_ref[0, pl.ds(n2, FFT_N1, stride=FFT_N2), :]
        zb = ab_ref[1, pl.ds(n2, FFT_N1, stride=FFT_N2), :]
        z = jnp.concatenate([za, zb], axis=0).astype(BF16)
        off = pl.multiple_of(n2 * (2 * FFT_N1), 2 * FFT_N1)
        t_ref[pl.ds(off, 2 * FFT_N1), :] = jnp.dot(mbig_ref[n2], z, preferred_element_type=F32)
        return carry

    lax.fori_loop(0, FFT_N2, stage1, 0)

    scale = 1.0 / np.sqrt(float(n * FOURIER_GROUP_DIM))

    def stage2(k1, carry):
        t = t_ref[pl.ds(k1, 2 * FFT_N2, stride=FFT_N1), :].astype(BF16)
        y = jnp.dot(wint_ref[...], t, preferred_element_type=F32) * scale
        o_ref[0, pl.ds(k1, FFT_N2, stride=FFT_N1), :] = y
        return carry

    lax.fori_loop(0, FFT_N1, stage2, 0)


def _fourier_mix(pfc):
    bsz, n, _ = pfc.shape
    cc, sc = _channel_dft_mats(LANES // FOURIER_GROUP_DIM)
    m_big, w_int = _fft_tables(n)
    col0 = POOL_WIDTH // LANES
    const2 = lambda b, h: (0, 0)
    return pl.pallas_call(
        functools.partial(_fourier_kernel, n=n),
        grid=(bsz, FOURIER_WIDTH // LANES),
        in_specs=[
            pl.BlockSpec((1, n, LANES), lambda b, h: (b, 0, col0 + h)),
            pl.BlockSpec((LANES, LANES), const2),
            pl.BlockSpec((LANES, LANES), const2),
            pl.BlockSpec(m_big.shape, lambda b, h: (0, 0, 0), pipeline_mode=pl.Buffered(1)),
            pl.BlockSpec(w_int.shape, const2),
        ],
        out_specs=pl.BlockSpec((1, n, LANES), lambda b, h: (b, 0, h)),
        out_shape=jax.ShapeDtypeStruct((bsz, n, FOURIER_WIDTH), F32),
        scratch_shapes=[pltpu.VMEM((2, n, LANES), F32), pltpu.VMEM((2 * n, LANES), F32)],
        compiler_params=_params(("arbitrary", "arbitrary"), 56),
        name="fourier",
    )(pfc, cc, sc, m_big, w_int)


def _fourier_small_kernel(x_ref, cc_ref, sc_ref, cn_ref, sn_ref, o_ref, *, n):
    xb = x_ref[0].astype(BF16)
    a = jnp.dot(xb, cc_ref[...], preferred_element_type=F32).astype(BF16)
    b = jnp.dot(xb, sc_ref[...], preferred_element_type=F32).astype(BF16)
    y = (jnp.dot(cn_ref[...], a, preferred_element_type=F32) - jnp.dot(sn_ref[...], b, preferred_element_type=F32))
    o_ref[0] = y * (1.0 / np.sqrt(float(n * FOURIER_GROUP_DIM)))


def _fourier_mix_small(pfc):
    bsz, n, _ = pfc.shape
    cc, sc = _channel_dft_mats(FOURIER_WIDTH // FOURIER_GROUP_DIM)
    k = np.arange(n)
    th = 2.0 * np.pi * ((k[:, None] * k[None, :]) % n) / n
    cn, sn = _bf16_const(np.cos(th)), _bf16_const(np.sin(th))
    const2 = lambda b: (0, 0)
    return pl.pallas_call(
        functools.partial(_fourier_small_kernel, n=n),
        grid=(bsz,),
        in_specs=[
            pl.BlockSpec((1, n, FOURIER_WIDTH), lambda b: (b, 0, POOL_WIDTH // FOURIER_WIDTH)),
            pl.BlockSpec((FOURIER_WIDTH, FOURIER_WIDTH), const2),
            pl.BlockSpec((FOURIER_WIDTH, FOURIER_WIDTH), const2),
            pl.BlockSpec((n, n), const2),
            pl.BlockSpec((n, n), const2),
        ],
        out_specs=pl.BlockSpec((1, n, FOURIER_WIDTH), lambda b: (b, 0, 0)),
        out_shape=jax.ShapeDtypeStruct((bsz, n, FOURIER_WIDTH), F32),
        compiler_params=_params(("arbitrary",), 32),
        name="fourier_small",
    )(pfc, cc, sc, cn, sn)


def _merge_kernel(attn_ref, pool_ref, four_ref, conv_ref, gate_ref, x_ref, gt_ref,
                  wa_ref, wp_ref, wf_ref, wc_ref, wo_ref, o_ref):
    d = D_MODEL
    branches = (
        (attn_ref[0], wa_ref),
        (pool_ref[0], wp_ref),
        (four_ref[0].astype(BF16), wf_ref),
        (conv_ref[0], wc_ref),
    )
    merged = None
    for bi, (act, w_ref) in enumerate(branches):
        y = jnp.dot(act, w_ref[...], preferred_element_type=F32)
        term = gate_ref[0, :, bi * d:(bi + 1) * d].astype(F32) * y
        merged = term if merged is None else merged + term
    out = jnp.dot(merged.astype(BF16), wo_ref[...], preferred_element_type=F32)
    o_ref[0] = x_ref[0] + gt_ref[0] * out


def _merge(attn, pool_o, four_o, conv_o, gate, x, gt, wa, wp, wf, wc, wo):
    bsz, n, d = x.shape
    tm = min(n, 512)
    act = lambda w: pl.BlockSpec((1, tm, w), lambda b, i: (b, i, 0))
    wsp = lambda w: pl.BlockSpec((w, d), lambda b, i: (0, 0))
    return pl.pallas_call(
        _merge_kernel,
        grid=(bsz, n // tm),
        in_specs=[act(NA_WIDTH), act(POOL_WIDTH), act(FOURIER_WIDTH), act(CONV_WIDTH), act(GATE_WIDTH), act(d),
                  pl.BlockSpec((1, 1, d), lambda b, i: (b, 0, 0)),
                  wsp(NA_WIDTH), wsp(POOL_WIDTH), wsp(FOURIER_WIDTH), wsp(CONV_WIDTH), wsp(d)],
        out_specs=act(d),
        out_shape=jax.ShapeDtypeStruct((bsz, n, d), F32),
        compiler_params=_params(("arbitrary", "arbitrary"), 48),
        name="merge",
    )(attn, pool_o, four_o, conv_o, gate, x, gt, wa, wp, wf, wc, wo)


def _mlp_kernel(x_ref, g_ref, sh_ref, sc_ref, gt_ref, w1_ref, w2_ref, gfin_ref, o_ref, h_ref, acc_ref,
                *, final_norm):
    f = pl.program_id(2)

    @pl.when(f == 0)
    def _():
        h_ref[...] = _norm_modulate(x_ref[0], g_ref[...], sh_ref[0], sc_ref[0]).astype(BF16)
        acc_ref[...] = jnp.zeros_like(acc_ref)

    a = jnp.dot(h_ref[...], w1_ref[...], preferred_element_type=F32)
    a = jnp.square(jnp.maximum(a, 0.0)).astype(BF16)
    acc_ref[...] += jnp.dot(a, w2_ref[...], preferred_element_type=F32)

    @pl.when(f == pl.num_programs(2) - 1)
    def _():
        y = x_ref[0] + gt_ref[0] * acc_ref[...]
        if final_norm:
            ms = jnp.mean(y * y, axis=-1, keepdims=True)
            y = y * lax.rsqrt(ms + EPS) * gfin_ref[...]
        o_ref[0] = y


def _mlp(x, g, shift, scale, gt, w1, w2, g_final, *, final_norm):
    bsz, n, d = x.shape
    tm = _row_tile(n)
    tf = 1024
    mod = lambda: pl.BlockSpec((1, 1, d), lambda b, i, f: (b, 0, 0))
    vec = lambda: pl.BlockSpec((1, d), lambda b, i, f: (0, 0))
    return pl.pallas_call(
        functools.partial(_mlp_kernel, final_norm=final_norm),
        grid=(bsz, n // tm, D_FF // tf),
        in_specs=[
            pl.BlockSpec((1, tm, d), lambda b, i, f: (b, i, 0)),
            vec(), mod(), mod(), mod(),
            pl.BlockSpec((d, tf), lambda b, i, f: (0, f)),
            pl.BlockSpec((tf, d), lambda b, i, f: (f, 0)),
            vec(),
        ],
        out_specs=pl.BlockSpec((1, tm, d), lambda b, i, f: (b, i, 0)),
        out_shape=jax.ShapeDtypeStruct((bsz, n, d), F32),
        scratch_shapes=[pltpu.VMEM((tm, d), BF16), pltpu.VMEM((tm, d), F32)],
        compiler_params=_params(("arbitrary", "arbitrary", "arbitrary"), 48),
        name="mlp",
    )(x, g.reshape(1, d), shift, scale, gt, w1, w2, g_final.reshape(1, d))


def _rope_tables(n):
    t = jnp.arange(n)
    row = (t // GRID_W).astype(F32)
    col = (t % GRID_W).astype(F32)
    inv = ROPE_THETA ** (-jnp.arange(0, ROPE_AXIS_DIM, 2, dtype=F32) / ROPE_AXIS_DIM)
    ang_r = row[:, None] * inv
    ang_c = col[:, None] * inv
    cr, sr, cc, sc = jnp.cos(ang_r), jnp.sin(ang_r), jnp.cos(ang_c), jnp.sin(ang_c)
    cos64 = jnp.concatenate([cr, cr, cc, cc], axis=1)
    sin64 = jnp.concatenate([-sr, sr, -sc, sc], axis=1)
    return jnp.tile(cos64, (1, LANES // HEAD_DIM)), jnp.tile(sin64, (1, LANES // HEAD_DIM))


def _pool_block_diag(w_pool_l):
    out = jnp.zeros((POOL_WIDTH, POOL_WIDTH), w_pool_l.dtype)
    for gi in range(POOL_GROUPS):
        s = gi * POOL_GROUP_DIM
        out = out.at[s:s + POOL_GROUP_DIM, s:s + POOL_GROUP_DIM].set(w_pool_l[gi])
    return out


def _mixer_tail(qkv_attn, pfc, gate, x, gt1, lw, *, small):
    pool_o = _pool_mix(pfc, lw["wbd"], lw["pool_scale"])
    four_o = _fourier_mix_small(pfc) if small else _fourier_mix(pfc)
    conv_o = _conv_mix(pfc, lw["w_dw"], lw["b_dw"], lw["ln_g"], lw["ln_b"])
    return _merge(qkv_attn, pool_o, four_o, conv_o, gate, x, gt1,
                  lw["wa"], lw["wp"], lw["wf"], lw["wc"], lw["wo"])


def kernel(x, c, ctx, c_ctx, w_mod, b_mod, g_mix, g_ff, w_in, rpb, w_pool, pool_scale, w_dw, b_dw, conv_ln_g,
           conv_ln_b, w_br_attn, w_br_pool, w_br_fourier, w_br_conv, w_out, w_ff1, w_ff2, g_final):
    bsz, n, d = x.shape
    n_ctx = ctx.shape[1]
    cos_t, sin_t = _rope_tables(n)
    ones_c = jnp.ones((n_ctx, LANES), F32)
    zeros_c = jnp.zeros((n_ctx, LANES), F32)

    rows = 8
    cvecs = jnp.concatenate([c, c_ctx[None, :], jnp.zeros((rows - bsz - 1, d), F32)], axis=0)
    ada = _ada_all(cvecs, w_mod, b_mod)

    h_ctx = ctx
    for l in range(DEPTH):
        last = l == DEPTH - 1
        lw = dict(
            wbd=_pool_block_diag(w_pool[l]).astype(BF16), pool_scale=pool_scale[l],
            w_dw=w_dw[l], b_dw=b_dw[l], ln_g=conv_ln_g[l], ln_b=conv_ln_b[l],
            wa=w_br_attn[l].astype(BF16), wp=w_br_pool[l].astype(BF16), wf=w_br_fourier[l].astype(BF16),
            wc=w_br_conv[l].astype(BF16), wo=w_out[l].astype(BF16),
        )
        w_in_bf = w_in[l].astype(BF16)
        w1_bf = w_ff1[l].astype(BF16)
        w2_bf = w_ff2[l].astype(BF16)
        lat = [ada[l, :bsz, k * d:(k + 1) * d].reshape(bsz, 1, d) for k in range(6)]
        cm = [jnp.broadcast_to(ada[l, bsz, k * d:(k + 1) * d].reshape(1, 1, d), (bsz, 1, d)) for k in range(6)]

        qkv_c, pfc_c, gate_c = _in_proj(h_ctx, g_mix[l], cm[0], cm[1], ones_c, zeros_c, w_in_bf, rope=False)
        if not last:
            attn_c = _ctx_attn(qkv_c)
            ctx_mid = _mixer_tail(attn_c, pfc_c, gate_c, h_ctx, cm[2], lw, small=True)
            ctx_next = _mlp(ctx_mid, g_ff[l], cm[3], cm[4], cm[5], w1_bf, w2_bf, g_final, final_norm=False)

        qkv, pfc, gate = _in_proj(x, g_mix[l], lat[0], lat[1], cos_t, sin_t, w_in_bf, rope=True)
        attn = _nattn(qkv, qkv_c, _attn_bias_table(rpb[l]))
        x = _mixer_tail(attn, pfc, gate, x, lat[2], lw, small=False)
        x = _mlp(x, g_ff[l], lat[3], lat[4], lat[5], w1_bf, w2_bf, g_final, final_norm=last)
        if not last:
            h_ctx = ctx_next
    return x
```

```python
import functools

import numpy as np
import jax
import jax.numpy as jnp
from jax import lax
from jax.experimental import pallas as pl
from jax.experimental.pallas import tpu as pltpu

F32 = jnp.float32
BF16 = jnp.bfloat16

D_MODEL = 1024
DEPTH = 4
GRID_W = 64
NA_HEADS = 8
HEAD_DIM = 64
NA_WIDTH = NA_HEADS * HEAD_DIM
NA_KH = 8
NA_KW = 16
ROPE_AXIS_DIM = HEAD_DIM // 2
ROPE_THETA = 10000.0
POOL_WIDTH = 256
POOL_GROUPS = 4
POOL_GROUP_DIM = 64
POOL_WINDOWS = (2, 4, 8, 16)
FOURIER_WIDTH = 256
FOURIER_GROUP_DIM = 64
CONV_WIDTH = 256
CONV_K = 31
N_BRANCH = 4
D_FF = 4 * D_MODEL
EPS = 1e-6
QKV_WIDTH = 3 * NA_WIDTH
PFC_WIDTH = POOL_WIDTH + FOURIER_WIDTH + 2 * CONV_WIDTH
GATE_WIDTH = N_BRANCH * D_MODEL
IN_WIDTH = QKV_WIDTH + PFC_WIDTH + GATE_WIDTH

LANES = 128
V7X_VMEM_BYTES = 64 * 1024 * 1024
HALO = 16
MASK_VALUE = -1e30

IN_TILE = 512
IN_ROW_CHUNK = 256
N_IN_TILES = IN_WIDTH // IN_TILE
N_QKV_TILES = QKV_WIDTH // IN_TILE
N_PFC_TILES = PFC_WIDTH // IN_TILE
N_GATE_TILES = GATE_WIDTH // IN_TILE

FFT_N1 = 128
FFT_N2 = 64


def _params(sem, vmem_mb):
    return pltpu.CompilerParams(dimension_semantics=sem, vmem_limit_bytes=vmem_mb * 1024 * 1024)


def _bf16_const(a):
    return jnp.asarray(a, F32).astype(BF16)


def _row_tile(n):
    return min(n, 1024)


def _ada_kernel(c_ref, w_ref, b_ref, o_ref):
    cv = c_ref[...]
    s = cv * jax.nn.sigmoid(cv)
    o_ref[0] = jnp.dot(s.astype(BF16), w_ref[0].astype(BF16), preferred_element_type=F32) + b_ref[0]


def _ada_all(cvecs, w_mod, b_mod):
    depth, d, width = w_mod.shape
    rows = cvecs.shape[0]
    tn = 1536
    return pl.pallas_call(
        _ada_kernel,
        grid=(depth, width // tn),
        in_specs=[
            pl.BlockSpec((rows, d), lambda l, j: (0, 0)),
            pl.BlockSpec((1, d, tn), lambda l, j: (l, 0, j)),
            pl.BlockSpec((1, 1, tn), lambda l, j: (l, 0, j)),
        ],
        out_specs=pl.BlockSpec((1, rows, tn), lambda l, j: (l, 0, j)),
        out_shape=jax.ShapeDtypeStruct((depth, rows, width), F32),
        compiler_params=_params(("arbitrary", "arbitrary"), 32),
        name="ada",
    )(cvecs, w_mod, b_mod.reshape(depth, 1, width))


def _norm_modulate(x, g, shift, scale):
    ms = jnp.mean(x * x, axis=-1, keepdims=True)
    y = x * lax.rsqrt(ms + EPS) * g
    return y * (1.0 + scale) + shift


def _in_proj_kernel(x_ref, g_ref, sh_ref, sc_ref, cos_ref, sin_ref, w_ref,
                    qkv_ref, pfc_ref, gate_ref, h_ref, *, rope):
    j = pl.program_id(2)

    @pl.when(j == 0)
    def _():
        h_ref[...] = _norm_modulate(x_ref[0], g_ref[...], sh_ref[0], sc_ref[0]).astype(BF16)

    tm = h_ref.shape[0]
    rc = min(tm, IN_ROW_CHUNK)

    def run(epilogue):
        def mm(c):
            return jnp.dot(h_ref[c * rc:(c + 1) * rc, :], w_ref[...], preferred_element_type=F32)

        pending = mm(0)
        for c in range(tm // rc):
            nxt = mm(c + 1) if c + 1 < tm // rc else None
            epilogue(slice(c * rc, (c + 1) * rc), pending)
            pending = nxt

    def qk_epilogue(rows, acc):
        mult = jnp.where(j == 0, HEAD_DIM ** -0.5, 1.0).astype(F32)
        if rope:
            cos = cos_ref[rows, :]
            sin = sin_ref[rows, :]
            lane = lax.broadcasted_iota(jnp.int32, cos.shape, 1)
            first = (lane % (ROPE_AXIS_DIM)) < (ROPE_AXIS_DIM // 2)
            for cc in range(IN_TILE // LANES):
                a = acc[:, cc * LANES:(cc + 1) * LANES]
                partner = jnp.where(first,
                                    pltpu.roll(a, LANES - ROPE_AXIS_DIM // 2, 1),
                                    pltpu.roll(a, ROPE_AXIS_DIM // 2, 1))
                qkv_ref[0, rows, cc * LANES:(cc + 1) * LANES] = ((a * cos + partner * sin) * mult).astype(BF16)
        else:
            qkv_ref[0, rows, :] = (acc * mult).astype(BF16)

    def v_epilogue(rows, acc):
        qkv_ref[0, rows, :] = acc.astype(BF16)

    def pfc_epilogue(rows, acc):
        pfc_ref[0, rows, :] = acc

    def gate_epilogue(rows, acc):
        gate_ref[0, rows, :] = jax.nn.sigmoid(acc).astype(BF16)

    pl.when(j < 2)(lambda: run(qk_epilogue))
    pl.when(j == 2)(lambda: run(v_epilogue))
    pl.when(jnp.logical_and(j >= N_QKV_TILES, j < N_QKV_TILES + N_PFC_TILES))(lambda: run(pfc_epilogue))
    pl.when(j >= N_QKV_TILES + N_PFC_TILES)(lambda: run(gate_epilogue))


def _in_proj(x, g, shift, scale, cos_t, sin_t, w_bf, *, rope):
    bsz, n, d = x.shape
    tm = _row_tile(n)
    grid = (bsz, n // tm, N_IN_TILES)
    qkv_map = lambda b, i, j: (b, i, jnp.minimum(j, N_QKV_TILES - 1))
    pfc_map = lambda b, i, j: (b, i, jnp.clip(j - N_QKV_TILES, 0, N_PFC_TILES - 1))
    gate_map = lambda b, i, j: (b, i, jnp.clip(j - N_QKV_TILES - N_PFC_TILES, 0, N_GATE_TILES - 1))
    return pl.pallas_call(
        functools.partial(_in_proj_kernel, rope=rope),
        grid=grid,
        in_specs=[
            pl.BlockSpec((1, tm, d), lambda b, i, j: (b, i, 0)),
            pl.BlockSpec((1, d), lambda b, i, j: (0, 0)),
            pl.BlockSpec((1, 1, d), lambda b, i, j: (b, 0, 0)),
            pl.BlockSpec((1, 1, d), lambda b, i, j: (b, 0, 0)),
            pl.BlockSpec((tm, LANES), lambda b, i, j: (i, 0)),
            pl.BlockSpec((tm, LANES), lambda b, i, j: (i, 0)),
            pl.BlockSpec((d, IN_TILE), lambda b, i, j: (0, j)),
        ],
        out_specs=[
            pl.BlockSpec((1, tm, IN_TILE), qkv_map),
            pl.BlockSpec((1, tm, IN_TILE), pfc_map),
            pl.BlockSpec((1, tm, IN_TILE), gate_map),
        ],
        out_shape=[
            jax.ShapeDtypeStruct((bsz, n, QKV_WIDTH), BF16),
            jax.ShapeDtypeStruct((bsz, n, PFC_WIDTH), F32),
            jax.ShapeDtypeStruct((bsz, n, GATE_WIDTH), BF16),
        ],
        scratch_shapes=[pltpu.VMEM((tm, d), BF16)],
        compiler_params=_params(("arbitrary", "arbitrary", "arbitrary"), 48),
        name="in_proj",
    )(x, g.reshape(1, d), shift, scale, cos_t, sin_t, w_bf)


def _stack_head_pair(qp):
    lane = lax.broadcasted_iota(jnp.int32, qp.shape, 1)
    zero = jnp.zeros_like(qp)
    return jnp.concatenate([jnp.where(lane < HEAD_DIM, qp, zero), jnp.where(lane < HEAD_DIM, zero, qp)], axis=0)


def _unstack_head_pair(o2):
    m = o2.shape[0] // 2
    lane = lax.broadcasted_iota(jnp.int32, (m, LANES), 1)
    return jnp.where(lane < HEAD_DIM, o2[:m], o2[m:])


_NT = (((1,), (1,)), ((), ()))


def _nattn_kernel(q_ref, k_ref, v_ref, kc_ref, vc_ref, bias_ref, o_ref, *, rows_per_step, n_rows):
    i = pl.program_id(1)
    win = NA_KH * GRID_W
    rows_per_iter = 4
    lookahead = 2

    def scores(rl, p):
        r = i * rows_per_step + rl
        rs = jnp.clip(r - NA_KH // 2, 0, n_rows - NA_KH)
        var = rs - r + (NA_KH - 1)
        qoff = pl.multiple_of(rl * GRID_W, GRID_W)
        koff = pl.multiple_of(rs * GRID_W, GRID_W)
        cs = slice(p * LANES, (p + 1) * LANES)
        q2 = _stack_head_pair(q_ref[0, pl.ds(qoff, GRID_W), cs])
        kp = k_ref[0, pl.ds(koff, win), cs]
        s_w = lax.dot_general(q2, kp, _NT, preferred_element_type=F32) + bias_ref[var, p]
        s_c = lax.dot_general(q2, kc_ref[0, :, cs], _NT, preferred_element_type=F32)
        return s_w, s_c, qoff, koff, cs

    def finish(s_w, s_c, qoff, koff, cs):
        m = jnp.maximum(jnp.max(s_w, axis=-1, keepdims=True), jnp.max(s_c, axis=-1, keepdims=True))
        p_w = jnp.exp(s_w - m)
        p_c = jnp.exp(s_c - m)
        l = jnp.sum(p_w, axis=-1, keepdims=True) + jnp.sum(p_c, axis=-1, keepdims=True)
        o2 = (jnp.dot(p_w.astype(BF16), v_ref[0, pl.ds(koff, win), cs], preferred_element_type=F32)
              + jnp.dot(p_c.astype(BF16), vc_ref[0, :, cs], preferred_element_type=F32))
        o2 = o2 / l
        o_ref[0, pl.ds(qoff, GRID_W), cs] = _unstack_head_pair(o2).astype(BF16)

    def body(it, carry):
        units = [(it * rows_per_iter + u, p) for u in range(rows_per_iter) for p in range(NA_HEADS // 2)]
        pending = [scores(*u) for u in units[:lookahead]]
        for idx in range(len(units)):
            if idx + lookahead < len(units):
                pending.append(scores(*units[idx + lookahead]))
            finish(*pending.pop(0))
        return carry

    lax.fori_loop(0, rows_per_step // rows_per_iter, body, 0)


def _nattn(qkv, qkv_ctx, bias_tbl):
    bsz, n, _ = qkv.shape
    n_ctx = qkv_ctx.shape[1]
    n_rows = n // GRID_W
    rows_per_step = 8
    tq = rows_per_step * GRID_W
    single = pl.Buffered(1)
    return pl.pallas_call(
        functools.partial(_nattn_kernel, rows_per_step=rows_per_step, n_rows=n_rows),
        grid=(bsz, n_rows // rows_per_step),
        in_specs=[
            pl.BlockSpec((1, tq, NA_WIDTH), lambda b, i: (b, i, 0)),
            pl.BlockSpec((1, n, NA_WIDTH), lambda b, i: (b, 0, 1), pipeline_mode=single),
            pl.BlockSpec((1, n, NA_WIDTH), lambda b, i: (b, 0, 2), pipeline_mode=single),
            pl.BlockSpec((1, n_ctx, NA_WIDTH), lambda b, i: (b, 0, 1)),
            pl.BlockSpec((1, n_ctx, NA_WIDTH), lambda b, i: (b, 0, 2)),
            pl.BlockSpec(bias_tbl.shape, lambda b, i: (0, 0, 0, 0), pipeline_mode=single),
        ],
        out_specs=pl.BlockSpec((1, tq, NA_WIDTH), lambda b, i: (b, i, 0)),
        out_shape=jax.ShapeDtypeStruct((bsz, n, NA_WIDTH), BF16),
        compiler_params=_params(("arbitrary", "arbitrary"), 48),
        name="nattn",
    )(qkv, qkv, qkv, qkv_ctx, qkv_ctx, bias_tbl)


def _ctx_attn_kernel(q_ref, k_ref, v_ref, o_ref):
    for p in range(NA_HEADS // 2):
        cs = slice(p * LANES, (p + 1) * LANES)
        q2 = _stack_head_pair(q_ref[0, :, cs])
        s = lax.dot_general(q2, k_ref[0, :, cs], _NT, preferred_element_type=F32)
        m = jnp.max(s, axis=-1, keepdims=True)
        e = jnp.exp(s - m)
        l = jnp.sum(e, axis=-1, keepdims=True)
        o2 = jnp.dot(e.astype(BF16), v_ref[0, :, cs], preferred_element_type=F32) / l
        o_ref[0, :, cs] = _unstack_head_pair(o2).astype(BF16)


def _ctx_attn(qkv_ctx):
    bsz, n_ctx, _ = qkv_ctx.shape
    return pl.pallas_call(
        _ctx_attn_kernel,
        grid=(bsz,),
        in_specs=[pl.BlockSpec((1, n_ctx, NA_WIDTH), lambda b, c=c: (b, 0, c)) for c in range(3)],
        out_specs=pl.BlockSpec((1, n_ctx, NA_WIDTH), lambda b: (b, 0, 0)),
        out_shape=jax.ShapeDtypeStruct((bsz, n_ctx, NA_WIDTH), BF16),
        compiler_params=_params(("arbitrary",), 32),
        name="ctx_attn",
    )(qkv_ctx, qkv_ctx, qkv_ctx)


def _attn_bias_table(rpb_l):
    cq = jnp.arange(GRID_W)
    kc = jnp.arange(GRID_W)
    col_start = jnp.clip(cq - NA_KW // 2, 0, GRID_W - NA_KW)
    valid = (kc[None, :] >= col_start[:, None]) & (kc[None, :] < col_start[:, None] + NA_KW)
    dcol = jnp.clip(kc[None, :] - cq[:, None] + (NA_KW - 1), 0, 2 * NA_KW - 2)
    plane = jnp.where(valid[None, None], rpb_l[:, :, dcol], MASK_VALUE)
    variants = []
    for vi in range(NA_KH):
        blk = plane[:, vi:vi + NA_KH]
        variants.append(blk.transpose(0, 2, 1, 3).reshape(NA_HEADS, GRID_W, NA_KH * GRID_W))
    tbl = jnp.stack(variants, axis=0)
    return tbl.reshape(NA_KH, NA_HEADS // 2, 2 * GRID_W, NA_KH * GRID_W).astype(F32)


def _halo_specs(tm, n, width, col_block):
    per = tm // HALO
    last = n // HALO - 1
    return [
        pl.BlockSpec((1, HALO, width), lambda b, i: (b, jnp.maximum(i * per - 1, 0), col_block)),
        pl.BlockSpec((1, tm, width), lambda b, i: (b, i, col_block)),
        pl.BlockSpec((1, HALO, width), lambda b, i: (b, jnp.minimum((i + 1) * per, last), col_block)),
    ]


def _pool_kernel(prev_ref, cur_ref, next_ref, wbd_ref, ps_ref, o_ref, ext_ref, *, tm, n_tok, rc):
    i = pl.program_id(1)
    nt = pl.num_programs(1)
    ext_ref[0:HALO] = jnp.where(i > 0, prev_ref[0], 0.0)
    ext_ref[HALO:HALO + tm] = cur_ref[0]
    ext_ref[HALO + tm:HALO + tm + HALO] = jnp.where(i < nt - 1, next_ref[0], 0.0)

    row = lax.broadcasted_iota(jnp.int32, (rc, LANES), 0)
    lane = lax.broadcasted_iota(jnp.int32, (rc, LANES), 1)
    first = lane < POOL_GROUP_DIM

    for ci in range(tm // rc):
        base = ci * rc
        t = i * tm + base + row
        halves = []
        for half in range(2):
            cs = slice(half * LANES, (half + 1) * LANES)
            w_a, w_b = POOL_WINDOWS[2 * half], POOL_WINDOWS[2 * half + 1]
            lo_a, lo_b = w_a // 2, w_b // 2
            hi_a, hi_b = w_a - lo_a - 1, w_b - lo_b - 1

            def ld(off, cs=cs):
                return ext_ref[pl.ds(base + (HALO + off), rc), cs]

            x0 = ld(0)
            s_a = x0
            for off in range(-lo_a, hi_a + 1):
                if off != 0:
                    s_a = s_a + ld(off)
            s_b = s_a
            for off in list(range(-lo_b, -lo_a)) + list(range(hi_a + 1, hi_b + 1)):
                s_b = s_b + ld(off)
            lo = jnp.where(first, lo_a, lo_b)
            hi = jnp.where(first, hi_a, hi_b)
            cnt = jnp.minimum(t + hi + 1, n_tok) - jnp.maximum(t - lo, 0)
            halves.append(jnp.where(first, s_a, s_b) / cnt.astype(F32) - x0)
        d = jnp.concatenate(halves, axis=1).astype(BF16)
        y = jnp.dot(d, wbd_ref[...], preferred_element_type=F32) * ps_ref[...]
        o_ref[0, pl.ds(base, rc), :] = y.astype(BF16)


def _pool_mix(pfc, wbd_bf, pool_scale):
    bsz, n, _ = pfc.shape
    tm = _row_tile(n)
    rc = min(tm, 128)
    return pl.pallas_call(
        functools.partial(_pool_kernel, tm=tm, n_tok=n, rc=rc),
        grid=(bsz, n // tm),
        in_specs=_halo_specs(tm, n, POOL_WIDTH, 0) + [
            pl.BlockSpec((POOL_WIDTH, POOL_WIDTH), lambda b, i: (0, 0)),
            pl.BlockSpec((1, POOL_WIDTH), lambda b, i: (0, 0)),
        ],
        out_specs=pl.BlockSpec((1, tm, POOL_WIDTH), lambda b, i: (b, i, 0)),
        out_shape=jax.ShapeDtypeStruct((bsz, n, POOL_WIDTH), BF16),
        scratch_shapes=[pltpu.VMEM((tm + 2 * HALO, POOL_WIDTH), F32)],
        compiler_params=_params(("arbitrary", "arbitrary"), 32),
        name="pool",
    )(pfc, pfc, pfc, wbd_bf, pool_scale.reshape(1, POOL_WIDTH))


def _glu(u):
    return u[:, :CONV_WIDTH] * jax.nn.sigmoid(u[:, CONV_WIDTH:])


def _conv_kernel(prev_ref, cur_ref, next_ref, wdw_ref, bdw_ref, lng_ref, lnb_ref, o_ref, z_ref, *, tm, rc):
    i = pl.program_id(1)
    nt = pl.num_programs(1)
    z_ref[0:HALO] = jnp.where(i > 0, _glu(prev_ref[0]), 0.0)
    z_ref[HALO:HALO + tm] = _glu(cur_ref[0])
    z_ref[HALO + tm:HALO + tm + HALO] = jnp.where(i < nt - 1, _glu(next_ref[0]), 0.0)
    first_tap = HALO - CONV_K // 2

    for ci in range(tm // rc):
        base = ci * rc
        acc = jnp.broadcast_to(bdw_ref[...], (rc, CONV_WIDTH))
        for j in range(CONV_K):
            acc = acc + z_ref[pl.ds(base + (first_tap + j), rc), :] * wdw_ref[j:j + 1, :]
        mu = jnp.mean(acc, axis=-1, keepdims=True)
        cen = acc - mu
        var = jnp.mean(cen * cen, axis=-1, keepdims=True)
        y = cen * lax.rsqrt(var + EPS) * lng_ref[...] + lnb_ref[...]
        o_ref[0, pl.ds(base, rc), :] = (y * jax.nn.sigmoid(y)).astype(BF16)


def _conv_mix(pfc, w_dw, b_dw, ln_g, ln_b):
    bsz, n, _ = pfc.shape
    tm = _row_tile(n)
    rc = min(tm, 64)
    vec = lambda: pl.BlockSpec((1, CONV_WIDTH), lambda b, i: (0, 0))
    return pl.pallas_call(
        functools.partial(_conv_kernel, tm=tm, rc=rc),
        grid=(bsz, n // tm),
        in_specs=_halo_specs(tm, n, 2 * CONV_WIDTH, 1) + [
            pl.BlockSpec((CONV_K, CONV_WIDTH), lambda b, i: (0, 0)), vec(), vec(), vec()],
        out_specs=pl.BlockSpec((1, tm, CONV_WIDTH), lambda b, i: (b, i, 0)),
        out_shape=jax.ShapeDtypeStruct((bsz, n, CONV_WIDTH), BF16),
        scratch_shapes=[pltpu.VMEM((tm + 2 * HALO, CONV_WIDTH), F32)],
        compiler_params=_params(("arbitrary", "arbitrary"), 32),
        name="conv",
    )(pfc, pfc, pfc, w_dw, b_dw.reshape(1, -1), ln_g.reshape(1, -1), ln_b.reshape(1, -1))


def _channel_dft_mats(n_groups):
    c = np.arange(FOURIER_GROUP_DIM)
    ang = 2.0 * np.pi * ((c[:, None] * c[None, :]) % FOURIER_GROUP_DIM) / FOURIER_GROUP_DIM
    eye = np.eye(n_groups)
    return _bf16_const(np.kron(eye, np.cos(ang))), _bf16_const(np.kron(eye, np.sin(ang)))


def _fft_tables(n):
    assert n == FFT_N1 * FFT_N2
    k1 = np.arange(FFT_N1)
    n1 = np.arange(FFT_N1)
    n2 = np.arange(FFT_N2)
    prod = (k1[None, :, None] * (FFT_N2 * n1[None, None, :] + n2[:, None, None])) % n
    th = 2.0 * np.pi * prod / n
    cs, sn = np.cos(th), np.sin(th)
    m_big = np.concatenate([np.concatenate([cs, -sn], axis=2), np.concatenate([-sn, -cs], axis=2)], axis=1)
    k2 = np.arange(FFT_N2)
    psi = 2.0 * np.pi * ((k2[:, None] * n2[None, :]) % FFT_N2) / FFT_N2
    w_int = np.stack([np.cos(psi), np.sin(psi)], axis=2).reshape(FFT_N2, 2 * FFT_N2)
    return _bf16_const(m_big), _bf16_const(w_int)


def _fourier_kernel(x_ref, cc_ref, sc_ref, mbig_ref, wint_ref, o_ref, ab_ref, t_ref, *, n):
    xb = x_ref[0].astype(BF16)
    ab_ref[0] = jnp.dot(xb, cc_ref[...], preferred_element_type=F32)
    ab_ref[1] = jnp.dot(xb, sc_ref[...], preferred_element_type=F32)

    def stage1(n2, carry):
        za = ab_ref[0, pl.ds(n2, FFT_N1, stride=FFT_N2), :]
        zb = ab_ref[1, pl.ds(n2, FFT_N1, stride=FFT_N2), :]
        z = jnp.concatenate([za, zb], axis=0).astype(BF16)
        off = pl.multiple_of(n2 * (2 * FFT_N1), 2 * FFT_N1)
        t_ref[pl.ds(off, 2 * FFT_N1), :] = jnp.dot(mbig_ref[n2], z, preferred_element_type=F32)
        return carry

    lax.fori_loop(0, FFT_N2, stage1, 0)

    scale = 1.0 / np.sqrt(float(n * FOURIER_GROUP_DIM))

    def stage2(k1, carry):
        t = t_ref[pl.ds(k1, 2 * FFT_N2, stride=FFT_N1), :].astype(BF16)
        y = jnp.dot(wint_ref[...], t, preferred_element_type=F32) * scale
        o_ref[0, pl.ds(k1, FFT_N2, stride=FFT_N1), :] = y
        return carry

    lax.fori_loop(0, FFT_N1, stage2, 0)


def _fourier_mix(pfc):
    bsz, n, _ = pfc.shape
    cc, sc = _channel_dft_mats(LANES // FOURIER_GROUP_DIM)
    m_big, w_int = _fft_tables(n)
    col0 = POOL_WIDTH // LANES
    const2 = lambda b, h: (0, 0)
    return pl.pallas_call(
        functools.partial(_fourier_kernel, n=n),
        grid=(bsz, FOURIER_WIDTH // LANES),
        in_specs=[
            pl.BlockSpec((1, n, LANES), lambda b, h: (b, 0, col0 + h)),
            pl.BlockSpec((LANES, LANES), const2),
            pl.BlockSpec((LANES, LANES), const2),
            pl.BlockSpec(m_big.shape, lambda b, h: (0, 0, 0), pipeline_mode=pl.Buffered(1)),
            pl.BlockSpec(w_int.shape, const2),
        ],
        out_specs=pl.BlockSpec((1, n, LANES), lambda b, h: (b, 0, h)),
        out_shape=jax.ShapeDtypeStruct((bsz, n, FOURIER_WIDTH), F32),
        scratch_shapes=[pltpu.VMEM((2, n, LANES), F32), pltpu.VMEM((2 * n, LANES), F32)],
        compiler_params=_params(("arbitrary", "arbitrary"), 56),
        name="fourier",
    )(pfc, cc, sc, m_big, w_int)


def _fourier_small_kernel(x_ref, cc_ref, sc_ref, cn_ref, sn_ref, o_ref, *, n):
    xb = x_ref[0].astype(BF16)
    a = jnp.dot(xb, cc_ref[...], preferred_element_type=F32).astype(BF16)
    b = jnp.dot(xb, sc_ref[...], preferred_element_type=F32).astype(BF16)
    y = (jnp.dot(cn_ref[...], a, preferred_element_type=F32) - jnp.dot(sn_ref[...], b, preferred_element_type=F32))
    o_ref[0] = y * (1.0 / np.sqrt(float(n * FOURIER_GROUP_DIM)))


def _fourier_mix_small(pfc):
    bsz, n, _ = pfc.shape
    cc, sc = _channel_dft_mats(FOURIER_WIDTH // FOURIER_GROUP_DIM)
    k = np.arange(n)
    th = 2.0 * np.pi * ((k[:, None] * k[None, :]) % n) / n
    cn, sn = _bf16_const(np.cos(th)), _bf16_const(np.sin(th))
    const2 = lambda b: (0, 0)
    return pl.pallas_call(
        functools.partial(_fourier_small_kernel, n=n),
        grid=(bsz,),
        in_specs=[
            pl.BlockSpec((1, n, FOURIER_WIDTH), lambda b: (b, 0, POOL_WIDTH // FOURIER_WIDTH)),
            pl.BlockSpec((FOURIER_WIDTH, FOURIER_WIDTH), const2),
            pl.BlockSpec((FOURIER_WIDTH, FOURIER_WIDTH), const2),
            pl.BlockSpec((n, n), const2),
            pl.BlockSpec((n, n), const2),
        ],
        out_specs=pl.BlockSpec((1, n, FOURIER_WIDTH), lambda b: (b, 0, 0)),
        out_shape=jax.ShapeDtypeStruct((bsz, n, FOURIER_WIDTH), F32),
        compiler_params=_params(("arbitrary",), 32),
        name="fourier_small",
    )(pfc, cc, sc, cn, sn)


def _merge_kernel(attn_ref, pool_ref, four_ref, conv_ref, gate_ref, x_ref, gt_ref,
                  wa_ref, wp_ref, wf_ref, wc_ref, wo_ref, o_ref):
    d = D_MODEL
    branches = (
        (attn_ref[0], wa_ref),
        (pool_ref[0], wp_ref),
        (four_ref[0].astype(BF16), wf_ref),
        (conv_ref[0], wc_ref),
    )
    merged = None
    for bi, (act, w_ref) in enumerate(branches):
        y = jnp.dot(act, w_ref[...], preferred_element_type=F32)
        term = gate_ref[0, :, bi * d:(bi + 1) * d].astype(F32) * y
        merged = term if merged is None else merged + term
    out = jnp.dot(merged.astype(BF16), wo_ref[...], preferred_element_type=F32)
    o_ref[0] = x_ref[0] + gt_ref[0] * out


def _merge(attn, pool_o, four_o, conv_o, gate, x, gt, wa, wp, wf, wc, wo):
    bsz, n, d = x.shape
    tm = min(n, 512)
    act = lambda w: pl.BlockSpec((1, tm, w), lambda b, i: (b, i, 0))
    wsp = lambda w: pl.BlockSpec((w, d), lambda b, i: (0, 0))
    return pl.pallas_call(
        _merge_kernel,
        grid=(bsz, n // tm),
        in_specs=[act(NA_WIDTH), act(POOL_WIDTH), act(FOURIER_WIDTH), act(CONV_WIDTH), act(GATE_WIDTH), act(d),
                  pl.BlockSpec((1, 1, d), lambda b, i: (b, 0, 0)),
                  wsp(NA_WIDTH), wsp(POOL_WIDTH), wsp(FOURIER_WIDTH), wsp(CONV_WIDTH), wsp(d)],
        out_specs=act(d),
        out_shape=jax.ShapeDtypeStruct((bsz, n, d), F32),
        compiler_params=_params(("arbitrary", "arbitrary"), 48),
        name="merge",
    )(attn, pool_o, four_o, conv_o, gate, x, gt, wa, wp, wf, wc, wo)


def _mlp_kernel(x_ref, g_ref, sh_ref, sc_ref, gt_ref, w1_ref, w2_ref, gfin_ref, o_ref, h_ref, acc_ref,
                *, final_norm):
    f = pl.program_id(2)

    @pl.when(f == 0)
    def _():
        h_ref[...] = _norm_modulate(x_ref[0], g_ref[...], sh_ref[0], sc_ref[0]).astype(BF16)
        acc_ref[...] = jnp.zeros_like(acc_ref)

    a = jnp.dot(h_ref[...], w1_ref[...], preferred_element_type=F32)
    a = jnp.square(jnp.maximum(a, 0.0)).astype(BF16)
    acc_ref[...] += jnp.dot(a, w2_ref[...], preferred_element_type=F32)

    @pl.when(f == pl.num_programs(2) - 1)
    def _():
        y = x_ref[0] + gt_ref[0] * acc_ref[...]
        if final_norm:
            ms = jnp.mean(y * y, axis=-1, keepdims=True)
            y = y * lax.rsqrt(ms + EPS) * gfin_ref[...]
        o_ref[0] = y


def _mlp(x, g, shift, scale, gt, w1, w2, g_final, *, final_norm):
    bsz, n, d = x.shape
    tm = _row_tile(n)
    tf = 1024
    mod = lambda: pl.BlockSpec((1, 1, d), lambda b, i, f: (b, 0, 0))
    vec = lambda: pl.BlockSpec((1, d), lambda b, i, f: (0, 0))
    return pl.pallas_call(
        functools.partial(_mlp_kernel, final_norm=final_norm),
        grid=(bsz, n // tm, D_FF // tf),
        in_specs=[
            pl.BlockSpec((1, tm, d), lambda b, i, f: (b, i, 0)),
            vec(), mod(), mod(), mod(),
            pl.BlockSpec((d, tf), lambda b, i, f: (0, f)),
            pl.BlockSpec((tf, d), lambda b, i, f: (f, 0)),
            vec(),
        ],
        out_specs=pl.BlockSpec((1, tm, d), lambda b, i, f: (b, i, 0)),
        out_shape=jax.ShapeDtypeStruct((bsz, n, d), F32),
        scratch_shapes=[pltpu.VMEM((tm, d), BF16), pltpu.VMEM((tm, d), F32)],
        compiler_params=_params(("arbitrary", "arbitrary", "arbitrary"), 48),
        name="mlp",
    )(x, g.reshape(1, d), shift, scale, gt, w1, w2, g_final.reshape(1, d))


def _rope_tables(n):
    t = jnp.arange(n)
    row = (t // GRID_W).astype(F32)
    col = (t % GRID_W).astype(F32)
    inv = ROPE_THETA ** (-jnp.arange(0, ROPE_AXIS_DIM, 2, dtype=F32) / ROPE_AXIS_DIM)
    ang_r = row[:, None] * inv
    ang_c = col[:, None] * inv
    cr, sr, cc, sc = jnp.cos(ang_r), jnp.sin(ang_r), jnp.cos(ang_c), jnp.sin(ang_c)
    cos64 = jnp.concatenate([cr, cr, cc, cc], axis=1)
    sin64 = jnp.concatenate([-sr, sr, -sc, sc], axis=1)
    return jnp.tile(cos64, (1, LANES // HEAD_DIM)), jnp.tile(sin64, (1, LANES // HEAD_DIM))


def _pool_block_diag(w_pool_l):
    out = jnp.zeros((POOL_WIDTH, POOL_WIDTH), w_pool_l.dtype)
    for gi in range(POOL_GROUPS):
        s = gi * POOL_GROUP_DIM
        out = out.at[s:s + POOL_GROUP_DIM, s:s + POOL_GROUP_DIM].set(w_pool_l[gi])
    return out


def _mixer_tail(qkv_attn, pfc, gate, x, gt1, lw, *, small):
    pool_o = _pool_mix(pfc, lw["wbd"], lw["pool_scale"])
    four_o = _fourier_mix_small(pfc) if small else _fourier_mix(pfc)
    conv_o = _conv_mix(pfc, lw["w_dw"], lw["b_dw"], lw["ln_g"], lw["ln_b"])
    return _merge(qkv_attn, pool_o, four_o, conv_o, gate, x, gt1,
                  lw["wa"], lw["wp"], lw["wf"], lw["wc"], lw["wo"])


def kernel(x, c, ctx, c_ctx, w_mod, b_mod, g_mix, g_ff, w_in, rpb, w_pool, pool_scale, w_dw, b_dw, conv_ln_g,
           conv_ln_b, w_br_attn, w_br_pool, w_br_fourier, w_br_conv, w_out, w_ff1, w_ff2, g_final):
    bsz, n, d = x.shape
    n_ctx = ctx.shape[1]
    cos_t, sin_t = _rope_tables(n)
    ones_c = jnp.ones((n_ctx, LANES), F32)
    zeros_c = jnp.zeros((n_ctx, LANES), F32)

    rows = 8
    cvecs = jnp.concatenate([c, c_ctx[None, :], jnp.zeros((rows - bsz - 1, d), F32)], axis=0)
    ada = _ada_all(cvecs, w_mod, b_mod)

    h_ctx = ctx
    for l in range(DEPTH):
        last = l == DEPTH - 1
        lw = dict(
            wbd=_pool_block_diag(w_pool[l]).astype(BF16), pool_scale=pool_scale[l],
            w_dw=w_dw[l], b_dw=b_dw[l], ln_g=conv_ln_g[l], ln_b=conv_ln_b[l],
            wa=w_br_attn[l].astype(BF16), wp=w_br_pool[l].astype(BF16), wf=w_br_fourier[l].astype(BF16),
            wc=w_br_conv[l].astype(BF16), wo=w_out[l].astype(BF16),
        )
        w_in_bf = w_in[l].astype(BF16)
        w1_bf = w_ff1[l].astype(BF16)
        w2_bf = w_ff2[l].astype(BF16)
        lat = [ada[l, :bsz, k * d:(k + 1) * d].reshape(bsz, 1, d) for k in range(6)]
        cm = [jnp.broadcast_to(ada[l, bsz, k * d:(k + 1) * d].reshape(1, 1, d), (bsz, 1, d)) for k in range(6)]

        qkv_c, pfc_c, gate_c = _in_proj(h_ctx, g_mix[l], cm[0], cm[1], ones_c, zeros_c, w_in_bf, rope=False)
        if not last:
            attn_c = _ctx_attn(qkv_c)
            ctx_mid = _mixer_tail(attn_c, pfc_c, gate_c, h_ctx, cm[2], lw, small=True)
            ctx_next = _mlp(ctx_mid, g_ff[l], cm[3], cm[4], cm[5], w1_bf, w2_bf, g_final, final_norm=False)

        qkv, pfc, gate = _in_proj(x, g_mix[l], lat[0], lat[1], cos_t, sin_t, w_in_bf, rope=True)
        attn = _nattn(qkv, qkv_c, _attn_bias_table(rpb[l]))
        x = _mixer_tail(attn, pfc, gate, x, lat[2], lw, small=False)
        x = _mlp(x, g_ff[l], lat[3], lat[4], lat[5], w1_bf, w2_bf, g_final, final_norm=last)
        if not last:
            h_ctx = ctx_next
    return x
```

```python
import functools

import numpy as np
import jax
import jax.numpy as jnp
from jax import lax
from jax.experimental import pallas as pl
from jax.experimental.pallas import tpu as pltpu

F32 = jnp.float32
BF16 = jnp.bfloat16

D_MODEL = 1024
DEPTH = 4
GRID_W = 64
NA_HEADS = 8
HEAD_DIM = 64
NA_WIDTH = NA_HEADS * HEAD_DIM
NA_KH = 8
NA_KW = 16
ROPE_AXIS_DIM = HEAD_DIM // 2
ROPE_THETA = 10000.0
POOL_WIDTH = 256
POOL_GROUPS = 4
POOL_GROUP_DIM = 64
POOL_WINDOWS = (2, 4, 8, 16)
FOURIER_WIDTH = 256
FOURIER_GROUP_DIM = 64
CONV_WIDTH = 256
CONV_K = 31
N_BRANCH = 4
D_FF = 4 * D_MODEL
EPS = 1e-6
QKV_WIDTH = 3 * NA_WIDTH
PFC_WIDTH = POOL_WIDTH + FOURIER_WIDTH + 2 * CONV_WIDTH
GATE_WIDTH = N_BRANCH * D_MODEL
IN_WIDTH = QKV_WIDTH + PFC_WIDTH + GATE_WIDTH

LANES = 128
SUBLANES = 8
HALO = 16
MASK_VALUE = -1e30

IN_TILE = 512
N_IN_TILES = IN_WIDTH // IN_TILE
N_QKV_TILES = QKV_WIDTH // IN_TILE
N_PFC_TILES = PFC_WIDTH // IN_TILE

FFT_N1 = 128
FFT_N2 = 64


def _params(sem, vmem_mb):
    return pltpu.CompilerParams(dimension_semantics=sem, vmem_limit_bytes=vmem_mb * 1024 * 1024)


def _bf16_const(a):
    return jnp.asarray(a, F32).astype(BF16)


def _layer_spec(arr, l):
    nd = arr.ndim - 1
    return pl.BlockSpec((1,) + arr.shape[1:], lambda *_: (l,) + (0,) * nd)


def _ada_spec(ada, l, chunk):
    return pl.BlockSpec((1, ada.shape[1], D_MODEL), lambda *_: (l, 0, chunk))


def _mod_row(ada_ref, mod_row):
    row = pl.program_id(0) if mod_row is None else mod_row
    return ada_ref[0, pl.ds(row, 1), :]


def _ada_kernel(c_ref, w_ref, b_ref, o_ref):
    cv = c_ref[...]
    s = cv * jax.nn.sigmoid(cv)
    o_ref[0] = jnp.dot(s.astype(BF16), w_ref[0].astype(BF16), preferred_element_type=F32) + b_ref[0]


def _ada_all(cvecs, w_mod, b_mod):
    depth, d, width = w_mod.shape
    rows = cvecs.shape[0]
    tn = 1536
    return pl.pallas_call(
        _ada_kernel,
        grid=(depth, width // tn),
        in_specs=[
            pl.BlockSpec((rows, d), lambda l, j: (0, 0)),
            pl.BlockSpec((1, d, tn), lambda l, j: (l, 0, j)),
            pl.BlockSpec((1, 1, tn), lambda l, j: (l, 0, j)),
        ],
        out_specs=pl.BlockSpec((1, rows, tn), lambda l, j: (l, 0, j)),
        out_shape=jax.ShapeDtypeStruct((depth, rows, width), F32),
        compiler_params=_params(("arbitrary", "arbitrary"), 32),
        name="ada",
    )(cvecs, w_mod, b_mod.reshape(depth, 1, width))


def _norm_modulate(x, g, shift, scale):
    ms = jnp.mean(x * x, axis=-1, keepdims=True)
    y = x * lax.rsqrt(ms + EPS) * g
    return y * (1.0 + scale) + shift


def _in_proj_kernel(x_ref, g_ref, sh_ref, sc_ref, cos_ref, sin_ref, w_ref,
                    qkv_ref, pfc_ref, gate_ref, h_ref, *, rope, mod_row):
    h_ref[...] = _norm_modulate(x_ref[0], g_ref[0], _mod_row(sh_ref, mod_row),
                                _mod_row(sc_ref, mod_row)).astype(BF16)

    def mm(t):
        return jnp.dot(h_ref[...], w_ref[0, t], preferred_element_type=F32)

    def qk_epilogue(t, acc):
        col0 = t * IN_TILE
        mult = HEAD_DIM ** -0.5 if t == 0 else None
        if rope:
            cos = cos_ref[...]
            sin = sin_ref[...]
            if mult is not None:
                cos, sin = cos * mult, sin * mult
            lane = lax.broadcasted_iota(jnp.int32, cos.shape, 1)
            first = (lane % (ROPE_AXIS_DIM)) < (ROPE_AXIS_DIM // 2)
            for cc in range(IN_TILE // LANES):
                a = acc[:, cc * LANES:(cc + 1) * LANES]
                partner = jnp.where(first,
                                    pltpu.roll(a, LANES - ROPE_AXIS_DIM // 2, 1),
                                    pltpu.roll(a, ROPE_AXIS_DIM // 2, 1))
                qkv_ref[0, :, col0 + cc * LANES:col0 + (cc + 1) * LANES] = (a * cos + partner * sin).astype(BF16)
        else:
            qkv_ref[0, :, col0:col0 + IN_TILE] = (acc if mult is None else acc * mult).astype(BF16)

    def epilogue(t, acc):
        if t < 2:
            qk_epilogue(t, acc)
        elif t < N_QKV_TILES:
            qkv_ref[0, :, t * IN_TILE:(t + 1) * IN_TILE] = acc.astype(BF16)
        elif t < N_QKV_TILES + N_PFC_TILES:
            c0 = (t - N_QKV_TILES) * IN_TILE
            pfc_ref[0, :, c0:c0 + IN_TILE] = acc
        else:
            c0 = (t - N_QKV_TILES - N_PFC_TILES) * IN_TILE
            gate_ref[0, :, c0:c0 + IN_TILE] = jax.nn.sigmoid(acc).astype(BF16)

    pending = mm(0)
    for t in range(N_IN_TILES):
        nxt = mm(t + 1) if t + 1 < N_IN_TILES else None
        epilogue(t, pending)
        pending = nxt


def _in_proj(x, g_all, ada, l, cos_t, sin_t, w_tiles, *, rope, mod_row):
    bsz, n, d = x.shape
    tm = min(n, 512)
    return pl.pallas_call(
        functools.partial(_in_proj_kernel, rope=rope, mod_row=mod_row),
        grid=(bsz, n // tm),
        in_specs=[
            pl.BlockSpec((1, tm, d), lambda b, i: (b, i, 0)),
            _layer_spec(g_all, l),
            _ada_spec(ada, l, 0),
            _ada_spec(ada, l, 1),
            pl.BlockSpec((tm, LANES), lambda b, i: (i, 0)),
            pl.BlockSpec((tm, LANES), lambda b, i: (i, 0)),
            pl.BlockSpec((1,) + w_tiles.shape[1:], lambda b, i: (l, 0, 0, 0), pipeline_mode=pl.Buffered(1)),
        ],
        out_specs=[
            pl.BlockSpec((1, tm, QKV_WIDTH), lambda b, i: (b, i, 0)),
            pl.BlockSpec((1, tm, PFC_WIDTH), lambda b, i: (b, i, 0)),
            pl.BlockSpec((1, tm, GATE_WIDTH), lambda b, i: (b, i, 0)),
        ],
        out_shape=[
            jax.ShapeDtypeStruct((bsz, n, QKV_WIDTH), BF16),
            jax.ShapeDtypeStruct((bsz, n, PFC_WIDTH), F32),
            jax.ShapeDtypeStruct((bsz, n, GATE_WIDTH), BF16),
        ],
        scratch_shapes=[pltpu.VMEM((tm, d), BF16)],
        compiler_params=_params(("arbitrary", "arbitrary"), 52),
        name="in_proj",
    )(x, g_all, ada, ada, cos_t, sin_t, w_tiles)


def _stack_head_pair(qp):
    lane = lax.broadcasted_iota(jnp.int32, qp.shape, 1)
    zero = jnp.zeros_like(qp)
    return jnp.concatenate([jnp.where(lane < HEAD_DIM, qp, zero), jnp.where(lane < HEAD_DIM, zero, qp)], axis=0)


def _unstack_head_pair(o2):
    m = o2.shape[0] // 2
    lane = lax.broadcasted_iota(jnp.int32, (m, LANES), 1)
    return jnp.where(lane < HEAD_DIM, o2[:m], o2[m:])


_NT = (((1,), (1,)), ((), ()))


def _nattn_kernel(q_ref, k_ref, v_ref, kc_ref, vc_ref, bias_ref, o_ref, *, rows_per_step, n_rows):
    i = pl.program_id(1)
    win = NA_KH * GRID_W
    rows_per_iter = 4
    lookahead = 2

    def scores(rl, p):
        r = i * rows_per_step + rl
        rs = jnp.clip(r - NA_KH // 2, 0, n_rows - NA_KH)
        var = rs - r + (NA_KH - 1)
        qoff = pl.multiple_of(rl * GRID_W, GRID_W)
        koff = pl.multiple_of(rs * GRID_W, GRID_W)
        cs = slice(p * LANES, (p + 1) * LANES)
        q2 = _stack_head_pair(q_ref[0, pl.ds(qoff, GRID_W), cs])
        kp = k_ref[0, pl.ds(koff, win), cs]
        s_w = lax.dot_general(q2, kp, _NT, preferred_element_type=F32) + bias_ref[0, var, p]
        s_c = lax.dot_general(q2, kc_ref[0, :, cs], _NT, preferred_element_type=F32)
        return s_w, s_c, qoff, koff, cs

    def finish(s_w, s_c, qoff, koff, cs):
        m = jnp.maximum(jnp.max(s_w, axis=-1, keepdims=True), jnp.max(s_c, axis=-1, keepdims=True))
        p_w = jnp.exp(s_w - m)
        p_c = jnp.exp(s_c - m)
        l = jnp.sum(p_w, axis=-1, keepdims=True) + jnp.sum(p_c, axis=-1, keepdims=True)
        o2 = (jnp.dot(p_w.astype(BF16), v_ref[0, pl.ds(koff, win), cs], preferred_element_type=F32)
              + jnp.dot(p_c.astype(BF16), vc_ref[0, :, cs], preferred_element_type=F32))
        o2 = o2 / l
        o_ref[0, pl.ds(qoff, GRID_W), cs] = _unstack_head_pair(o2).astype(BF16)

    def body(it, carry):
        units = [(it * rows_per_iter + u, p) for u in range(rows_per_iter) for p in range(NA_HEADS // 2)]
        pending = [scores(*u) for u in units[:lookahead]]
        for idx in range(len(units)):
            if idx + lookahead < len(units):
                pending.append(scores(*units[idx + lookahead]))
            finish(*pending.pop(0))
        return carry

    lax.fori_loop(0, rows_per_step // rows_per_iter, body, 0)


def _nattn(qkv, qkv_ctx, bias_all, l):
    bsz, n, _ = qkv.shape
    n_ctx = qkv_ctx.shape[1]
    n_rows = n // GRID_W
    rows_per_step = 8
    tq = rows_per_step * GRID_W
    single = pl.Buffered(1)
    return pl.pallas_call(
        functools.partial(_nattn_kernel, rows_per_step=rows_per_step, n_rows=n_rows),
        grid=(bsz, n_rows // rows_per_step),
        in_specs=[
            pl.BlockSpec((1, tq, NA_WIDTH), lambda b, i: (b, i, 0)),
            pl.BlockSpec((1, n, NA_WIDTH), lambda b, i: (b, 0, 1), pipeline_mode=single),
            pl.BlockSpec((1, n, NA_WIDTH), lambda b, i: (b, 0, 2), pipeline_mode=single),
            pl.BlockSpec((1, n_ctx, NA_WIDTH), lambda b, i: (b, 0, 1)),
            pl.BlockSpec((1, n_ctx, NA_WIDTH), lambda b, i: (b, 0, 2)),
            pl.BlockSpec((1,) + bias_all.shape[1:], lambda b, i: (l, 0, 0, 0, 0), pipeline_mode=single),
        ],
        out_specs=pl.BlockSpec((1, tq, NA_WIDTH), lambda b, i: (b, i, 0)),
        out_shape=jax.ShapeDtypeStruct((bsz, n, NA_WIDTH), BF16),
        compiler_params=_params(("arbitrary", "arbitrary"), 48),
        name="nattn",
    )(qkv, qkv, qkv, qkv_ctx, qkv_ctx, bias_all)


def _ctx_attn_kernel(q_ref, k_ref, v_ref, o_ref):
    for p in range(NA_HEADS // 2):
        cs = slice(p * LANES, (p + 1) * LANES)
        q2 = _stack_head_pair(q_ref[0, :, cs])
        s = lax.dot_general(q2, k_ref[0, :, cs], _NT, preferred_element_type=F32)
        m = jnp.max(s, axis=-1, keepdims=True)
        e = jnp.exp(s - m)
        l = jnp.sum(e, axis=-1, keepdims=True)
        o2 = jnp.dot(e.astype(BF16), v_ref[0, :, cs], preferred_element_type=F32) / l
        o_ref[0, :, cs] = _unstack_head_pair(o2).astype(BF16)


def _ctx_attn(qkv_ctx):
    bsz, n_ctx, _ = qkv_ctx.shape
    return pl.pallas_call(
        _ctx_attn_kernel,
        grid=(bsz,),
        in_specs=[pl.BlockSpec((1, n_ctx, NA_WIDTH), lambda b, c=c: (b, 0, c)) for c in range(3)],
        out_specs=pl.BlockSpec((1, n_ctx, NA_WIDTH), lambda b: (b, 0, 0)),
        out_shape=jax.ShapeDtypeStruct((bsz, n_ctx, NA_WIDTH), BF16),
        compiler_params=_params(("arbitrary",), 32),
        name="ctx_attn",
    )(qkv_ctx, qkv_ctx, qkv_ctx)


def _attn_bias_tables(rpb):
    depth = rpb.shape[0]
    cq = np.arange(GRID_W)
    kc = np.arange(GRID_W)
    col_start = np.clip(cq - NA_KW // 2, 0, GRID_W - NA_KW)
    valid = (kc[None, :] >= col_start[:, None]) & (kc[None, :] < col_start[:, None] + NA_KW)
    dcol = kc[None, :] - cq[:, None] + (NA_KW - 1)
    onehot = (valid[:, :, None] & (dcol[:, :, None] == np.arange(2 * NA_KW - 1)[None, None, :])).astype(np.float32)
    plane = jnp.einsum("lhrd,qkd->lhrqk", rpb, jnp.asarray(onehot), precision=lax.Precision.HIGHEST)
    plane = plane + jnp.asarray(np.where(valid, 0.0, MASK_VALUE).astype(np.float32))
    variants = []
    for vi in range(NA_KH):
        blk = plane[:, :, vi:vi + NA_KH]
        variants.append(blk.transpose(0, 1, 3, 2, 4).reshape(depth, NA_HEADS, GRID_W, NA_KH * GRID_W))
    tbl = jnp.stack(variants, axis=1)
    return tbl.reshape(depth, NA_KH, NA_HEADS // 2, 2 * GRID_W, NA_KH * GRID_W)


def _halo_specs(tm, n, width, col_block):
    per = tm // HALO
    last = n // HALO - 1
    return [
        pl.BlockSpec((1, HALO, width), lambda b, i: (b, jnp.maximum(i * per - 1, 0), col_block)),
        pl.BlockSpec((1, tm, width), lambda b, i: (b, i, col_block)),
        pl.BlockSpec((1, HALO, width), lambda b, i: (b, jnp.minimum((i + 1) * per, last), col_block)),
    ]


def _pool_kernel(prev_ref, cur_ref, next_ref, wbd_ref, ps_ref, o_ref, ext_ref, *, tm, n_tok, rc):
    i = pl.program_id(1)
    nt = pl.num_programs(1)
    ext_ref[0:HALO] = jnp.where(i > 0, prev_ref[0], 0.0)
    ext_ref[HALO:HALO + tm] = cur_ref[0]
    ext_ref[HALO + tm:HALO + tm + HALO] = jnp.where(i < nt - 1, next_ref[0], 0.0)

    row = lax.broadcasted_iota(jnp.int32, (rc, LANES), 0)
    lane = lax.broadcasted_iota(jnp.int32, (rc, LANES), 1)
    first = lane < POOL_GROUP_DIM

    for ci in range(tm // rc):
        base = ci * rc
        t = i * tm + base + row
        halves = []
        for half in range(2):
            cs = slice(half * LANES, (half + 1) * LANES)
            w_a, w_b = POOL_WINDOWS[2 * half], POOL_WINDOWS[2 * half + 1]
            lo_a, lo_b = w_a // 2, w_b // 2
            hi_a, hi_b = w_a - lo_a - 1, w_b - lo_b - 1

            def ld(off, cs=cs):
                return ext_ref[pl.ds(base + (HALO + off), rc), cs]

            x0 = ld(0)
            s_a = x0
            for off in range(-lo_a, hi_a + 1):
                if off != 0:
                    s_a = s_a + ld(off)
            s_b = s_a
            for off in list(range(-lo_b, -lo_a)) + list(range(hi_a + 1, hi_b + 1)):
                s_b = s_b + ld(off)
            lo = jnp.where(first, lo_a, lo_b)
            hi = jnp.where(first, hi_a, hi_b)
            cnt = jnp.minimum(t + hi + 1, n_tok) - jnp.maximum(t - lo, 0)
            halves.append(jnp.where(first, s_a, s_b) / cnt.astype(F32) - x0)
        d = jnp.concatenate(halves, axis=1).astype(BF16)
        y = jnp.dot(d, wbd_ref[0], preferred_element_type=F32) * ps_ref[0]
        o_ref[0, pl.ds(base, rc), :] = y.astype(BF16)


def _pool_mix(pfc, wbd_all, pool_scale_all, l):
    bsz, n, _ = pfc.shape
    tm = min(n, 1024)
    rc = min(tm, 128)
    return pl.pallas_call(
        functools.partial(_pool_kernel, tm=tm, n_tok=n, rc=rc),
        grid=(bsz, n // tm),
        in_specs=_halo_specs(tm, n, POOL_WIDTH, 0) + [_layer_spec(wbd_all, l), _layer_spec(pool_scale_all, l)],
        out_specs=pl.BlockSpec((1, tm, POOL_WIDTH), lambda b, i: (b, i, 0)),
        out_shape=jax.ShapeDtypeStruct((bsz, n, POOL_WIDTH), BF16),
        scratch_shapes=[pltpu.VMEM((tm + 2 * HALO, POOL_WIDTH), F32)],
        compiler_params=_params(("arbitrary", "arbitrary"), 32),
        name="pool",
    )(pfc, pfc, pfc, wbd_all, pool_scale_all)


def _glu(u):
    return u[:, :CONV_WIDTH] * jax.nn.sigmoid(u[:, CONV_WIDTH:])


def _conv_kernel(prev_ref, cur_ref, next_ref, wdw_ref, bdw_ref, lng_ref, lnb_ref, o_ref, z_ref, *, tm, rc):
    i = pl.program_id(1)
    nt = pl.num_programs(1)
    ext = tm + 2 * HALO
    z_ref[0, 0:HALO] = jnp.where(i > 0, _glu(prev_ref[0]), 0.0)
    z_ref[0, HALO:HALO + tm] = _glu(cur_ref[0])
    z_ref[0, HALO + tm:ext] = jnp.where(i < nt - 1, _glu(next_ref[0]), 0.0)
    shifted_rows = ext - SUBLANES
    copy_rows = 128
    for s in range(1, SUBLANES):
        for c0 in range(0, shifted_rows, copy_rows):
            size = min(copy_rows, shifted_rows - c0)
            z_ref[s, c0:c0 + size] = z_ref[0, c0 + s:c0 + s + size]
    first_tap = HALO - CONV_K // 2

    for ci in range(tm // rc):
        base = ci * rc
        acc = jnp.broadcast_to(bdw_ref[0], (rc, CONV_WIDTH))
        for j in range(CONV_K):
            off = first_tap + j
            a0 = base + (off // SUBLANES) * SUBLANES
            acc = acc + z_ref[off % SUBLANES, a0:a0 + rc, :] * wdw_ref[0, j:j + 1, :]
        mu = jnp.mean(acc, axis=-1, keepdims=True)
        cen = acc - mu
        var = jnp.mean(cen * cen, axis=-1, keepdims=True)
        y = cen * lax.rsqrt(var + EPS) * lng_ref[0] + lnb_ref[0]
        o_ref[0, pl.ds(base, rc), :] = (y * jax.nn.sigmoid(y)).astype(BF16)


def _conv_mix(pfc, w_dw_all, b_dw_all, ln_g_all, ln_b_all, l):
    bsz, n, _ = pfc.shape
    tm = min(n, 1024)
    rc = min(tm, 64)
    return pl.pallas_call(
        functools.partial(_conv_kernel, tm=tm, rc=rc),
        grid=(bsz, n // tm),
        in_specs=_halo_specs(tm, n, 2 * CONV_WIDTH, 1) + [
            _layer_spec(w_dw_all, l), _layer_spec(b_dw_all, l), _layer_spec(ln_g_all, l), _layer_spec(ln_b_all, l)],
        out_specs=pl.BlockSpec((1, tm, CONV_WIDTH), lambda b, i: (b, i, 0)),
        out_shape=jax.ShapeDtypeStruct((bsz, n, CONV_WIDTH), BF16),
        scratch_shapes=[pltpu.VMEM((SUBLANES, tm + 2 * HALO, CONV_WIDTH), F32)],
        compiler_params=_params(("arbitrary", "arbitrary"), 40),
        name="conv",
    )(pfc, pfc, pfc, w_dw_all, b_dw_all, ln_g_all, ln_b_all)


def _channel_dft_mats(n_groups):
    c = np.arange(FOURIER_GROUP_DIM)
    ang = 2.0 * np.pi * ((c[:, None] * c[None, :]) % FOURIER_GROUP_DIM) / FOURIER_GROUP_DIM
    eye = np.eye(n_groups)
    return _bf16_const(np.kron(eye, np.cos(ang))), _bf16_const(np.kron(eye, np.sin(ang)))


def _fft_tables(n):
    assert n == FFT_N1 * FFT_N2
    k1 = np.arange(FFT_N1)
    n1 = np.arange(FFT_N1)
    n2 = np.arange(FFT_N2)
    prod = (k1[None, :, None] * (FFT_N2 * n1[None, None, :] + n2[:, None, None])) % n
    th = 2.0 * np.pi * prod / n
    cs, sn = np.cos(th), np.sin(th)
    m_big = np.concatenate([np.concatenate([cs, -sn], axis=2), np.concatenate([-sn, -cs], axis=2)], axis=1)
    k2 = np.arange(FFT_N2)
    psi = 2.0 * np.pi * ((k2[:, None] * n2[None, :]) % FFT_N2) / FFT_N2
    w_int = np.stack([np.cos(psi), np.sin(psi)], axis=2).reshape(FFT_N2, 2 * FFT_N2)
    return _bf16_const(m_big), _bf16_const(w_int)


def _fourier_kernel(x_ref, cc_ref, sc_ref, mbig_ref, wint_ref, o_ref, ab_ref, t_ref, *, n):
    xb = x_ref[0].astype(BF16)
    ab_ref[0] = jnp.dot(xb, cc_ref[...], preferred_element_type=F32)
    ab_ref[1] = jnp.dot(xb, sc_ref[...], preferred_element_type=F32)

    def stage1(n2, carry):
        za = ab_ref[0, pl.ds(n2, FFT_N1, stride=FFT_N2), :]
        zb = ab_ref[1, pl.ds(n2, FFT_N1, stride=FFT_N2), :]
        z = jnp.concatenate([za, zb], axis=0).astype(BF16)
        off = pl.multiple_of(n2 * (2 * FFT_N1), 2 * FFT_N1)
        t_ref[pl.ds(off, 2 * FFT_N1), :] = jnp.dot(mbig_ref[n2], z, preferred_element_type=F32)
        return carry

    lax.fori_loop(0, FFT_N2, stage1, 0, unroll=4)

    scale = 1.0 / np.sqrt(float(n * FOURIER_GROUP_DIM))

    def stage2(k1, carry):
        t = t_ref[pl.ds(k1, 2 * FFT_N2, stride=FFT_N1), :].astype(BF16)
        y = jnp.dot(wint_ref[...], t, preferred_element_type=F32) * scale
        o_ref[0, pl.ds(k1, FFT_N2, stride=FFT_N1), :] = y
        return carry

    lax.fori_loop(0, FFT_N1, stage2, 0, unroll=8)


def _fourier_mix(pfc):
    bsz, n, _ = pfc.shape
    cc, sc = _channel_dft_mats(LANES // FOURIER_GROUP_DIM)
    m_big, w_int = _fft_tables(n)
    col0 = POOL_WIDTH // LANES
    const2 = lambda b, h: (0, 0)
    return pl.pallas_call(
        functools.partial(_fourier_kernel, n=n),
        grid=(bsz, FOURIER_WIDTH // LANES),
        in_specs=[
            pl.BlockSpec((1, n, LANES), lambda b, h: (b, 0, col0 + h)),
            pl.BlockSpec((LANES, LANES), const2),
            pl.BlockSpec((LANES, LANES), const2),
            pl.BlockSpec(m_big.shape, lambda b, h: (0, 0, 0), pipeline_mode=pl.Buffered(1)),
            pl.BlockSpec(w_int.shape, const2),
        ],
        out_specs=pl.BlockSpec((1, n, LANES), lambda b, h: (b, 0, h)),
        out_shape=jax.ShapeDtypeStruct((bsz, n, FOURIER_WIDTH), F32),
        scratch_shapes=[pltpu.VMEM((2, n, LANES), F32), pltpu.VMEM((2 * n, LANES), F32)],
        compiler_params=_params(("arbitrary", "arbitrary"), 56),
        name="fourier",
    )(pfc, cc, sc, m_big, w_int)


def _fourier_small_kernel(x_ref, cc_ref, sc_ref, cn_ref, sn_ref, o_ref, *, n):
    xb = x_ref[0].astype(BF16)
    a = jnp.dot(xb, cc_ref[...], preferred_element_type=F32).astype(BF16)
    b = jnp.dot(xb, sc_ref[...], preferred_element_type=F32).astype(BF16)
    y = (jnp.dot(cn_ref[...], a, preferred_element_type=F32) - jnp.dot(sn_ref[...], b, preferred_element_type=F32))
    o_ref[0] = y * (1.0 / np.sqrt(float(n * FOURIER_GROUP_DIM)))


def _fourier_mix_small(pfc):
    bsz, n, _ = pfc.shape
    cc, sc = _channel_dft_mats(FOURIER_WIDTH // FOURIER_GROUP_DIM)
    k = np.arange(n)
    th = 2.0 * np.pi * ((k[:, None] * k[None, :]) % n) / n
    cn, sn = _bf16_const(np.cos(th)), _bf16_const(np.sin(th))
    const2 = lambda b: (0, 0)
    return pl.pallas_call(
        functools.partial(_fourier_small_kernel, n=n),
        grid=(bsz,),
        in_specs=[
            pl.BlockSpec((1, n, FOURIER_WIDTH), lambda b: (b, 0, POOL_WIDTH // FOURIER_WIDTH)),
            pl.BlockSpec((FOURIER_WIDTH, FOURIER_WIDTH), const2),
            pl.BlockSpec((FOURIER_WIDTH, FOURIER_WIDTH), const2),
            pl.BlockSpec((n, n), const2),
            pl.BlockSpec((n, n), const2),
        ],
        out_specs=pl.BlockSpec((1, n, FOURIER_WIDTH), lambda b: (b, 0, 0)),
        out_shape=jax.ShapeDtypeStruct((bsz, n, FOURIER_WIDTH), F32),
        compiler_params=_params(("arbitrary",), 32),
        name="fourier_small",
    )(pfc, cc, sc, cn, sn)


def _merge_kernel(attn_ref, pool_ref, four_ref, conv_ref, gate_ref, x_ref, gt_ref,
                  wa_ref, wp_ref, wf_ref, wc_ref, wo_ref, o_ref, *, mod_row, rc):
    d = D_MODEL
    tm = x_ref.shape[1]
    gt = _mod_row(gt_ref, mod_row)

    def branch_dots(rows):
        acts = (attn_ref[0, rows, :], pool_ref[0, rows, :], four_ref[0, rows, :].astype(BF16), conv_ref[0, rows, :])
        return [jnp.dot(a, w_ref[0], preferred_element_type=F32)
                for a, w_ref in zip(acts, (wa_ref, wp_ref, wf_ref, wc_ref))]

    def finish(rows, ys):
        merged = None
        for bi, y in enumerate(ys):
            term = gate_ref[0, rows, bi * d:(bi + 1) * d].astype(F32) * y
            merged = term if merged is None else merged + term
        out = jnp.dot(merged.astype(BF16), wo_ref[0], preferred_element_type=F32)
        o_ref[0, rows, :] = x_ref[0, rows, :] + gt * out

    chunks = [slice(c * rc, (c + 1) * rc) for c in range(tm // rc)]
    pending = branch_dots(chunks[0])
    for ci, rows in enumerate(chunks):
        nxt = branch_dots(chunks[ci + 1]) if ci + 1 < len(chunks) else None
        finish(rows, pending)
        pending = nxt


def _merge(attn, pool_o, four_o, conv_o, gate, x, ada, l, wts, *, mod_row):
    bsz, n, d = x.shape
    tm = min(n, 512)
    rc = min(tm, 256)
    act = lambda w: pl.BlockSpec((1, tm, w), lambda b, i: (b, i, 0))
    weights = [wts[k] for k in ("wa", "wp", "wf", "wc", "wo")]
    return pl.pallas_call(
        functools.partial(_merge_kernel, mod_row=mod_row, rc=rc),
        grid=(bsz, n // tm),
        in_specs=[act(NA_WIDTH), act(POOL_WIDTH), act(FOURIER_WIDTH), act(CONV_WIDTH), act(GATE_WIDTH), act(d),
                  _ada_spec(ada, l, 2)] + [_layer_spec(w, l) for w in weights],
        out_specs=act(d),
        out_shape=jax.ShapeDtypeStruct((bsz, n, d), F32),
        compiler_params=_params(("arbitrary", "arbitrary"), 48),
        name="merge",
    )(attn, pool_o, four_o, conv_o, gate, x, ada, *weights)


def _mlp_kernel(x_ref, g_ref, sh_ref, sc_ref, gt_ref, w1_ref, w2_ref, gfin_ref, o_ref, h_ref, acc_ref,
                *, final_norm, mod_row):
    f = pl.program_id(2)

    @pl.when(f == 0)
    def _():
        h_ref[...] = _norm_modulate(x_ref[0], g_ref[0], _mod_row(sh_ref, mod_row),
                                    _mod_row(sc_ref, mod_row)).astype(BF16)
        acc_ref[...] = jnp.zeros_like(acc_ref)

    a = jnp.dot(h_ref[...], w1_ref[0], preferred_element_type=F32)
    a = jnp.square(jnp.maximum(a, 0.0)).astype(BF16)
    acc_ref[...] += jnp.dot(a, w2_ref[0], preferred_element_type=F32)

    @pl.when(f == pl.num_programs(2) - 1)
    def _():
        y = x_ref[0] + _mod_row(gt_ref, mod_row) * acc_ref[...]
        if final_norm:
            ms = jnp.mean(y * y, axis=-1, keepdims=True)
            y = y * lax.rsqrt(ms + EPS) * gfin_ref[...]
        o_ref[0] = y


def _mlp(x, g_all, ada, l, w1_all, w2_all, g_final, *, final_norm, mod_row):
    bsz, n, d = x.shape
    tm = min(n, 1024)
    tf = 1024
    return pl.pallas_call(
        functools.partial(_mlp_kernel, final_norm=final_norm, mod_row=mod_row),
        grid=(bsz, n // tm, D_FF // tf),
        in_specs=[
            pl.BlockSpec((1, tm, d), lambda b, i, f: (b, i, 0)),
            _layer_spec(g_all, l),
            _ada_spec(ada, l, 3), _ada_spec(ada, l, 4), _ada_spec(ada, l, 5),
            pl.BlockSpec((1, d, tf), lambda b, i, f: (l, 0, f)),
            pl.BlockSpec((1, tf, d), lambda b, i, f: (l, f, 0)),
            pl.BlockSpec((1, d), lambda b, i, f: (0, 0)),
        ],
        out_specs=pl.BlockSpec((1, tm, d), lambda b, i, f: (b, i, 0)),
        out_shape=jax.ShapeDtypeStruct((bsz, n, d), F32),
        scratch_shapes=[pltpu.VMEM((tm, d), BF16), pltpu.VMEM((tm, d), F32)],
        compiler_params=_params(("arbitrary", "arbitrary", "arbitrary"), 48),
        name="mlp",
    )(x, g_all, ada, ada, ada, w1_all, w2_all, g_final.reshape(1, d))


def _rope_tables(n):
    t = jnp.arange(n)
    row = (t // GRID_W).astype(F32)
    col = (t % GRID_W).astype(F32)
    inv = ROPE_THETA ** (-jnp.arange(0, ROPE_AXIS_DIM, 2, dtype=F32) / ROPE_AXIS_DIM)
    ang_r = row[:, None] * inv
    ang_c = col[:, None] * inv
    cr, sr, cc, sc = jnp.cos(ang_r), jnp.sin(ang_r), jnp.cos(ang_c), jnp.sin(ang_c)
    cos64 = jnp.concatenate([cr, cr, cc, cc], axis=1)
    sin64 = jnp.concatenate([-sr, sr, -sc, sc], axis=1)
    return jnp.tile(cos64, (1, LANES // HEAD_DIM)), jnp.tile(sin64, (1, LANES // HEAD_DIM))


def _pool_block_diag(w_pool):
    eye = jnp.asarray(np.eye(POOL_GROUPS, dtype=np.float32))
    bd = w_pool[:, :, :, None, :] * eye[None, :, None, :, None]
    return bd.reshape(w_pool.shape[0], POOL_WIDTH, POOL_WIDTH)


def _mixer_tail(attn, pfc, gate, x, ada, l, p, *, small, mod_row):
    pool_o = _pool_mix(pfc, p["wbd"], p["pool_scale"], l)
    four_o = _fourier_mix_small(pfc) if small else _fourier_mix(pfc)
    conv_o = _conv_mix(pfc, p["w_dw"], p["b_dw"], p["ln_g"], p["ln_b"], l)
    return _merge(attn, pool_o, four_o, conv_o, gate, x, ada, l, p, mod_row=mod_row)


def kernel(x, c, ctx, c_ctx, w_mod, b_mod, g_mix, g_ff, w_in, rpb, w_pool, pool_scale, w_dw, b_dw, conv_ln_g,
           conv_ln_b, w_br_attn, w_br_pool, w_br_fourier, w_br_conv, w_out, w_ff1, w_ff2, g_final):
    bsz, n, d = x.shape
    n_ctx = ctx.shape[1]
    depth = w_in.shape[0]
    cos_t, sin_t = _rope_tables(n)
    ones_c = jnp.ones((n_ctx, LANES), F32)
    zeros_c = jnp.zeros((n_ctx, LANES), F32)

    ada_rows = SUBLANES
    cvecs = jnp.concatenate([c, c_ctx[None, :], jnp.zeros((ada_rows - bsz - 1, d), F32)], axis=0)
    ada = _ada_all(cvecs, w_mod, b_mod)

    vec = lambda a: a.reshape(depth, 1, a.shape[-1])
    p = dict(
        wbd=_pool_block_diag(w_pool).astype(BF16), pool_scale=vec(pool_scale),
        w_dw=w_dw, b_dw=vec(b_dw), ln_g=vec(conv_ln_g), ln_b=vec(conv_ln_b),
        wa=w_br_attn.astype(BF16), wp=w_br_pool.astype(BF16), wf=w_br_fourier.astype(BF16),
        wc=w_br_conv.astype(BF16), wo=w_out.astype(BF16),
    )
    w_tiles = w_in.reshape(depth, d, N_IN_TILES, IN_TILE).transpose(0, 2, 1, 3).astype(BF16)
    w1_bf = w_ff1.astype(BF16)
    w2_bf = w_ff2.astype(BF16)
    g_mix_v, g_ff_v = vec(g_mix), vec(g_ff)
    bias_all = _attn_bias_tables(rpb)

    h_ctx = ctx
    for l in range(depth):
        last = l == depth - 1
        qkv_c, pfc_c, gate_c = _in_proj(h_ctx, g_mix_v, ada, l, ones_c, zeros_c, w_tiles, rope=False, mod_row=bsz)
        if not last:
            attn_c = _ctx_attn(qkv_c)
            ctx_mid = _mixer_tail(attn_c, pfc_c, gate_c, h_ctx, ada, l, p, small=True, mod_row=bsz)
            ctx_next = _mlp(ctx_mid, g_ff_v, ada, l, w1_bf, w2_bf, g_final, final_norm=False, mod_row=bsz)

        qkv, pfc, gate = _in_proj(x, g_mix_v, ada, l, cos_t, sin_t, w_tiles, rope=True, mod_row=None)
        attn = _nattn(qkv, qkv_c, bias_all, l)
        x = _mixer_tail(attn, pfc, gate, x, ada, l, p, small=False, mod_row=None)
        x = _mlp(x, g_ff_v, ada, l, w1_bf, w2_bf, g_final, final_norm=last, mod_row=None)
        if not last:
            h_ctx = ctx_next
    return x
```

```python
import functools

import numpy as np
import jax
import jax.numpy as jnp
from jax import lax
from jax.experimental import pallas as pl
from jax.experimental.pallas import tpu as pltpu

F32 = jnp.float32
BF16 = jnp.bfloat16

D_MODEL = 1024
DEPTH = 4
GRID_W = 64
NA_HEADS = 8
HEAD_DIM = 64
NA_WIDTH = NA_HEADS * HEAD_DIM
NA_KH = 8
NA_KW = 16
ROPE_AXIS_DIM = HEAD_DIM // 2
ROPE_THETA = 10000.0
POOL_WIDTH = 256
POOL_GROUPS = 4
POOL_GROUP_DIM = 64
POOL_WINDOWS = (2, 4, 8, 16)
FOURIER_WIDTH = 256
FOURIER_GROUP_DIM = 64
CONV_WIDTH = 256
CONV_K = 31
N_BRANCH = 4
D_FF = 4 * D_MODEL
EPS = 1e-6
QKV_WIDTH = 3 * NA_WIDTH
PFC_WIDTH = POOL_WIDTH + FOURIER_WIDTH + 2 * CONV_WIDTH
GATE_WIDTH = N_BRANCH * D_MODEL
IN_WIDTH = QKV_WIDTH + PFC_WIDTH + GATE_WIDTH

LANES = 128
SUBLANES = 8
HALO = 16
MASK_VALUE = -1e30

IN_TILE = 512
N_IN_TILES = IN_WIDTH // IN_TILE
N_QKV_TILES = QKV_WIDTH // IN_TILE
N_PFC_TILES = PFC_WIDTH // IN_TILE

FFT_N1 = 128
FFT_N2 = 64


def _params(sem, vmem_mb):
    return pltpu.CompilerParams(dimension_semantics=sem, vmem_limit_bytes=vmem_mb * 1024 * 1024)


def _bf16_const(a):
    return jnp.asarray(a, F32).astype(BF16)


def _layer_spec(arr, l):
    nd = arr.ndim - 1
    return pl.BlockSpec((1,) + arr.shape[1:], lambda *_: (l,) + (0,) * nd)


def _ada_spec(ada, l, chunk):
    return pl.BlockSpec((1, ada.shape[1], D_MODEL), lambda *_: (l, 0, chunk))


def _mod_row(ada_ref, mod_row):
    row = pl.program_id(0) if mod_row is None else mod_row
    return ada_ref[0, pl.ds(row, 1), :]


def _ada_kernel(c_ref, w_ref, b_ref, o_ref):
    cv = c_ref[...]
    s = cv * jax.nn.sigmoid(cv)
    o_ref[0] = jnp.dot(s.astype(BF16), w_ref[0].astype(BF16), preferred_element_type=F32) + b_ref[0]


def _ada_all(cvecs, w_mod, b_mod):
    depth, d, width = w_mod.shape
    rows = cvecs.shape[0]
    tn = 1536
    return pl.pallas_call(
        _ada_kernel,
        grid=(depth, width // tn),
        in_specs=[
            pl.BlockSpec((rows, d), lambda l, j: (0, 0)),
            pl.BlockSpec((1, d, tn), lambda l, j: (l, 0, j)),
            pl.BlockSpec((1, 1, tn), lambda l, j: (l, 0, j)),
        ],
        out_specs=pl.BlockSpec((1, rows, tn), lambda l, j: (l, 0, j)),
        out_shape=jax.ShapeDtypeStruct((depth, rows, width), F32),
        compiler_params=_params(("arbitrary", "arbitrary"), 32),
        name="ada",
    )(cvecs, w_mod, b_mod.reshape(depth, 1, width))


def _norm_modulate(x, g, shift, scale):
    ms = jnp.mean(x * x, axis=-1, keepdims=True)
    y = x * lax.rsqrt(ms + EPS) * g
    return y * (1.0 + scale) + shift


def _norm_next_tile(dst_ref, x_ref, g_ref, sh_ref, sc_ref, tile, mod_row, tiles_per_batch):
    row = mod_row if mod_row is not None else tile // tiles_per_batch
    dst_ref[...] = _norm_modulate(x_ref[0], g_ref[0], sh_ref[0, pl.ds(row, 1), :],
                                  sc_ref[0, pl.ds(row, 1), :]).astype(BF16)


def _ping_pong(step, body, ref_a, ref_b):
    pl.when(step % 2 == 0)(lambda: body(ref_a, ref_b))
    pl.when(step % 2 == 1)(lambda: body(ref_b, ref_a))


def _in_proj_kernel(x_ref, xn_ref, g_ref, sh_ref, sc_ref, cos_ref, sin_ref, w_ref,
                    qkv_ref, pfc_ref, gate_ref, ha_ref, hb_ref, *, rope, mod_row, tiles_per_batch):
    step = pl.program_id(0)
    last = pl.num_programs(0) - 1
    norm = functools.partial(_norm_next_tile, g_ref=g_ref, sh_ref=sh_ref, sc_ref=sc_ref,
                             mod_row=mod_row, tiles_per_batch=tiles_per_batch)

    @pl.when(step == 0)
    def _():
        norm(ha_ref, x_ref, tile=0)

    def body(cur_ref, nxt_ref):
        _in_proj_tile(cur_ref, lambda: norm(nxt_ref, xn_ref, tile=jnp.minimum(step + 1, last)),
                      cos_ref, sin_ref, w_ref, qkv_ref, pfc_ref, gate_ref, rope=rope)

    _ping_pong(step, body, ha_ref, hb_ref)


def _in_proj_tile(h_ref, prepare_next, cos_ref, sin_ref, w_ref, qkv_ref, pfc_ref, gate_ref, *, rope):
    def mm(t):
        return jnp.dot(h_ref[...], w_ref[0, :, t * IN_TILE:(t + 1) * IN_TILE], preferred_element_type=F32)

    def qk_epilogue(t, acc):
        col0 = t * IN_TILE
        mult = HEAD_DIM ** -0.5 if t == 0 else None
        if rope:
            cos = cos_ref[...]
            sin = sin_ref[...]
            if mult is not None:
                cos, sin = cos * mult, sin * mult
            lane = lax.broadcasted_iota(jnp.int32, cos.shape, 1)
            first = (lane % (ROPE_AXIS_DIM)) < (ROPE_AXIS_DIM // 2)
            for cc in range(IN_TILE // LANES):
                a = acc[:, cc * LANES:(cc + 1) * LANES]
                partner = jnp.where(first,
                                    pltpu.roll(a, LANES - ROPE_AXIS_DIM // 2, 1),
                                    pltpu.roll(a, ROPE_AXIS_DIM // 2, 1))
                qkv_ref[0, :, col0 + cc * LANES:col0 + (cc + 1) * LANES] = (a * cos + partner * sin).astype(BF16)
        else:
            qkv_ref[0, :, col0:col0 + IN_TILE] = (acc if mult is None else acc * mult).astype(BF16)

    def epilogue(t, acc):
        if t < 2:
            qk_epilogue(t, acc)
        elif t < N_QKV_TILES:
            qkv_ref[0, :, t * IN_TILE:(t + 1) * IN_TILE] = acc.astype(BF16)
        elif t < N_QKV_TILES + N_PFC_TILES:
            c0 = (t - N_QKV_TILES) * IN_TILE
            pfc_ref[0, :, c0:c0 + IN_TILE] = acc
        else:
            c0 = (t - N_QKV_TILES - N_PFC_TILES) * IN_TILE
            gate_ref[0, :, c0:c0 + IN_TILE] = jax.nn.sigmoid(acc).astype(BF16)

    pending = mm(0)
    prepare_next()
    for t in range(N_IN_TILES):
        nxt = mm(t + 1) if t + 1 < N_IN_TILES else None
        epilogue(t, pending)
        pending = nxt


def _in_proj(x, g_all, ada, l, cos_t, sin_t, w_all, *, rope, mod_row):
    bsz, n, d = x.shape
    rows = bsz * n
    tm = 512
    tiles_per_batch = n // tm if mod_row is None else 1
    n_tiles = rows // tm
    rope_tiles = cos_t.shape[0] // tm
    xf = x.reshape(1, rows, d)
    row_spec = lambda w: pl.BlockSpec((1, tm, w), lambda t: (0, t, 0))
    outs = pl.pallas_call(
        functools.partial(_in_proj_kernel, rope=rope, mod_row=mod_row, tiles_per_batch=tiles_per_batch),
        grid=(n_tiles,),
        in_specs=[
            row_spec(d),
            pl.BlockSpec((1, tm, d), lambda t: (0, jnp.minimum(t + 1, n_tiles - 1), 0)),
            _layer_spec(g_all, l),
            _ada_spec(ada, l, 0),
            _ada_spec(ada, l, 1),
            pl.BlockSpec((tm, LANES), lambda t: (t % rope_tiles, 0)),
            pl.BlockSpec((tm, LANES), lambda t: (t % rope_tiles, 0)),
            pl.BlockSpec((1,) + w_all.shape[1:], lambda t: (l, 0, 0), pipeline_mode=pl.Buffered(1)),
        ],
        out_specs=[row_spec(QKV_WIDTH), row_spec(PFC_WIDTH), row_spec(GATE_WIDTH)],
        out_shape=[
            jax.ShapeDtypeStruct((1, rows, QKV_WIDTH), BF16),
            jax.ShapeDtypeStruct((1, rows, PFC_WIDTH), F32),
            jax.ShapeDtypeStruct((1, rows, GATE_WIDTH), BF16),
        ],
        scratch_shapes=[pltpu.VMEM((tm, d), BF16), pltpu.VMEM((tm, d), BF16)],
        compiler_params=_params(("arbitrary",), 56),
        name="in_proj",
    )(xf, xf, g_all, ada, ada, cos_t, sin_t, w_all)
    return [o.reshape(bsz, n, o.shape[-1]) for o in outs]


def _stack_head_pair(qp):
    lane = lax.broadcasted_iota(jnp.int32, qp.shape, 1)
    zero = jnp.zeros_like(qp)
    return jnp.concatenate([jnp.where(lane < HEAD_DIM, qp, zero), jnp.where(lane < HEAD_DIM, zero, qp)], axis=0)


def _unstack_head_pair(o2):
    m = o2.shape[0] // 2
    lane = lax.broadcasted_iota(jnp.int32, (m, LANES), 1)
    return jnp.where(lane < HEAD_DIM, o2[:m], o2[m:])


_NT = (((1,), (1,)), ((), ()))


def _nattn_kernel(q_ref, k_ref, v_ref, kc_ref, vc_ref, bias_ref, o_ref, *, rows_per_step, n_rows):
    i = pl.program_id(1)
    win = NA_KH * GRID_W
    rows_per_iter = 4
    lookahead = 2

    def scores(rl, p):
        r = i * rows_per_step + rl
        rs = jnp.clip(r - NA_KH // 2, 0, n_rows - NA_KH)
        var = rs - r + (NA_KH - 1)
        qoff = pl.multiple_of(rl * GRID_W, GRID_W)
        koff = pl.multiple_of(rs * GRID_W, GRID_W)
        cs = slice(p * LANES, (p + 1) * LANES)
        q2 = _stack_head_pair(q_ref[0, pl.ds(qoff, GRID_W), cs])
        kp = k_ref[0, pl.ds(koff, win), cs]
        s_w = lax.dot_general(q2, kp, _NT, preferred_element_type=F32) + bias_ref[0, var, p]
        s_c = lax.dot_general(q2, kc_ref[0, :, cs], _NT, preferred_element_type=F32)
        return s_w, s_c, qoff, koff, cs

    def finish(s_w, s_c, qoff, koff, cs):
        m = jnp.maximum(jnp.max(s_w, axis=-1, keepdims=True), jnp.max(s_c, axis=-1, keepdims=True))
        p_w = jnp.exp(s_w - m)
        p_c = jnp.exp(s_c - m)
        l = jnp.sum(p_w, axis=-1, keepdims=True) + jnp.sum(p_c, axis=-1, keepdims=True)
        o2 = (jnp.dot(p_w.astype(BF16), v_ref[0, pl.ds(koff, win), cs], preferred_element_type=F32)
              + jnp.dot(p_c.astype(BF16), vc_ref[0, :, cs], preferred_element_type=F32))
        o2 = o2 / l
        o_ref[0, pl.ds(qoff, GRID_W), cs] = _unstack_head_pair(o2).astype(BF16)

    def body(it, carry):
        units = [(it * rows_per_iter + u, p) for u in range(rows_per_iter) for p in range(NA_HEADS // 2)]
        pending = [scores(*u) for u in units[:lookahead]]
        for idx in range(len(units)):
            if idx + lookahead < len(units):
                pending.append(scores(*units[idx + lookahead]))
            finish(*pending.pop(0))
        return carry

    lax.fori_loop(0, rows_per_step // rows_per_iter, body, 0)


def _nattn(qkv, qkv_ctx, bias_all, l):
    bsz, n, _ = qkv.shape
    n_ctx = qkv_ctx.shape[1]
    n_rows = n // GRID_W
    rows_per_step = 8
    tq = rows_per_step * GRID_W
    single = pl.Buffered(1)
    return pl.pallas_call(
        functools.partial(_nattn_kernel, rows_per_step=rows_per_step, n_rows=n_rows),
        grid=(bsz, n_rows // rows_per_step),
        in_specs=[
            pl.BlockSpec((1, tq, NA_WIDTH), lambda b, i: (b, i, 0)),
            pl.BlockSpec((1, n, NA_WIDTH), lambda b, i: (b, 0, 1), pipeline_mode=single),
            pl.BlockSpec((1, n, NA_WIDTH), lambda b, i: (b, 0, 2), pipeline_mode=single),
            pl.BlockSpec((1, n_ctx, NA_WIDTH), lambda b, i: (b, 0, 1)),
            pl.BlockSpec((1, n_ctx, NA_WIDTH), lambda b, i: (b, 0, 2)),
            pl.BlockSpec((1,) + bias_all.shape[1:], lambda b, i: (l, 0, 0, 0, 0), pipeline_mode=single),
        ],
        out_specs=pl.BlockSpec((1, tq, NA_WIDTH), lambda b, i: (b, i, 0)),
        out_shape=jax.ShapeDtypeStruct((bsz, n, NA_WIDTH), BF16),
        compiler_params=_params(("arbitrary", "arbitrary"), 48),
        name="nattn",
    )(qkv, qkv, qkv, qkv_ctx, qkv_ctx, bias_all)


def _ctx_attn_kernel(q_ref, k_ref, v_ref, o_ref):
    for p in range(NA_HEADS // 2):
        cs = slice(p * LANES, (p + 1) * LANES)
        q2 = _stack_head_pair(q_ref[0, :, cs])
        s = lax.dot_general(q2, k_ref[0, :, cs], _NT, preferred_element_type=F32)
        m = jnp.max(s, axis=-1, keepdims=True)
        e = jnp.exp(s - m)
        l = jnp.sum(e, axis=-1, keepdims=True)
        o2 = jnp.dot(e.astype(BF16), v_ref[0, :, cs], preferred_element_type=F32) / l
        o_ref[0, :, cs] = _unstack_head_pair(o2).astype(BF16)


def _ctx_attn(qkv_ctx):
    bsz, n_ctx, _ = qkv_ctx.shape
    return pl.pallas_call(
        _ctx_attn_kernel,
        grid=(bsz,),
        in_specs=[pl.BlockSpec((1, n_ctx, NA_WIDTH), lambda b, c=c: (b, 0, c)) for c in range(3)],
        out_specs=pl.BlockSpec((1, n_ctx, NA_WIDTH), lambda b: (b, 0, 0)),
        out_shape=jax.ShapeDtypeStruct((bsz, n_ctx, NA_WIDTH), BF16),
        compiler_params=_params(("arbitrary",), 32),
        name="ctx_attn",
    )(qkv_ctx, qkv_ctx, qkv_ctx)


def _attn_bias_tables(rpb):
    depth = rpb.shape[0]
    cq = np.arange(GRID_W)
    kc = np.arange(GRID_W)
    col_start = np.clip(cq - NA_KW // 2, 0, GRID_W - NA_KW)
    valid = (kc[None, :] >= col_start[:, None]) & (kc[None, :] < col_start[:, None] + NA_KW)
    dcol = kc[None, :] - cq[:, None] + (NA_KW - 1)
    onehot = (valid[:, :, None] & (dcol[:, :, None] == np.arange(2 * NA_KW - 1)[None, None, :])).astype(np.float32)
    plane = jnp.einsum("lhrd,qkd->lhrqk", rpb, jnp.asarray(onehot), precision=lax.Precision.HIGHEST)
    plane = plane + jnp.asarray(np.where(valid, 0.0, MASK_VALUE).astype(np.float32))
    variants = []
    for vi in range(NA_KH):
        blk = plane[:, :, vi:vi + NA_KH]
        variants.append(blk.transpose(0, 1, 3, 2, 4).reshape(depth, NA_HEADS, GRID_W, NA_KH * GRID_W))
    tbl = jnp.stack(variants, axis=1)
    return tbl.reshape(depth, NA_KH, NA_HEADS // 2, 2 * GRID_W, NA_KH * GRID_W)


def _halo_specs(tm, n, width, col_block):
    per = tm // HALO
    last = n // HALO - 1
    return [
        pl.BlockSpec((1, HALO, width), lambda b, i: (b, jnp.maximum(i * per - 1, 0), col_block)),
        pl.BlockSpec((1, tm, width), lambda b, i: (b, i, col_block)),
        pl.BlockSpec((1, HALO, width), lambda b, i: (b, jnp.minimum((i + 1) * per, last), col_block)),
    ]


def _pool_kernel(prev_ref, cur_ref, next_ref, wbd_ref, ps_ref, o_ref, ext_ref, *, tm, n_tok, rc):
    i = pl.program_id(1)
    nt = pl.num_programs(1)
    ext_ref[0:HALO] = jnp.where(i > 0, prev_ref[0], 0.0)
    ext_ref[HALO:HALO + tm] = cur_ref[0]
    ext_ref[HALO + tm:HALO + tm + HALO] = jnp.where(i < nt - 1, next_ref[0], 0.0)

    row = lax.broadcasted_iota(jnp.int32, (rc, LANES), 0)
    lane = lax.broadcasted_iota(jnp.int32, (rc, LANES), 1)
    first = lane < POOL_GROUP_DIM

    for ci in range(tm // rc):
        base = ci * rc
        t = i * tm + base + row
        halves = []
        for half in range(2):
            cs = slice(half * LANES, (half + 1) * LANES)
            w_a, w_b = POOL_WINDOWS[2 * half], POOL_WINDOWS[2 * half + 1]
            lo_a, lo_b = w_a // 2, w_b // 2
            hi_a, hi_b = w_a - lo_a - 1, w_b - lo_b - 1

            def ld(off, cs=cs):
                return ext_ref[pl.ds(base + (HALO + off), rc), cs]

            x0 = ld(0)
            s_a = x0
            for off in range(-lo_a, hi_a + 1):
                if off != 0:
                    s_a = s_a + ld(off)
            s_b = s_a
            for off in list(range(-lo_b, -lo_a)) + list(range(hi_a + 1, hi_b + 1)):
                s_b = s_b + ld(off)
            lo = jnp.where(first, lo_a, lo_b)
            hi = jnp.where(first, hi_a, hi_b)
            cnt = jnp.minimum(t + hi + 1, n_tok) - jnp.maximum(t - lo, 0)
            halves.append(jnp.where(first, s_a, s_b) / cnt.astype(F32) - x0)
        d = jnp.concatenate(halves, axis=1).astype(BF16)
        y = jnp.dot(d, wbd_ref[0], preferred_element_type=F32) * ps_ref[0]
        o_ref[0, pl.ds(base, rc), :] = y.astype(BF16)


def _pool_mix(pfc, wbd_all, pool_scale_all, l):
    bsz, n, _ = pfc.shape
    tm = min(n, 1024)
    rc = min(tm, 128)
    return pl.pallas_call(
        functools.partial(_pool_kernel, tm=tm, n_tok=n, rc=rc),
        grid=(bsz, n // tm),
        in_specs=_halo_specs(tm, n, POOL_WIDTH, 0) + [_layer_spec(wbd_all, l), _layer_spec(pool_scale_all, l)],
        out_specs=pl.BlockSpec((1, tm, POOL_WIDTH), lambda b, i: (b, i, 0)),
        out_shape=jax.ShapeDtypeStruct((bsz, n, POOL_WIDTH), BF16),
        scratch_shapes=[pltpu.VMEM((tm + 2 * HALO, POOL_WIDTH), F32)],
        compiler_params=_params(("arbitrary", "arbitrary"), 32),
        name="pool",
    )(pfc, pfc, pfc, wbd_all, pool_scale_all)


def _glu(u):
    return u[:, :CONV_WIDTH] * jax.nn.sigmoid(u[:, CONV_WIDTH:])


def _conv_kernel(prev_ref, cur_ref, next_ref, wdw_ref, bdw_ref, lng_ref, lnb_ref, o_ref, z_ref, *, tm, rc):
    i = pl.program_id(1)
    nt = pl.num_programs(1)
    ext = tm + 2 * HALO
    z_ref[0, 0:HALO] = jnp.where(i > 0, _glu(prev_ref[0]), 0.0)
    z_ref[0, HALO:HALO + tm] = _glu(cur_ref[0])
    z_ref[0, HALO + tm:ext] = jnp.where(i < nt - 1, _glu(next_ref[0]), 0.0)
    shifted_rows = ext - SUBLANES
    copy_rows = 128
    for s in range(1, SUBLANES):
        for c0 in range(0, shifted_rows, copy_rows):
            size = min(copy_rows, shifted_rows - c0)
            z_ref[s, c0:c0 + size] = z_ref[0, c0 + s:c0 + s + size]
    first_tap = HALO - CONV_K // 2

    for ci in range(tm // rc):
        base = ci * rc
        acc = jnp.broadcast_to(bdw_ref[0], (rc, CONV_WIDTH))
        for j in range(CONV_K):
            off = first_tap + j
            a0 = base + (off // SUBLANES) * SUBLANES
            acc = acc + z_ref[off % SUBLANES, a0:a0 + rc, :] * wdw_ref[0, j:j + 1, :]
        mu = jnp.mean(acc, axis=-1, keepdims=True)
        cen = acc - mu
        var = jnp.mean(cen * cen, axis=-1, keepdims=True)
        y = cen * lax.rsqrt(var + EPS) * lng_ref[0] + lnb_ref[0]
        o_ref[0, pl.ds(base, rc), :] = (y * jax.nn.sigmoid(y)).astype(BF16)


def _conv_mix(pfc, w_dw_all, b_dw_all, ln_g_all, ln_b_all, l):
    bsz, n, _ = pfc.shape
    tm = min(n, 1024)
    rc = min(tm, 64)
    return pl.pallas_call(
        functools.partial(_conv_kernel, tm=tm, rc=rc),
        grid=(bsz, n // tm),
        in_specs=_halo_specs(tm, n, 2 * CONV_WIDTH, 1) + [
            _layer_spec(w_dw_all, l), _layer_spec(b_dw_all, l), _layer_spec(ln_g_all, l), _layer_spec(ln_b_all, l)],
        out_specs=pl.BlockSpec((1, tm, CONV_WIDTH), lambda b, i: (b, i, 0)),
        out_shape=jax.ShapeDtypeStruct((bsz, n, CONV_WIDTH), BF16),
        scratch_shapes=[pltpu.VMEM((SUBLANES, tm + 2 * HALO, CONV_WIDTH), F32)],
        compiler_params=_params(("arbitrary", "arbitrary"), 40),
        name="conv",
    )(pfc, pfc, pfc, w_dw_all, b_dw_all, ln_g_all, ln_b_all)


def _channel_dft_mats(n_groups):
    c = np.arange(FOURIER_GROUP_DIM)
    ang = 2.0 * np.pi * ((c[:, None] * c[None, :]) % FOURIER_GROUP_DIM) / FOURIER_GROUP_DIM
    eye = np.eye(n_groups)
    return _bf16_const(np.kron(eye, np.cos(ang))), _bf16_const(np.kron(eye, np.sin(ang)))


def _fft_tables(n):
    assert n == FFT_N1 * FFT_N2
    k1 = np.arange(FFT_N1)
    n1 = np.arange(FFT_N1)
    n2 = np.arange(FFT_N2)
    prod = (k1[None, :, None] * (FFT_N2 * n1[None, None, :] + n2[:, None, None])) % n
    th = 2.0 * np.pi * prod / n
    cs, sn = np.cos(th), np.sin(th)
    m_big = np.concatenate([np.concatenate([cs, -sn], axis=2), np.concatenate([-sn, -cs], axis=2)], axis=1)
    k2 = np.arange(FFT_N2)
    psi = 2.0 * np.pi * ((k2[:, None] * n2[None, :]) % FFT_N2) / FFT_N2
    w_int = np.stack([np.cos(psi), np.sin(psi)], axis=2).reshape(FFT_N2, 2 * FFT_N2)
    return _bf16_const(m_big), _bf16_const(w_int)


def _fourier_kernel(x_ref, cc_ref, sc_ref, mbig_ref, wint_ref, o_ref, ab_ref, t_ref, *, n):
    xb = x_ref[0].astype(BF16)
    ab_ref[0] = jnp.dot(xb, cc_ref[...], preferred_element_type=F32)
    ab_ref[1] = jnp.dot(xb, sc_ref[...], preferred_element_type=F32)

    def stage1(n2, carry):
        za = ab_ref[0, pl.ds(n2, FFT_N1, stride=FFT_N2), :]
        zb = ab_ref[1, pl.ds(n2, FFT_N1, stride=FFT_N2), :]
        z = jnp.concatenate([za, zb], axis=0).astype(BF16)
        off = pl.multiple_of(n2 * (2 * FFT_N1), 2 * FFT_N1)
        t_ref[pl.ds(off, 2 * FFT_N1), :] = jnp.dot(mbig_ref[n2], z, preferred_element_type=F32)
        return carry

    lax.fori_loop(0, FFT_N2, stage1, 0, unroll=4)

    scale = 1.0 / np.sqrt(float(n * FOURIER_GROUP_DIM))

    def stage2(k1, carry):
        t = t_ref[pl.ds(k1, 2 * FFT_N2, stride=FFT_N1), :].astype(BF16)
        y = jnp.dot(wint_ref[...], t, preferred_element_type=F32) * scale
        o_ref[0, pl.ds(k1, FFT_N2, stride=FFT_N1), :] = y
        return carry

    lax.fori_loop(0, FFT_N1, stage2, 0, unroll=8)


def _fourier_mix(pfc):
    bsz, n, _ = pfc.shape
    cc, sc = _channel_dft_mats(LANES // FOURIER_GROUP_DIM)
    m_big, w_int = _fft_tables(n)
    col0 = POOL_WIDTH // LANES
    const2 = lambda b, h: (0, 0)
    return pl.pallas_call(
        functools.partial(_fourier_kernel, n=n),
        grid=(bsz, FOURIER_WIDTH // LANES),
        in_specs=[
            pl.BlockSpec((1, n, LANES), lambda b, h: (b, 0, col0 + h)),
            pl.BlockSpec((LANES, LANES), const2),
            pl.BlockSpec((LANES, LANES), const2),
            pl.BlockSpec(m_big.shape, lambda b, h: (0, 0, 0), pipeline_mode=pl.Buffered(1)),
            pl.BlockSpec(w_int.shape, const2),
        ],
        out_specs=pl.BlockSpec((1, n, LANES), lambda b, h: (b, 0, h)),
        out_shape=jax.ShapeDtypeStruct((bsz, n, FOURIER_WIDTH), F32),
        scratch_shapes=[pltpu.VMEM((2, n, LANES), F32), pltpu.VMEM((2 * n, LANES), F32)],
        compiler_params=_params(("arbitrary", "arbitrary"), 56),
        name="fourier",
    )(pfc, cc, sc, m_big, w_int)


def _fourier_small_kernel(x_ref, cc_ref, sc_ref, cn_ref, sn_ref, o_ref, *, n):
    xb = x_ref[0].astype(BF16)
    a = jnp.dot(xb, cc_ref[...], preferred_element_type=F32).astype(BF16)
    b = jnp.dot(xb, sc_ref[...], preferred_element_type=F32).astype(BF16)
    y = (jnp.dot(cn_ref[...], a, preferred_element_type=F32) - jnp.dot(sn_ref[...], b, preferred_element_type=F32))
    o_ref[0] = y * (1.0 / np.sqrt(float(n * FOURIER_GROUP_DIM)))


def _fourier_mix_small(pfc):
    bsz, n, _ = pfc.shape
    cc, sc = _channel_dft_mats(FOURIER_WIDTH // FOURIER_GROUP_DIM)
    k = np.arange(n)
    th = 2.0 * np.pi * ((k[:, None] * k[None, :]) % n) / n
    cn, sn = _bf16_const(np.cos(th)), _bf16_const(np.sin(th))
    const2 = lambda b: (0, 0)
    return pl.pallas_call(
        functools.partial(_fourier_small_kernel, n=n),
        grid=(bsz,),
        in_specs=[
            pl.BlockSpec((1, n, FOURIER_WIDTH), lambda b: (b, 0, POOL_WIDTH // FOURIER_WIDTH)),
            pl.BlockSpec((FOURIER_WIDTH, FOURIER_WIDTH), const2),
            pl.BlockSpec((FOURIER_WIDTH, FOURIER_WIDTH), const2),
            pl.BlockSpec((n, n), const2),
            pl.BlockSpec((n, n), const2),
        ],
        out_specs=pl.BlockSpec((1, n, FOURIER_WIDTH), lambda b: (b, 0, 0)),
        out_shape=jax.ShapeDtypeStruct((bsz, n, FOURIER_WIDTH), F32),
        compiler_params=_params(("arbitrary",), 32),
        name="fourier_small",
    )(pfc, cc, sc, cn, sn)


def _merge_kernel(attn_ref, pool_ref, four_ref, conv_ref, gate_ref, x_ref, gt_ref,
                  wa_ref, wp_ref, wf_ref, wc_ref, wo_ref, o_ref, *, mod_row, rc):
    d = D_MODEL
    tm = x_ref.shape[1]
    gt = _mod_row(gt_ref, mod_row)

    def branch_dots(rows):
        acts = (attn_ref[0, rows, :], pool_ref[0, rows, :], four_ref[0, rows, :].astype(BF16), conv_ref[0, rows, :])
        return [jnp.dot(a, w_ref[0], preferred_element_type=F32)
                for a, w_ref in zip(acts, (wa_ref, wp_ref, wf_ref, wc_ref))]

    def finish(rows, ys):
        merged = None
        for bi, y in enumerate(ys):
            term = gate_ref[0, rows, bi * d:(bi + 1) * d].astype(F32) * y
            merged = term if merged is None else merged + term
        out = jnp.dot(merged.astype(BF16), wo_ref[0], preferred_element_type=F32)
        o_ref[0, rows, :] = x_ref[0, rows, :] + gt * out

    chunks = [slice(c * rc, (c + 1) * rc) for c in range(tm // rc)]
    pending = branch_dots(chunks[0])
    for ci, rows in enumerate(chunks):
        nxt = branch_dots(chunks[ci + 1]) if ci + 1 < len(chunks) else None
        finish(rows, pending)
        pending = nxt


def _merge(attn, pool_o, four_o, conv_o, gate, x, ada, l, wts, *, mod_row):
    bsz, n, d = x.shape
    tm = min(n, 512)
    rc = min(tm, 256)
    act = lambda w: pl.BlockSpec((1, tm, w), lambda b, i: (b, i, 0))
    weights = [wts[k] for k in ("wa", "wp", "wf", "wc", "wo")]
    return pl.pallas_call(
        functools.partial(_merge_kernel, mod_row=mod_row, rc=rc),
        grid=(bsz, n // tm),
        in_specs=[act(NA_WIDTH), act(POOL_WIDTH), act(FOURIER_WIDTH), act(CONV_WIDTH), act(GATE_WIDTH), act(d),
                  _ada_spec(ada, l, 2)] + [_layer_spec(w, l) for w in weights],
        out_specs=act(d),
        out_shape=jax.ShapeDtypeStruct((bsz, n, d), F32),
        compiler_params=_params(("arbitrary", "arbitrary"), 48),
        name="merge",
    )(attn, pool_o, four_o, conv_o, gate, x, ada, *weights)


def _mlp_kernel(x_ref, xn_ref, g_ref, sh_ref, sc_ref, gt_ref, w1_ref, w2_ref, gfin_ref, o_ref, ha_ref, hb_ref,
                *, final_norm, mod_row, tiles_per_batch, tf):
    step = pl.program_id(0)
    last = pl.num_programs(0) - 1
    norm = functools.partial(_norm_next_tile, g_ref=g_ref, sh_ref=sh_ref, sc_ref=sc_ref,
                             mod_row=mod_row, tiles_per_batch=tiles_per_batch)

    @pl.when(step == 0)
    def _():
        norm(ha_ref, x_ref, tile=0)

    def body(cur_ref, nxt_ref):
        def up(f):
            return jnp.dot(cur_ref[...], w1_ref[0, :, f * tf:(f + 1) * tf], preferred_element_type=F32)

        pending = up(0)
        norm(nxt_ref, xn_ref, tile=jnp.minimum(step + 1, last))
        acc = None
        for f in range(D_FF // tf):
            nxt = up(f + 1) if (f + 1) * tf < D_FF else None
            a = jnp.square(jnp.maximum(pending, 0.0)).astype(BF16)
            o = jnp.dot(a, w2_ref[0, f * tf:(f + 1) * tf, :], preferred_element_type=F32)
            acc = o if acc is None else acc + o
            pending = nxt
        row = mod_row if mod_row is not None else step // tiles_per_batch
        y = x_ref[0] + gt_ref[0, pl.ds(row, 1), :] * acc
        if final_norm:
            ms = jnp.mean(y * y, axis=-1, keepdims=True)
            y = y * lax.rsqrt(ms + EPS) * gfin_ref[...]
        o_ref[0] = y

    _ping_pong(step, body, ha_ref, hb_ref)


def _mlp(x, g_all, ada, l, w1_all, w2_all, g_final, *, final_norm, mod_row):
    bsz, n, d = x.shape
    rows = bsz * n
    tm = 512
    tf = 1024
    tiles_per_batch = n // tm if mod_row is None else 1
    n_tiles = rows // tm
    xf = x.reshape(1, rows, d)
    single = pl.Buffered(1)
    out = pl.pallas_call(
        functools.partial(_mlp_kernel, final_norm=final_norm, mod_row=mod_row,
                          tiles_per_batch=tiles_per_batch, tf=tf),
        grid=(n_tiles,),
        in_specs=[
            pl.BlockSpec((1, tm, d), lambda t: (0, t, 0)),
            pl.BlockSpec((1, tm, d), lambda t: (0, jnp.minimum(t + 1, n_tiles - 1), 0)),
            _layer_spec(g_all, l),
            _ada_spec(ada, l, 3), _ada_spec(ada, l, 4), _ada_spec(ada, l, 5),
            pl.BlockSpec((1,) + w1_all.shape[1:], lambda t: (l, 0, 0), pipeline_mode=single),
            pl.BlockSpec((1,) + w2_all.shape[1:], lambda t: (l, 0, 0), pipeline_mode=single),
            pl.BlockSpec((1, d), lambda t: (0, 0)),
        ],
        out_specs=pl.BlockSpec((1, tm, d), lambda t: (0, t, 0)),
        out_shape=jax.ShapeDtypeStruct((1, rows, d), F32),
        scratch_shapes=[pltpu.VMEM((tm, d), BF16), pltpu.VMEM((tm, d), BF16)],
        compiler_params=_params(("arbitrary",), 56),
        name="mlp",
    )(xf, xf, g_all, ada, ada, ada, w1_all, w2_all, g_final.reshape(1, d))
    return out.reshape(bsz, n, d)


def _rope_tables(n):
    t = jnp.arange(n)
    row = (t // GRID_W).astype(F32)
    col = (t % GRID_W).astype(F32)
    inv = ROPE_THETA ** (-jnp.arange(0, ROPE_AXIS_DIM, 2, dtype=F32) / ROPE_AXIS_DIM)
    ang_r = row[:, None] * inv
    ang_c = col[:, None] * inv
    cr, sr, cc, sc = jnp.cos(ang_r), jnp.sin(ang_r), jnp.cos(ang_c), jnp.sin(ang_c)
    cos64 = jnp.concatenate([cr, cr, cc, cc], axis=1)
    sin64 = jnp.concatenate([-sr, sr, -sc, sc], axis=1)
    return jnp.tile(cos64, (1, LANES // HEAD_DIM)), jnp.tile(sin64, (1, LANES // HEAD_DIM))


def _pool_block_diag(w_pool):
    eye = jnp.asarray(np.eye(POOL_GROUPS, dtype=np.float32))
    bd = w_pool[:, :, :, None, :] * eye[None, :, None, :, None]
    return bd.reshape(w_pool.shape[0], POOL_WIDTH, POOL_WIDTH)


def _mixer_tail(attn, pfc, gate, x, ada, l, p, *, small, mod_row):
    pool_o = _pool_mix(pfc, p["wbd"], p["pool_scale"], l)
    four_o = _fourier_mix_small(pfc) if small else _fourier_mix(pfc)
    conv_o = _conv_mix(pfc, p["w_dw"], p["b_dw"], p["ln_g"], p["ln_b"], l)
    acts = (attn, pool_o, four_o, conv_o, gate, x)
    if mod_row is not None:
        flat = [a.reshape(1, -1, a.shape[-1]) for a in acts]
        return _merge(*flat, ada, l, p, mod_row=mod_row).reshape(x.shape)
    return _merge(*acts, ada, l, p, mod_row=None)


def kernel(x, c, ctx, c_ctx, w_mod, b_mod, g_mix, g_ff, w_in, rpb, w_pool, pool_scale, w_dw, b_dw, conv_ln_g,
           conv_ln_b, w_br_attn, w_br_pool, w_br_fourier, w_br_conv, w_out, w_ff1, w_ff2, g_final):
    bsz, n, d = x.shape
    n_ctx = ctx.shape[1]
    depth = w_in.shape[0]
    cos_t, sin_t = _rope_tables(n)

    ada_rows = SUBLANES
    cvecs = jnp.concatenate([c, c_ctx[None, :], jnp.zeros((ada_rows - bsz - 1, d), F32)], axis=0)
    ada = _ada_all(cvecs, w_mod, b_mod)

    vec = lambda a: a.reshape(depth, 1, a.shape[-1])
    p = dict(
        wbd=_pool_block_diag(w_pool).astype(BF16), pool_scale=vec(pool_scale),
        w_dw=w_dw, b_dw=vec(b_dw), ln_g=vec(conv_ln_g), ln_b=vec(conv_ln_b),
        wa=w_br_attn.astype(BF16), wp=w_br_pool.astype(BF16), wf=w_br_fourier.astype(BF16),
        wc=w_br_conv.astype(BF16), wo=w_out.astype(BF16),
    )
    w_in_bf = w_in.astype(BF16)
    w1_bf = w_ff1.astype(BF16)
    w2_bf = w_ff2.astype(BF16)
    g_mix_v, g_ff_v = vec(g_mix), vec(g_ff)
    bias_all = _attn_bias_tables(rpb)

    h_ctx = ctx
    for l in range(depth):
        last = l == depth - 1
        qkv_c, pfc_c, gate_c = _in_proj(h_ctx, g_mix_v, ada, l, cos_t, sin_t, w_in_bf, rope=False, mod_row=bsz)
        if not last:
            attn_c = _ctx_attn(qkv_c)
            ctx_mid = _mixer_tail(attn_c, pfc_c, gate_c, h_ctx, ada, l, p, small=True, mod_row=bsz)
            ctx_next = _mlp(ctx_mid, g_ff_v, ada, l, w1_bf, w2_bf, g_final, final_norm=False, mod_row=bsz)

        qkv, pfc, gate = _in_proj(x, g_mix_v, ada, l, cos_t, sin_t, w_in_bf, rope=True, mod_row=None)
        attn = _nattn(qkv, qkv_c, bias_all, l)
        x = _mixer_tail(attn, pfc, gate, x, ada, l, p, small=False, mod_row=None)
        x = _mlp(x, g_ff_v, ada, l, w1_bf, w2_bf, g_final, final_norm=last, mod_row=None)
        if not last:
            h_ctx = ctx_next
    return x
```

```python
import functools

import numpy as np
import jax
import jax.numpy as jnp
from jax import lax
from jax.experimental import pallas as pl
from jax.experimental.pallas import tpu as pltpu

F32 = jnp.float32
BF16 = jnp.bfloat16

D_MODEL = 1024
DEPTH = 4
GRID_W = 64
NA_HEADS = 8
HEAD_DIM = 64
NA_WIDTH = NA_HEADS * HEAD_DIM
NA_KH = 8
NA_KW = 16
ROPE_AXIS_DIM = HEAD_DIM // 2
ROPE_THETA = 10000.0
POOL_WIDTH = 256
POOL_GROUPS = 4
POOL_GROUP_DIM = 64
POOL_WINDOWS = (2, 4, 8, 16)
FOURIER_WIDTH = 256
FOURIER_GROUP_DIM = 64
CONV_WIDTH = 256
CONV_K = 31
N_BRANCH = 4
D_FF = 4 * D_MODEL
EPS = 1e-6
QKV_WIDTH = 3 * NA_WIDTH
PFC_WIDTH = POOL_WIDTH + FOURIER_WIDTH + 2 * CONV_WIDTH
GATE_WIDTH = N_BRANCH * D_MODEL
IN_WIDTH = QKV_WIDTH + PFC_WIDTH + GATE_WIDTH

LANES = 128
SUBLANES = 8
HALO = 16
MASK_VALUE = -1e30
LOG2E = 1.4426950408889634
Q_SCALE = HEAD_DIM ** -0.5 * LOG2E

IN_TILE = 512
N_IN_TILES = IN_WIDTH // IN_TILE
N_QKV_TILES = QKV_WIDTH // IN_TILE
N_PFC_TILES = PFC_WIDTH // IN_TILE

FFT_N1 = 128
FFT_N2 = 64


def _params(sem, vmem_mb):
    return pltpu.CompilerParams(dimension_semantics=sem, vmem_limit_bytes=vmem_mb * 1024 * 1024)


def _bf16_const(a):
    return jnp.asarray(a, F32).astype(BF16)


def _layer_spec(arr, l):
    nd = arr.ndim - 1
    return pl.BlockSpec((1,) + arr.shape[1:], lambda *_: (l,) + (0,) * nd)


def _ada_spec(ada, l, chunk):
    return pl.BlockSpec((1, ada.shape[1], D_MODEL), lambda *_: (l, 0, chunk))


def _mod_row(ada_ref, mod_row):
    row = pl.program_id(0) if mod_row is None else mod_row
    return ada_ref[0, pl.ds(row, 1), :]


def _ada_kernel(c_ref, w_ref, b_ref, o_ref):
    cv = c_ref[...]
    s = cv * jax.nn.sigmoid(cv)
    o_ref[0] = jnp.dot(s.astype(BF16), w_ref[0].astype(BF16), preferred_element_type=F32) + b_ref[0]


def _ada_all(cvecs, w_mod, b_mod):
    depth, d, width = w_mod.shape
    rows = cvecs.shape[0]
    tn = 1536
    return pl.pallas_call(
        _ada_kernel,
        grid=(depth, width // tn),
        in_specs=[
            pl.BlockSpec((rows, d), lambda l, j: (0, 0)),
            pl.BlockSpec((1, d, tn), lambda l, j: (l, 0, j)),
            pl.BlockSpec((1, 1, tn), lambda l, j: (l, 0, j)),
        ],
        out_specs=pl.BlockSpec((1, rows, tn), lambda l, j: (l, 0, j)),
        out_shape=jax.ShapeDtypeStruct((depth, rows, width), F32),
        compiler_params=_params(("arbitrary", "arbitrary"), 32),
        name="ada",
    )(cvecs, w_mod, b_mod.reshape(depth, 1, width))


def _norm_modulate(x, g, shift, scale):
    ms = jnp.mean(x * x, axis=-1, keepdims=True)
    y = x * lax.rsqrt(ms + EPS) * g
    return y * (1.0 + scale) + shift


def _norm_next_tile(dst_ref, x_ref, g_ref, sh_ref, sc_ref, tile, mod_row, tiles_per_batch):
    row = mod_row if mod_row is not None else tile // tiles_per_batch
    dst_ref[...] = _norm_modulate(x_ref[0], g_ref[0], sh_ref[0, pl.ds(row, 1), :],
                                  sc_ref[0, pl.ds(row, 1), :]).astype(BF16)


def _ping_pong(step, body, ref_a, ref_b):
    pl.when(step % 2 == 0)(lambda: body(ref_a, ref_b))
    pl.when(step % 2 == 1)(lambda: body(ref_b, ref_a))


def _in_proj_kernel(x_ref, xn_ref, g_ref, sh_ref, sc_ref, cos_ref, sin_ref, w_ref,
                    qkv_ref, pfc_ref, gate_ref, ha_ref, hb_ref, *, rope, mod_row, tiles_per_batch):
    step = pl.program_id(0)
    last = pl.num_programs(0) - 1
    norm = functools.partial(_norm_next_tile, g_ref=g_ref, sh_ref=sh_ref, sc_ref=sc_ref,
                             mod_row=mod_row, tiles_per_batch=tiles_per_batch)

    @pl.when(step == 0)
    def _():
        norm(ha_ref, x_ref, tile=0)

    def body(cur_ref, nxt_ref):
        _in_proj_tile(cur_ref, lambda: norm(nxt_ref, xn_ref, tile=jnp.minimum(step + 1, last)),
                      cos_ref, sin_ref, w_ref, qkv_ref, pfc_ref, gate_ref, rope=rope)

    _ping_pong(step, body, ha_ref, hb_ref)


def _in_proj_tile(h_ref, prepare_next, cos_ref, sin_ref, w_ref, qkv_ref, pfc_ref, gate_ref, *, rope):
    def mm(t):
        return jnp.dot(h_ref[...], w_ref[0, :, t * IN_TILE:(t + 1) * IN_TILE], preferred_element_type=F32)

    def qk_epilogue(t, acc):
        col0 = t * IN_TILE
        mult = Q_SCALE if t == 0 else None
        if rope:
            cos = cos_ref[...]
            sin = sin_ref[...]
            if mult is not None:
                cos, sin = cos * mult, sin * mult
            lane = lax.broadcasted_iota(jnp.int32, cos.shape, 1)
            first = (lane % (ROPE_AXIS_DIM)) < (ROPE_AXIS_DIM // 2)
            for cc in range(IN_TILE // LANES):
                a = acc[:, cc * LANES:(cc + 1) * LANES]
                partner = jnp.where(first,
                                    pltpu.roll(a, LANES - ROPE_AXIS_DIM // 2, 1),
                                    pltpu.roll(a, ROPE_AXIS_DIM // 2, 1))
                qkv_ref[0, :, col0 + cc * LANES:col0 + (cc + 1) * LANES] = (a * cos + partner * sin).astype(BF16)
        else:
            qkv_ref[0, :, col0:col0 + IN_TILE] = (acc if mult is None else acc * mult).astype(BF16)

    def epilogue(t, acc):
        if t < 2:
            qk_epilogue(t, acc)
        elif t < N_QKV_TILES:
            qkv_ref[0, :, t * IN_TILE:(t + 1) * IN_TILE] = acc.astype(BF16)
        elif t < N_QKV_TILES + N_PFC_TILES:
            c0 = (t - N_QKV_TILES) * IN_TILE
            pfc_ref[0, :, c0:c0 + IN_TILE] = acc
        else:
            c0 = (t - N_QKV_TILES - N_PFC_TILES) * IN_TILE
            gate_ref[0, :, c0:c0 + IN_TILE] = jax.nn.sigmoid(acc).astype(BF16)

    pending = mm(0)
    prepare_next()
    for t in range(N_IN_TILES):
        nxt = mm(t + 1) if t + 1 < N_IN_TILES else None
        epilogue(t, pending)
        pending = nxt


def _in_proj(x, g_all, ada, l, cos_t, sin_t, w_all, *, rope, mod_row):
    bsz, n, d = x.shape
    rows = bsz * n
    tm = 512
    tiles_per_batch = n // tm if mod_row is None else 1
    n_tiles = rows // tm
    rope_tiles = cos_t.shape[0] // tm
    xf = x.reshape(1, rows, d)
    row_spec = lambda w: pl.BlockSpec((1, tm, w), lambda t: (0, t, 0))
    outs = pl.pallas_call(
        functools.partial(_in_proj_kernel, rope=rope, mod_row=mod_row, tiles_per_batch=tiles_per_batch),
        grid=(n_tiles,),
        in_specs=[
            row_spec(d),
            pl.BlockSpec((1, tm, d), lambda t: (0, jnp.minimum(t + 1, n_tiles - 1), 0)),
            _layer_spec(g_all, l),
            _ada_spec(ada, l, 0),
            _ada_spec(ada, l, 1),
            pl.BlockSpec((tm, LANES), lambda t: (t % rope_tiles, 0)),
            pl.BlockSpec((tm, LANES), lambda t: (t % rope_tiles, 0)),
            pl.BlockSpec((1,) + w_all.shape[1:], lambda t: (l, 0, 0), pipeline_mode=pl.Buffered(1)),
        ],
        out_specs=[row_spec(QKV_WIDTH), row_spec(PFC_WIDTH), row_spec(GATE_WIDTH)],
        out_shape=[
            jax.ShapeDtypeStruct((1, rows, QKV_WIDTH), BF16),
            jax.ShapeDtypeStruct((1, rows, PFC_WIDTH), F32),
            jax.ShapeDtypeStruct((1, rows, GATE_WIDTH), BF16),
        ],
        scratch_shapes=[pltpu.VMEM((tm, d), BF16), pltpu.VMEM((tm, d), BF16)],
        compiler_params=_params(("arbitrary",), 56),
        name="in_proj",
    )(xf, xf, g_all, ada, ada, cos_t, sin_t, w_all)
    return [o.reshape(bsz, n, o.shape[-1]) for o in outs]


def _stack_head_pair(qp):
    lane = lax.broadcasted_iota(jnp.int32, qp.shape, 1)
    zero = jnp.zeros_like(qp)
    return jnp.concatenate([jnp.where(lane < HEAD_DIM, qp, zero), jnp.where(lane < HEAD_DIM, zero, qp)], axis=0)


def _unstack_head_pair(o2):
    m = o2.shape[0] // 2
    lane = lax.broadcasted_iota(jnp.int32, (m, LANES), 1)
    return jnp.where(lane < HEAD_DIM, o2[:m], o2[m:])


_NT = (((1,), (1,)), ((), ()))


def _nattn_kernel(q_ref, k_ref, v_ref, kc_ref, vc_ref, bias_ref, o_ref, *, rows_per_step, n_rows):
    i = pl.program_id(1)
    win = NA_KH * GRID_W
    rows_per_iter = rows_per_step
    lookahead = 2

    def scores(rl, p):
        r = i * rows_per_step + rl
        rs = jnp.clip(r - NA_KH // 2, 0, n_rows - NA_KH)
        var = rs - r + (NA_KH - 1)
        qoff = pl.multiple_of(rl * GRID_W, GRID_W)
        koff = pl.multiple_of(rs * GRID_W, GRID_W)
        cs = slice(p * LANES, (p + 1) * LANES)
        q2 = _stack_head_pair(q_ref[0, pl.ds(qoff, GRID_W), cs])
        kp = k_ref[0, pl.ds(koff, win), cs]
        s_w = lax.dot_general(q2, kp, _NT, preferred_element_type=F32) + bias_ref[0, var, p]
        s_c = lax.dot_general(q2, kc_ref[0, :, cs], _NT, preferred_element_type=F32)
        return s_w, s_c, qoff, koff, cs

    def finish(s_w, s_c, qoff, koff, cs):
        m = jnp.maximum(jnp.max(s_w, axis=-1, keepdims=True), jnp.max(s_c, axis=-1, keepdims=True))
        p_w = jnp.exp2(s_w - m)
        p_c = jnp.exp2(s_c - m)
        l = jnp.sum(p_w, axis=-1, keepdims=True) + jnp.sum(p_c, axis=-1, keepdims=True)
        o2 = (jnp.dot(p_w.astype(BF16), v_ref[0, pl.ds(koff, win), cs], preferred_element_type=F32)
              + jnp.dot(p_c.astype(BF16), vc_ref[0, :, cs], preferred_element_type=F32))
        o2 = o2 / l
        o_ref[0, pl.ds(qoff, GRID_W), cs] = _unstack_head_pair(o2).astype(BF16)

    def body(it, carry):
        units = [(it * rows_per_iter + u, p) for u in range(rows_per_iter) for p in range(NA_HEADS // 2)]
        pending = [scores(*u) for u in units[:lookahead]]
        for idx in range(len(units)):
            if idx + lookahead < len(units):
                pending.append(scores(*units[idx + lookahead]))
            finish(*pending.pop(0))
        return carry

    lax.fori_loop(0, rows_per_step // rows_per_iter, body, 0)


def _nattn(qkv, qkv_ctx, bias_all, l):
    bsz, n, _ = qkv.shape
    n_ctx = qkv_ctx.shape[1]
    n_rows = n // GRID_W
    rows_per_step = 8
    tq = rows_per_step * GRID_W
    single = pl.Buffered(1)
    return pl.pallas_call(
        functools.partial(_nattn_kernel, rows_per_step=rows_per_step, n_rows=n_rows),
        grid=(bsz, n_rows // rows_per_step),
        in_specs=[
            pl.BlockSpec((1, tq, NA_WIDTH), lambda b, i: (b, i, 0)),
            pl.BlockSpec((1, n, NA_WIDTH), lambda b, i: (b, 0, 1), pipeline_mode=single),
            pl.BlockSpec((1, n, NA_WIDTH), lambda b, i: (b, 0, 2), pipeline_mode=single),
            pl.BlockSpec((1, n_ctx, NA_WIDTH), lambda b, i: (b, 0, 1)),
            pl.BlockSpec((1, n_ctx, NA_WIDTH), lambda b, i: (b, 0, 2)),
            pl.BlockSpec((1,) + bias_all.shape[1:], lambda b, i: (l, 0, 0, 0, 0), pipeline_mode=single),
        ],
        out_specs=pl.BlockSpec((1, tq, NA_WIDTH), lambda b, i: (b, i, 0)),
        out_shape=jax.ShapeDtypeStruct((bsz, n, NA_WIDTH), BF16),
        compiler_params=_params(("arbitrary", "arbitrary"), 48),
        name="nattn",
    )(qkv, qkv, qkv, qkv_ctx, qkv_ctx, bias_all)


def _ctx_attn_kernel(q_ref, k_ref, v_ref, o_ref):
    for p in range(NA_HEADS // 2):
        cs = slice(p * LANES, (p + 1) * LANES)
        q2 = _stack_head_pair(q_ref[0, :, cs])
        s = lax.dot_general(q2, k_ref[0, :, cs], _NT, preferred_element_type=F32)
        m = jnp.max(s, axis=-1, keepdims=True)
        e = jnp.exp2(s - m)
        l = jnp.sum(e, axis=-1, keepdims=True)
        o2 = jnp.dot(e.astype(BF16), v_ref[0, :, cs], preferred_element_type=F32) / l
        o_ref[0, :, cs] = _unstack_head_pair(o2).astype(BF16)


def _ctx_attn(qkv_ctx):
    bsz, n_ctx, _ = qkv_ctx.shape
    return pl.pallas_call(
        _ctx_attn_kernel,
        grid=(bsz,),
        in_specs=[pl.BlockSpec((1, n_ctx, NA_WIDTH), lambda b, c=c: (b, 0, c)) for c in range(3)],
        out_specs=pl.BlockSpec((1, n_ctx, NA_WIDTH), lambda b: (b, 0, 0)),
        out_shape=jax.ShapeDtypeStruct((bsz, n_ctx, NA_WIDTH), BF16),
        compiler_params=_params(("arbitrary",), 32),
        name="ctx_attn",
    )(qkv_ctx, qkv_ctx, qkv_ctx)


def _attn_bias_tables(rpb):
    depth = rpb.shape[0]
    cq = np.arange(GRID_W)
    kc = np.arange(GRID_W)
    col_start = np.clip(cq - NA_KW // 2, 0, GRID_W - NA_KW)
    valid = (kc[None, :] >= col_start[:, None]) & (kc[None, :] < col_start[:, None] + NA_KW)
    dcol = kc[None, :] - cq[:, None] + (NA_KW - 1)
    onehot = (valid[:, :, None] & (dcol[:, :, None] == np.arange(2 * NA_KW - 1)[None, None, :])).astype(np.float32)
    plane = jnp.einsum("lhrd,qkd->lhrqk", rpb, jnp.asarray(onehot), precision=lax.Precision.HIGHEST)
    plane = plane * LOG2E + jnp.asarray(np.where(valid, 0.0, MASK_VALUE).astype(np.float32))
    variants = []
    for vi in range(NA_KH):
        blk = plane[:, :, vi:vi + NA_KH]
        variants.append(blk.transpose(0, 1, 3, 2, 4).reshape(depth, NA_HEADS, GRID_W, NA_KH * GRID_W))
    tbl = jnp.stack(variants, axis=1)
    return tbl.reshape(depth, NA_KH, NA_HEADS // 2, 2 * GRID_W, NA_KH * GRID_W)


def _halo_specs(tm, n, width, col_block):
    per = tm // HALO
    last = n // HALO - 1
    return [
        pl.BlockSpec((1, HALO, width), lambda b, i: (b, jnp.maximum(i * per - 1, 0), col_block)),
        pl.BlockSpec((1, tm, width), lambda b, i: (b, i, col_block)),
        pl.BlockSpec((1, HALO, width), lambda b, i: (b, jnp.minimum((i + 1) * per, last), col_block)),
    ]


def _pool_fill(ext_ref, prev_ref, cur_ref, next_ref, tm):
    i = pl.program_id(1)
    nt = pl.num_programs(1)
    ext_ref[0:HALO] = jnp.where(i > 0, prev_ref[0], 0.0)
    ext_ref[HALO:HALO + tm] = cur_ref[0]
    ext_ref[HALO + tm:HALO + tm + HALO] = jnp.where(i < nt - 1, next_ref[0], 0.0)


def _pool_rows(ext_ref, wbd_ref, ps_ref, o_ref, row0, n_rows, *, tm, n_tok, rc=128):
    i = pl.program_id(1)
    rc = min(rc, n_rows)
    row = lax.broadcasted_iota(jnp.int32, (rc, LANES), 0)
    lane = lax.broadcasted_iota(jnp.int32, (rc, LANES), 1)
    first = lane < POOL_GROUP_DIM

    for ci in range(n_rows // rc):
        base = row0 + ci * rc
        t = i * tm + base + row
        halves = []
        for half in range(2):
            cs = slice(half * LANES, (half + 1) * LANES)
            w_a, w_b = POOL_WINDOWS[2 * half], POOL_WINDOWS[2 * half + 1]
            lo_a, lo_b = w_a // 2, w_b // 2
            hi_a, hi_b = w_a - lo_a - 1, w_b - lo_b - 1

            def ld(off, cs=cs):
                return ext_ref[pl.ds(base + (HALO + off), rc), cs]

            x0 = ld(0)
            s_a = x0
            for off in range(-lo_a, hi_a + 1):
                if off != 0:
                    s_a = s_a + ld(off)
            s_b = s_a
            for off in list(range(-lo_b, -lo_a)) + list(range(hi_a + 1, hi_b + 1)):
                s_b = s_b + ld(off)
            lo = jnp.where(first, lo_a, lo_b)
            hi = jnp.where(first, hi_a, hi_b)
            cnt = jnp.minimum(t + hi + 1, n_tok) - jnp.maximum(t - lo, 0)
            halves.append(jnp.where(first, s_a, s_b) / cnt.astype(F32) - x0)
        d = jnp.concatenate(halves, axis=1).astype(BF16)
        y = jnp.dot(d, wbd_ref[0], preferred_element_type=F32) * ps_ref[0]
        o_ref[pl.ds(base, rc), :] = y.astype(BF16)


def _glu(u):
    return u[:, :CONV_WIDTH] * jax.nn.sigmoid(u[:, CONV_WIDTH:])


def _conv_fill(z_ref, prev_ref, cur_ref, next_ref, tm):
    i = pl.program_id(1)
    nt = pl.num_programs(1)
    ext = tm + 2 * HALO
    z_ref[0, 0:HALO] = jnp.where(i > 0, _glu(prev_ref[0]), 0.0)
    z_ref[0, HALO:HALO + tm] = _glu(cur_ref[0])
    z_ref[0, HALO + tm:ext] = jnp.where(i < nt - 1, _glu(next_ref[0]), 0.0)
    shifted_rows = ext - SUBLANES
    copy_rows = 128
    for s in range(1, SUBLANES):
        for c0 in range(0, shifted_rows, copy_rows):
            size = min(copy_rows, shifted_rows - c0)
            z_ref[s, c0:c0 + size] = z_ref[0, c0 + s:c0 + s + size]


def _conv_rows(z_ref, wdw_ref, bdw_ref, lng_ref, lnb_ref, o_ref, row0, n_rows, *, rc=64):
    rc = min(rc, n_rows)
    first_tap = HALO - CONV_K // 2

    for ci in range(n_rows // rc):
        base = row0 + ci * rc
        acc = jnp.broadcast_to(bdw_ref[0], (rc, CONV_WIDTH))
        for j in range(CONV_K):
            off = first_tap + j
            a0 = base + (off // SUBLANES) * SUBLANES
            acc = acc + z_ref[off % SUBLANES, a0:a0 + rc, :] * wdw_ref[0, j:j + 1, :]
        mu = jnp.mean(acc, axis=-1, keepdims=True)
        cen = acc - mu
        var = jnp.mean(cen * cen, axis=-1, keepdims=True)
        y = cen * lax.rsqrt(var + EPS) * lng_ref[0] + lnb_ref[0]
        o_ref[pl.ds(base, rc), :] = (y * jax.nn.sigmoid(y)).astype(BF16)


def _channel_dft_mats(n_groups):
    c = np.arange(FOURIER_GROUP_DIM)
    ang = 2.0 * np.pi * ((c[:, None] * c[None, :]) % FOURIER_GROUP_DIM) / FOURIER_GROUP_DIM
    eye = np.eye(n_groups)
    return _bf16_const(np.kron(eye, np.cos(ang))), _bf16_const(np.kron(eye, np.sin(ang)))


def _fft_tables(n):
    assert n == FFT_N1 * FFT_N2
    k1 = np.arange(FFT_N1)
    n1 = np.arange(FFT_N1)
    n2 = np.arange(FFT_N2)
    prod = (k1[None, :, None] * (FFT_N2 * n1[None, None, :] + n2[:, None, None])) % n
    th = 2.0 * np.pi * prod / n
    cs, sn = np.cos(th), np.sin(th)
    m_big = np.concatenate([np.concatenate([cs, -sn], axis=2), np.concatenate([-sn, -cs], axis=2)], axis=1)
    k2 = np.arange(FFT_N2)
    psi = 2.0 * np.pi * ((k2[:, None] * n2[None, :]) % FFT_N2) / FFT_N2
    w_int = np.stack([np.cos(psi), np.sin(psi)], axis=2).reshape(FFT_N2, 2 * FFT_N2)
    return _bf16_const(m_big), _bf16_const(w_int)


def _fourier_kernel(x_ref, cc_ref, sc_ref, mbig_ref, wint_ref, o_ref, ab_ref, t_ref, *, n):
    xb = x_ref[0].astype(BF16)
    ab_ref[0] = jnp.dot(xb, cc_ref[...], preferred_element_type=F32)
    ab_ref[1] = jnp.dot(xb, sc_ref[...], preferred_element_type=F32)

    def stage1(n2, carry):
        za = ab_ref[0, pl.ds(n2, FFT_N1, stride=FFT_N2), :]
        zb = ab_ref[1, pl.ds(n2, FFT_N1, stride=FFT_N2), :]
        z = jnp.concatenate([za, zb], axis=0).astype(BF16)
        off = pl.multiple_of(n2 * (2 * FFT_N1), 2 * FFT_N1)
        t_ref[pl.ds(off, 2 * FFT_N1), :] = jnp.dot(mbig_ref[n2], z, preferred_element_type=F32)
        return carry

    lax.fori_loop(0, FFT_N2, stage1, 0, unroll=4)

    scale = 1.0 / np.sqrt(float(n * FOURIER_GROUP_DIM))

    def stage2(k1, carry):
        t = t_ref[pl.ds(k1, 2 * FFT_N2, stride=FFT_N1), :].astype(BF16)
        y = jnp.dot(wint_ref[...], t, preferred_element_type=F32) * scale
        o_ref[0, pl.ds(k1, FFT_N2, stride=FFT_N1), :] = y
        return carry

    lax.fori_loop(0, FFT_N1, stage2, 0, unroll=8)


def _fourier_mix(pfc):
    bsz, n, _ = pfc.shape
    cc, sc = _channel_dft_mats(LANES // FOURIER_GROUP_DIM)
    m_big, w_int = _fft_tables(n)
    col0 = POOL_WIDTH // LANES
    const2 = lambda b, h: (0, 0)
    return pl.pallas_call(
        functools.partial(_fourier_kernel, n=n),
        grid=(bsz, FOURIER_WIDTH // LANES),
        in_specs=[
            pl.BlockSpec((1, n, LANES), lambda b, h: (b, 0, col0 + h)),
            pl.BlockSpec((LANES, LANES), const2),
            pl.BlockSpec((LANES, LANES), const2),
            pl.BlockSpec(m_big.shape, lambda b, h: (0, 0, 0), pipeline_mode=pl.Buffered(1)),
            pl.BlockSpec(w_int.shape, const2),
        ],
        out_specs=pl.BlockSpec((1, n, LANES), lambda b, h: (b, 0, h)),
        out_shape=jax.ShapeDtypeStruct((bsz, n, FOURIER_WIDTH), F32),
        scratch_shapes=[pltpu.VMEM((2, n, LANES), F32), pltpu.VMEM((2 * n, LANES), F32)],
        compiler_params=_params(("arbitrary", "arbitrary"), 56),
        name="fourier",
    )(pfc, cc, sc, m_big, w_int)


def _fourier_small_kernel(x_ref, cc_ref, sc_ref, cn_ref, sn_ref, o_ref, *, n):
    xb = x_ref[0].astype(BF16)
    a = jnp.dot(xb, cc_ref[...], preferred_element_type=F32).astype(BF16)
    b = jnp.dot(xb, sc_ref[...], preferred_element_type=F32).astype(BF16)
    y = (jnp.dot(cn_ref[...], a, preferred_element_type=F32) - jnp.dot(sn_ref[...], b, preferred_element_type=F32))
    o_ref[0] = y * (1.0 / np.sqrt(float(n * FOURIER_GROUP_DIM)))


def _fourier_mix_small(pfc):
    bsz, n, _ = pfc.shape
    cc, sc = _channel_dft_mats(FOURIER_WIDTH // FOURIER_GROUP_DIM)
    k = np.arange(n)
    th = 2.0 * np.pi * ((k[:, None] * k[None, :]) % n) / n
    cn, sn = _bf16_const(np.cos(th)), _bf16_const(np.sin(th))
    const2 = lambda b: (0, 0)
    return pl.pallas_call(
        functools.partial(_fourier_small_kernel, n=n),
        grid=(bsz,),
        in_specs=[
            pl.BlockSpec((1, n, FOURIER_WIDTH), lambda b: (b, 0, POOL_WIDTH // FOURIER_WIDTH)),
            pl.BlockSpec((FOURIER_WIDTH, FOURIER_WIDTH), const2),
            pl.BlockSpec((FOURIER_WIDTH, FOURIER_WIDTH), const2),
            pl.BlockSpec((n, n), const2),
            pl.BlockSpec((n, n), const2),
        ],
        out_specs=pl.BlockSpec((1, n, FOURIER_WIDTH), lambda b: (b, 0, 0)),
        out_shape=jax.ShapeDtypeStruct((bsz, n, FOURIER_WIDTH), F32),
        compiler_params=_params(("arbitrary",), 32),
        name="fourier_small",
    )(pfc, cc, sc, cn, sn)


def _merge_kernel(attn_ref, four_ref, gate_ref, x_ref, pp_ref, pc_ref, pn_ref, cp_ref, cc_ref, cn_ref, gt_ref,
                  wbd_ref, ps_ref, wdw_ref, bdw_ref, lng_ref, lnb_ref,
                  wa_ref, wp_ref, wf_ref, wc_ref, wo_ref, o_ref, ext_ref, z_ref, pool_ref, conv_ref,
                  *, mod_row, rc, n_tok):
    d = D_MODEL
    tm = x_ref.shape[1]
    gt = _mod_row(gt_ref, mod_row)
    _pool_fill(ext_ref, pp_ref, pc_ref, pn_ref, tm)
    _conv_fill(z_ref, cp_ref, cc_ref, cn_ref, tm)

    def mixers(row0):
        _pool_rows(ext_ref, wbd_ref, ps_ref, pool_ref, row0, rc, tm=tm, n_tok=n_tok)
        _conv_rows(z_ref, wdw_ref, bdw_ref, lng_ref, lnb_ref, conv_ref, row0, rc)

    def branch_dots(rows):
        acts = (attn_ref[0, rows, :], pool_ref[rows, :], four_ref[0, rows, :].astype(BF16), conv_ref[rows, :])
        return [jnp.dot(a, w_ref[0], preferred_element_type=F32)
                for a, w_ref in zip(acts, (wa_ref, wp_ref, wf_ref, wc_ref))]

    def finish(rows, ys):
        merged = None
        for bi, y in enumerate(ys):
            term = gate_ref[0, rows, bi * d:(bi + 1) * d].astype(F32) * y
            merged = term if merged is None else merged + term
        out = jnp.dot(merged.astype(BF16), wo_ref[0], preferred_element_type=F32)
        o_ref[0, rows, :] = x_ref[0, rows, :] + gt * out

    chunks = [slice(c * rc, (c + 1) * rc) for c in range(tm // rc)]
    mixers(0)
    pending = branch_dots(chunks[0])
    for ci, rows in enumerate(chunks):
        nxt = None
        if ci + 1 < len(chunks):
            mixers((ci + 1) * rc)
            nxt = branch_dots(chunks[ci + 1])
        finish(rows, pending)
        pending = nxt


def _merge(attn, four_o, gate, x, pfc, ada, l, p, *, mod_row):
    bsz, n, d = x.shape
    tm = min(n, 512)
    rc = min(tm, 256)
    act = lambda w: pl.BlockSpec((1, tm, w), lambda b, i: (b, i, 0))
    params = [p[k] for k in ("wbd", "pool_scale", "w_dw", "b_dw", "ln_g", "ln_b", "wa", "wp", "wf", "wc", "wo")]
    return pl.pallas_call(
        functools.partial(_merge_kernel, mod_row=mod_row, rc=rc, n_tok=n),
        grid=(bsz, n // tm),
        in_specs=[act(NA_WIDTH), act(FOURIER_WIDTH), act(GATE_WIDTH), act(d)]
        + _halo_specs(tm, n, POOL_WIDTH, 0) + _halo_specs(tm, n, 2 * CONV_WIDTH, 1)
        + [_ada_spec(ada, l, 2)] + [_layer_spec(w, l) for w in params],
        out_specs=act(d),
        out_shape=jax.ShapeDtypeStruct((bsz, n, d), F32),
        scratch_shapes=[
            pltpu.VMEM((tm + 2 * HALO, POOL_WIDTH), F32),
            pltpu.VMEM((SUBLANES, tm + 2 * HALO, CONV_WIDTH), F32),
            pltpu.VMEM((tm, POOL_WIDTH), BF16),
            pltpu.VMEM((tm, CONV_WIDTH), BF16),
        ],
        compiler_params=_params(("arbitrary", "arbitrary"), 56),
        name="merge",
    )(attn, four_o, gate, x, pfc, pfc, pfc, pfc, pfc, pfc, ada, *params)


def _mlp_kernel(x_ref, xn_ref, g_ref, sh_ref, sc_ref, gt_ref, w1_ref, w2_ref, gfin_ref, o_ref, ha_ref, hb_ref,
                *, final_norm, mod_row, tiles_per_batch, tf):
    step = pl.program_id(0)
    last = pl.num_programs(0) - 1
    norm = functools.partial(_norm_next_tile, g_ref=g_ref, sh_ref=sh_ref, sc_ref=sc_ref,
                             mod_row=mod_row, tiles_per_batch=tiles_per_batch)

    @pl.when(step == 0)
    def _():
        norm(ha_ref, x_ref, tile=0)

    def body(cur_ref, nxt_ref):
        def up(f):
            return jnp.dot(cur_ref[...], w1_ref[0, :, f * tf:(f + 1) * tf], preferred_element_type=F32)

        pending = up(0)
        norm(nxt_ref, xn_ref, tile=jnp.minimum(step + 1, last))
        acc = None
        for f in range(D_FF // tf):
            nxt = up(f + 1) if (f + 1) * tf < D_FF else None
            a = jnp.square(jnp.maximum(pending, 0.0)).astype(BF16)
            o = jnp.dot(a, w2_ref[0, f * tf:(f + 1) * tf, :], preferred_element_type=F32)
            acc = o if acc is None else acc + o
            pending = nxt
        row = mod_row if mod_row is not None else step // tiles_per_batch
        y = x_ref[0] + gt_ref[0, pl.ds(row, 1), :] * acc
        if final_norm:
            ms = jnp.mean(y * y, axis=-1, keepdims=True)
            y = y * lax.rsqrt(ms + EPS) * gfin_ref[...]
        o_ref[0] = y

    _ping_pong(step, body, ha_ref, hb_ref)


def _mlp(x, g_all, ada, l, w1_all, w2_all, g_final, *, final_norm, mod_row):
    bsz, n, d = x.shape
    rows = bsz * n
    tm = 512
    tf = 1024
    tiles_per_batch = n // tm if mod_row is None else 1
    n_tiles = rows // tm
    xf = x.reshape(1, rows, d)
    single = pl.Buffered(1)
    out = pl.pallas_call(
        functools.partial(_mlp_kernel, final_norm=final_norm, mod_row=mod_row,
                          tiles_per_batch=tiles_per_batch, tf=tf),
        grid=(n_tiles,),
        in_specs=[
            pl.BlockSpec((1, tm, d), lambda t: (0, t, 0)),
            pl.BlockSpec((1, tm, d), lambda t: (0, jnp.minimum(t + 1, n_tiles - 1), 0)),
            _layer_spec(g_all, l),
            _ada_spec(ada, l, 3), _ada_spec(ada, l, 4), _ada_spec(ada, l, 5),
            pl.BlockSpec((1,) + w1_all.shape[1:], lambda t: (l, 0, 0), pipeline_mode=single),
            pl.BlockSpec((1,) + w2_all.shape[1:], lambda t: (l, 0, 0), pipeline_mode=single),
            pl.BlockSpec((1, d), lambda t: (0, 0)),
        ],
        out_specs=pl.BlockSpec((1, tm, d), lambda t: (0, t, 0)),
        out_shape=jax.ShapeDtypeStruct((1, rows, d), F32),
        scratch_shapes=[pltpu.VMEM((tm, d), BF16), pltpu.VMEM((tm, d), BF16)],
        compiler_params=_params(("arbitrary",), 56),
        name="mlp",
    )(xf, xf, g_all, ada, ada, ada, w1_all, w2_all, g_final.reshape(1, d))
    return out.reshape(bsz, n, d)


def _rope_tables(n):
    t = jnp.arange(n)
    row = (t // GRID_W).astype(F32)
    col = (t % GRID_W).astype(F32)
    inv = ROPE_THETA ** (-jnp.arange(0, ROPE_AXIS_DIM, 2, dtype=F32) / ROPE_AXIS_DIM)
    ang_r = row[:, None] * inv
    ang_c = col[:, None] * inv
    cr, sr, cc, sc = jnp.cos(ang_r), jnp.sin(ang_r), jnp.cos(ang_c), jnp.sin(ang_c)
    cos64 = jnp.concatenate([cr, cr, cc, cc], axis=1)
    sin64 = jnp.concatenate([-sr, sr, -sc, sc], axis=1)
    return jnp.tile(cos64, (1, LANES // HEAD_DIM)), jnp.tile(sin64, (1, LANES // HEAD_DIM))


def _pool_block_diag(w_pool):
    eye = jnp.asarray(np.eye(POOL_GROUPS, dtype=np.float32))
    bd = w_pool[:, :, :, None, :] * eye[None, :, None, :, None]
    return bd.reshape(w_pool.shape[0], POOL_WIDTH, POOL_WIDTH)


def _mixer_tail(attn, pfc, gate, x, ada, l, p, *, small, mod_row):
    four_o = _fourier_mix_small(pfc) if small else _fourier_mix(pfc)
    return _merge(attn, four_o, gate, x, pfc, ada, l, p, mod_row=mod_row)


def kernel(x, c, ctx, c_ctx, w_mod, b_mod, g_mix, g_ff, w_in, rpb, w_pool, pool_scale, w_dw, b_dw, conv_ln_g,
           conv_ln_b, w_br_attn, w_br_pool, w_br_fourier, w_br_conv, w_out, w_ff1, w_ff2, g_final):
    bsz, n, d = x.shape
    n_ctx = ctx.shape[1]
    depth = w_in.shape[0]
    cos_t, sin_t = _rope_tables(n)

    ada_rows = SUBLANES
    cvecs = jnp.concatenate([c, c_ctx[None, :], jnp.zeros((ada_rows - bsz - 1, d), F32)], axis=0)
    ada = _ada_all(cvecs, w_mod, b_mod)

    vec = lambda a: a.reshape(depth, 1, a.shape[-1])
    p = dict(
        wbd=_pool_block_diag(w_pool).astype(BF16), pool_scale=vec(pool_scale),
        w_dw=w_dw, b_dw=vec(b_dw), ln_g=vec(conv_ln_g), ln_b=vec(conv_ln_b),
        wa=w_br_attn.astype(BF16), wp=w_br_pool.astype(BF16), wf=w_br_fourier.astype(BF16),
        wc=w_br_conv.astype(BF16), wo=w_out.astype(BF16),
    )
    w_in_bf = w_in.astype(BF16)
    w1_bf = w_ff1.astype(BF16)
    w2_bf = w_ff2.astype(BF16)
    g_mix_v, g_ff_v = vec(g_mix), vec(g_ff)
    bias_all = _attn_bias_tables(rpb)

    h_ctx = ctx
    for l in range(depth):
        last = l == depth - 1
        qkv_c, pfc_c, gate_c = _in_proj(h_ctx, g_mix_v, ada, l, cos_t, sin_t, w_in_bf, rope=False, mod_row=bsz)
        if not last:
            attn_c = _ctx_attn(qkv_c)
            ctx_mid = _mixer_tail(attn_c, pfc_c, gate_c, h_ctx, ada, l, p, small=True, mod_row=bsz)
            ctx_next = _mlp(ctx_mid, g_ff_v, ada, l, w1_bf, w2_bf, g_final, final_norm=False, mod_row=bsz)

        qkv, pfc, gate = _in_proj(x, g_mix_v, ada, l, cos_t, sin_t, w_in_bf, rope=True, mod_row=None)
        attn = _nattn(qkv, qkv_c, bias_all, l)
        x = _mixer_tail(attn, pfc, gate, x, ada, l, p, small=False, mod_row=None)
        x = _mlp(x, g_ff_v, ada, l, w1_bf, w2_bf, g_final, final_norm=last, mod_row=None)
        if not last:
            h_ctx = ctx_next
    return x
```

```python
import functools

import numpy as np
import jax
import jax.numpy as jnp
from jax import lax
from jax.experimental import pallas as pl
from jax.experimental.pallas import tpu as pltpu

F32 = jnp.float32
BF16 = jnp.bfloat16

D_MODEL = 1024
DEPTH = 4
GRID_W = 64
NA_HEADS = 8
HEAD_DIM = 64
NA_WIDTH = NA_HEADS * HEAD_DIM
NA_KH = 8
NA_KW = 16
ROPE_AXIS_DIM = HEAD_DIM // 2
ROPE_THETA = 10000.0
POOL_WIDTH = 256
POOL_GROUPS = 4
POOL_GROUP_DIM = 64
POOL_WINDOWS = (2, 4, 8, 16)
FOURIER_WIDTH = 256
FOURIER_GROUP_DIM = 64
CONV_WIDTH = 256
CONV_K = 31
N_BRANCH = 4
D_FF = 4 * D_MODEL
EPS = 1e-6
QKV_WIDTH = 3 * NA_WIDTH
PFC_WIDTH = POOL_WIDTH + FOURIER_WIDTH + 2 * CONV_WIDTH
GATE_WIDTH = N_BRANCH * D_MODEL
IN_WIDTH = QKV_WIDTH + PFC_WIDTH + GATE_WIDTH

LANES = 128
SUBLANES = 8
HALO = 16
MASK_VALUE = -1e30
LOG2E = 1.4426950408889634
Q_SCALE = HEAD_DIM ** -0.5 * LOG2E

IN_TILE = 512
N_IN_TILES = IN_WIDTH // IN_TILE
N_QKV_TILES = QKV_WIDTH // IN_TILE
N_PFC_TILES = PFC_WIDTH // IN_TILE
IN_DOT_TILES = 2
MIX_WIDTH = POOL_WIDTH + FOURIER_WIDTH + CONV_WIDTH

FFT_N1 = 128
FFT_N2 = 64


def _params(sem, vmem_mb):
    return pltpu.CompilerParams(dimension_semantics=sem, vmem_limit_bytes=vmem_mb * 1024 * 1024)


def _bf16_const(a):
    return jnp.asarray(a, F32).astype(BF16)


def _layer_spec(arr, l):
    nd = arr.ndim - 1
    return pl.BlockSpec((1,) + arr.shape[1:], lambda *_: (l,) + (0,) * nd)


def _ada_spec(ada, l, chunk):
    return pl.BlockSpec((1, ada.shape[1], D_MODEL), lambda *_: (l, 0, chunk))


def _mod_row(ada_ref, mod_row):
    row = pl.program_id(0) if mod_row is None else mod_row
    return ada_ref[0, pl.ds(row, 1), :]


def _ada_kernel(c_ref, w_ref, b_ref, o_ref):
    cv = c_ref[...]
    s = cv * jax.nn.sigmoid(cv)
    o_ref[0] = jnp.dot(s.astype(BF16), w_ref[0].astype(BF16), preferred_element_type=F32) + b_ref[0]


def _ada_all(cvecs, w_mod, b_mod):
    depth, d, width = w_mod.shape
    rows = cvecs.shape[0]
    tn = 1536
    return pl.pallas_call(
        _ada_kernel,
        grid=(depth, width // tn),
        in_specs=[
            pl.BlockSpec((rows, d), lambda l, j: (0, 0)),
            pl.BlockSpec((1, d, tn), lambda l, j: (l, 0, j)),
            pl.BlockSpec((1, 1, tn), lambda l, j: (l, 0, j)),
        ],
        out_specs=pl.BlockSpec((1, rows, tn), lambda l, j: (l, 0, j)),
        out_shape=jax.ShapeDtypeStruct((depth, rows, width), F32),
        compiler_params=_params(("arbitrary", "arbitrary"), 32),
        name="ada",
    )(cvecs, w_mod, b_mod.reshape(depth, 1, width))


def _norm_modulate(x, g, shift, scale):
    ms = jnp.mean(x * x, axis=-1, keepdims=True)
    y = x * lax.rsqrt(ms + EPS) * g
    return y * (1.0 + scale) + shift


def _norm_next_tile(dst_ref, x_ref, g_ref, sh_ref, sc_ref, tile, mod_row, tiles_per_batch):
    row = mod_row if mod_row is not None else tile // tiles_per_batch
    dst_ref[...] = _norm_modulate(x_ref[0], g_ref[0], sh_ref[0, pl.ds(row, 1), :],
                                  sc_ref[0, pl.ds(row, 1), :]).astype(BF16)


def _ping_pong(step, body, ref_a, ref_b):
    pl.when(step % 2 == 0)(lambda: body(ref_a, ref_b))
    pl.when(step % 2 == 1)(lambda: body(ref_b, ref_a))


def _in_proj_kernel(x_ref, xn_ref, g_ref, sh_ref, sc_ref, cos_ref, sin_ref, w_ref,
                    qkv_ref, pfc_ref, gate_ref, ha_ref, hb_ref, *, rope, mod_row, tiles_per_batch):
    step = pl.program_id(0)
    last = pl.num_programs(0) - 1
    norm = functools.partial(_norm_next_tile, g_ref=g_ref, sh_ref=sh_ref, sc_ref=sc_ref,
                             mod_row=mod_row, tiles_per_batch=tiles_per_batch)

    @pl.when(step == 0)
    def _():
        norm(ha_ref, x_ref, tile=0)

    def body(cur_ref, nxt_ref):
        _in_proj_tile(cur_ref, lambda: norm(nxt_ref, xn_ref, tile=jnp.minimum(step + 1, last)),
                      cos_ref, sin_ref, w_ref, qkv_ref, pfc_ref, gate_ref, rope=rope)

    _ping_pong(step, body, ha_ref, hb_ref)


def _in_proj_tile(h_ref, prepare_next, cos_ref, sin_ref, w_ref, qkv_ref, pfc_ref, gate_ref, *, rope):
    def mm(t0, n_sub):
        return jnp.dot(h_ref[...], w_ref[0, :, t0 * IN_TILE:(t0 + n_sub) * IN_TILE], preferred_element_type=F32)

    def qk_epilogue(t, acc):
        col0 = t * IN_TILE
        mult = Q_SCALE if t == 0 else None
        if rope:
            cos = cos_ref[...]
            sin = sin_ref[...]
            if mult is not None:
                cos, sin = cos * mult, sin * mult
            lane = lax.broadcasted_iota(jnp.int32, cos.shape, 1)
            first = (lane % (ROPE_AXIS_DIM)) < (ROPE_AXIS_DIM // 2)
            for cc in range(IN_TILE // LANES):
                a = acc[:, cc * LANES:(cc + 1) * LANES]
                partner = jnp.where(first,
                                    pltpu.roll(a, LANES - ROPE_AXIS_DIM // 2, 1),
                                    pltpu.roll(a, ROPE_AXIS_DIM // 2, 1))
                qkv_ref[0, :, col0 + cc * LANES:col0 + (cc + 1) * LANES] = (a * cos + partner * sin).astype(BF16)
        else:
            qkv_ref[0, :, col0:col0 + IN_TILE] = (acc if mult is None else acc * mult).astype(BF16)

    def epilogue(t, acc):
        if t < 2:
            qk_epilogue(t, acc)
        elif t < N_QKV_TILES:
            qkv_ref[0, :, t * IN_TILE:(t + 1) * IN_TILE] = acc.astype(BF16)
        elif t == N_QKV_TILES:
            pfc_ref[0, :, 0:IN_TILE] = acc
        elif t == N_QKV_TILES + 1:
            pfc_ref[0, :, IN_TILE:IN_TILE + CONV_WIDTH] = _glu(acc)
        else:
            c0 = (t - N_QKV_TILES - N_PFC_TILES) * IN_TILE
            gate_ref[0, :, c0:c0 + IN_TILE] = jax.nn.sigmoid(acc).astype(BF16)

    starts = list(range(0, N_IN_TILES, IN_DOT_TILES))
    width = lambda t0: min(IN_DOT_TILES, N_IN_TILES - t0)
    pending = mm(starts[0], width(starts[0]))
    prepare_next()
    for si, t0 in enumerate(starts):
        nxt = mm(starts[si + 1], width(starts[si + 1])) if si + 1 < len(starts) else None
        for sub in range(width(t0)):
            epilogue(t0 + sub, pending[:, sub * IN_TILE:(sub + 1) * IN_TILE])
        pending = nxt


def _in_proj(x, g_all, ada, l, cos_t, sin_t, w_all, *, rope, mod_row):
    bsz, n, d = x.shape
    rows = bsz * n
    tm = 512
    tiles_per_batch = n // tm if mod_row is None else 1
    n_tiles = rows // tm
    rope_tiles = cos_t.shape[0] // tm
    xf = x.reshape(1, rows, d)
    row_spec = lambda w: pl.BlockSpec((1, tm, w), lambda t: (0, t, 0))
    outs = pl.pallas_call(
        functools.partial(_in_proj_kernel, rope=rope, mod_row=mod_row, tiles_per_batch=tiles_per_batch),
        grid=(n_tiles,),
        in_specs=[
            row_spec(d),
            pl.BlockSpec((1, tm, d), lambda t: (0, jnp.minimum(t + 1, n_tiles - 1), 0)),
            _layer_spec(g_all, l),
            _ada_spec(ada, l, 0),
            _ada_spec(ada, l, 1),
            pl.BlockSpec((tm, LANES), lambda t: (t % rope_tiles, 0)),
            pl.BlockSpec((tm, LANES), lambda t: (t % rope_tiles, 0)),
            pl.BlockSpec((1,) + w_all.shape[1:], lambda t: (l, 0, 0), pipeline_mode=pl.Buffered(1)),
        ],
        out_specs=[row_spec(QKV_WIDTH), row_spec(MIX_WIDTH), row_spec(GATE_WIDTH)],
        out_shape=[
            jax.ShapeDtypeStruct((1, rows, QKV_WIDTH), BF16),
            jax.ShapeDtypeStruct((1, rows, MIX_WIDTH), F32),
            jax.ShapeDtypeStruct((1, rows, GATE_WIDTH), BF16),
        ],
        scratch_shapes=[pltpu.VMEM((tm, d), BF16), pltpu.VMEM((tm, d), BF16)],
        compiler_params=_params(("arbitrary",), 56),
        name="in_proj",
    )(xf, xf, g_all, ada, ada, cos_t, sin_t, w_all)
    return [o.reshape(bsz, n, o.shape[-1]) for o in outs]


def _stack_head_pair(qp):
    lane = lax.broadcasted_iota(jnp.int32, qp.shape, 1)
    zero = jnp.zeros_like(qp)
    return jnp.concatenate([jnp.where(lane < HEAD_DIM, qp, zero), jnp.where(lane < HEAD_DIM, zero, qp)], axis=0)


def _unstack_head_pair(o2):
    m = o2.shape[0] // 2
    lane = lax.broadcasted_iota(jnp.int32, (m, LANES), 1)
    return jnp.where(lane < HEAD_DIM, o2[:m], o2[m:])


_NT = (((1,), (1,)), ((), ()))


def _nattn_kernel(q_ref, k_ref, v_ref, kc_ref, vc_ref, bias_ref, o_ref, *, rows_per_step, n_rows):
    i = pl.program_id(1)
    win = NA_KH * GRID_W
    rows_per_iter = rows_per_step
    lookahead = 2

    def scores(rl, p):
        r = i * rows_per_step + rl
        rs = jnp.clip(r - NA_KH // 2, 0, n_rows - NA_KH)
        var = rs - r + (NA_KH - 1)
        qoff = pl.multiple_of(rl * GRID_W, GRID_W)
        koff = pl.multiple_of(rs * GRID_W, GRID_W)
        cs = slice(p * LANES, (p + 1) * LANES)
        q2 = _stack_head_pair(q_ref[0, pl.ds(qoff, GRID_W), cs])
        kp = k_ref[0, pl.ds(koff, win), cs]
        s_w = lax.dot_general(q2, kp, _NT, preferred_element_type=F32) + bias_ref[0, var, p]
        s_c = lax.dot_general(q2, kc_ref[0, :, cs], _NT, preferred_element_type=F32)
        return s_w, s_c, qoff, koff, cs

    def finish(s_w, s_c, qoff, koff, cs):
        m = jnp.maximum(jnp.max(s_w, axis=-1, keepdims=True), jnp.max(s_c, axis=-1, keepdims=True))
        p_w = jnp.exp2(s_w - m)
        p_c = jnp.exp2(s_c - m)
        l = jnp.sum(p_w, axis=-1, keepdims=True) + jnp.sum(p_c, axis=-1, keepdims=True)
        o2 = (jnp.dot(p_w.astype(BF16), v_ref[0, pl.ds(koff, win), cs], preferred_element_type=F32)
              + jnp.dot(p_c.astype(BF16), vc_ref[0, :, cs], preferred_element_type=F32))
        o2 = o2 / l
        o_ref[0, pl.ds(qoff, GRID_W), cs] = _unstack_head_pair(o2).astype(BF16)

    def body(it, carry):
        units = [(it * rows_per_iter + u, p) for u in range(rows_per_iter) for p in range(NA_HEADS // 2)]
        pending = [scores(*u) for u in units[:lookahead]]
        for idx in range(len(units)):
            if idx + lookahead < len(units):
                pending.append(scores(*units[idx + lookahead]))
            finish(*pending.pop(0))
        return carry

    lax.fori_loop(0, rows_per_step // rows_per_iter, body, 0)


def _nattn(qkv, qkv_ctx, bias_all, l):
    bsz, n, _ = qkv.shape
    n_ctx = qkv_ctx.shape[1]
    n_rows = n // GRID_W
    rows_per_step = 8
    tq = rows_per_step * GRID_W
    single = pl.Buffered(1)
    return pl.pallas_call(
        functools.partial(_nattn_kernel, rows_per_step=rows_per_step, n_rows=n_rows),
        grid=(bsz, n_rows // rows_per_step),
        in_specs=[
            pl.BlockSpec((1, tq, NA_WIDTH), lambda b, i: (b, i, 0)),
            pl.BlockSpec((1, n, NA_WIDTH), lambda b, i: (b, 0, 1), pipeline_mode=single),
            pl.BlockSpec((1, n, NA_WIDTH), lambda b, i: (b, 0, 2), pipeline_mode=single),
            pl.BlockSpec((1, n_ctx, NA_WIDTH), lambda b, i: (b, 0, 1)),
            pl.BlockSpec((1, n_ctx, NA_WIDTH), lambda b, i: (b, 0, 2)),
            pl.BlockSpec((1,) + bias_all.shape[1:], lambda b, i: (l, 0, 0, 0, 0), pipeline_mode=single),
        ],
        out_specs=pl.BlockSpec((1, tq, NA_WIDTH), lambda b, i: (b, i, 0)),
        out_shape=jax.ShapeDtypeStruct((bsz, n, NA_WIDTH), BF16),
        compiler_params=_params(("arbitrary", "arbitrary"), 48),
        name="nattn",
    )(qkv, qkv, qkv, qkv_ctx, qkv_ctx, bias_all)


def _ctx_attn_kernel(q_ref, k_ref, v_ref, o_ref):
    for p in range(NA_HEADS // 2):
        cs = slice(p * LANES, (p + 1) * LANES)
        q2 = _stack_head_pair(q_ref[0, :, cs])
        s = lax.dot_general(q2, k_ref[0, :, cs], _NT, preferred_element_type=F32)
        m = jnp.max(s, axis=-1, keepdims=True)
        e = jnp.exp2(s - m)
        l = jnp.sum(e, axis=-1, keepdims=True)
        o2 = jnp.dot(e.astype(BF16), v_ref[0, :, cs], preferred_element_type=F32) / l
        o_ref[0, :, cs] = _unstack_head_pair(o2).astype(BF16)


def _ctx_attn(qkv_ctx):
    bsz, n_ctx, _ = qkv_ctx.shape
    return pl.pallas_call(
        _ctx_attn_kernel,
        grid=(bsz,),
        in_specs=[pl.BlockSpec((1, n_ctx, NA_WIDTH), lambda b, c=c: (b, 0, c)) for c in range(3)],
        out_specs=pl.BlockSpec((1, n_ctx, NA_WIDTH), lambda b: (b, 0, 0)),
        out_shape=jax.ShapeDtypeStruct((bsz, n_ctx, NA_WIDTH), BF16),
        compiler_params=_params(("arbitrary",), 32),
        name="ctx_attn",
    )(qkv_ctx, qkv_ctx, qkv_ctx)


def _attn_bias_tables(rpb):
    depth = rpb.shape[0]
    cq = np.arange(GRID_W)
    kc = np.arange(GRID_W)
    col_start = np.clip(cq - NA_KW // 2, 0, GRID_W - NA_KW)
    valid = (kc[None, :] >= col_start[:, None]) & (kc[None, :] < col_start[:, None] + NA_KW)
    dcol = kc[None, :] - cq[:, None] + (NA_KW - 1)
    onehot = (valid[:, :, None] & (dcol[:, :, None] == np.arange(2 * NA_KW - 1)[None, None, :])).astype(np.float32)
    plane = jnp.einsum("lhrd,qkd->lhrqk", rpb, jnp.asarray(onehot), precision=lax.Precision.HIGHEST)
    plane = plane * LOG2E + jnp.asarray(np.where(valid, 0.0, MASK_VALUE).astype(np.float32))
    variants = []
    for vi in range(NA_KH):
        blk = plane[:, :, vi:vi + NA_KH]
        variants.append(blk.transpose(0, 1, 3, 2, 4).reshape(depth, NA_HEADS, GRID_W, NA_KH * GRID_W))
    tbl = jnp.stack(variants, axis=1)
    return tbl.reshape(depth, NA_KH, NA_HEADS // 2, 2 * GRID_W, NA_KH * GRID_W)


def _halo_specs(tm, n, width, col_block):
    per = tm // HALO
    last = n // HALO - 1
    return [
        pl.BlockSpec((1, HALO, width), lambda b, i: (b, jnp.maximum(i * per - 1, 0), col_block)),
        pl.BlockSpec((1, tm, width), lambda b, i: (b, i, col_block)),
        pl.BlockSpec((1, HALO, width), lambda b, i: (b, jnp.minimum((i + 1) * per, last), col_block)),
    ]


def _pool_fill(ext_ref, prev_ref, cur_ref, next_ref, tm):
    i = pl.program_id(1)
    nt = pl.num_programs(1)
    ext_ref[0:HALO] = jnp.where(i > 0, prev_ref[0], 0.0)
    ext_ref[HALO:HALO + tm] = cur_ref[0]
    ext_ref[HALO + tm:HALO + tm + HALO] = jnp.where(i < nt - 1, next_ref[0], 0.0)


def _pool_rows(ext_ref, wbd_ref, ps_ref, o_ref, row0, n_rows, *, tm, n_tok, rc=128):
    i = pl.program_id(1)
    rc = min(rc, n_rows)
    row = lax.broadcasted_iota(jnp.int32, (rc, LANES), 0)
    lane = lax.broadcasted_iota(jnp.int32, (rc, LANES), 1)
    first = lane < POOL_GROUP_DIM

    for ci in range(n_rows // rc):
        base = row0 + ci * rc
        t = i * tm + base + row
        halves = []
        for half in range(2):
            cs = slice(half * LANES, (half + 1) * LANES)
            w_a, w_b = POOL_WINDOWS[2 * half], POOL_WINDOWS[2 * half + 1]
            lo_a, lo_b = w_a // 2, w_b // 2
            hi_a, hi_b = w_a - lo_a - 1, w_b - lo_b - 1

            def ld(off, cs=cs):
                return ext_ref[pl.ds(base + (HALO + off), rc), cs]

            x0 = ld(0)
            s_a = x0
            for off in range(-lo_a, hi_a + 1):
                if off != 0:
                    s_a = s_a + ld(off)
            s_b = s_a
            for off in list(range(-lo_b, -lo_a)) + list(range(hi_a + 1, hi_b + 1)):
                s_b = s_b + ld(off)
            lo = jnp.where(first, lo_a, lo_b)
            hi = jnp.where(first, hi_a, hi_b)
            cnt = jnp.minimum(t + hi + 1, n_tok) - jnp.maximum(t - lo, 0)
            halves.append(jnp.where(first, s_a, s_b) / cnt.astype(F32) - x0)
        d = jnp.concatenate(halves, axis=1).astype(BF16)
        y = jnp.dot(d, wbd_ref[0], preferred_element_type=F32) * ps_ref[0]
        o_ref[pl.ds(base, rc), :] = y.astype(BF16)


def _glu(u):
    return u[:, :CONV_WIDTH] * jax.nn.sigmoid(u[:, CONV_WIDTH:])


def _conv_fill(z_ref, prev_ref, cur_ref, next_ref, tm):
    i = pl.program_id(1)
    nt = pl.num_programs(1)
    ext = tm + 2 * HALO
    z_ref[0, 0:HALO] = jnp.where(i > 0, prev_ref[0], 0.0)
    z_ref[0, HALO:HALO + tm] = cur_ref[0]
    z_ref[0, HALO + tm:ext] = jnp.where(i < nt - 1, next_ref[0], 0.0)
    shifted_rows = ext - SUBLANES
    copy_rows = 128
    for s in range(1, SUBLANES):
        for c0 in range(0, shifted_rows, copy_rows):
            size = min(copy_rows, shifted_rows - c0)
            z_ref[s, c0:c0 + size] = z_ref[0, c0 + s:c0 + s + size]


def _conv_rows(z_ref, wdw_ref, bdw_ref, lng_ref, lnb_ref, o_ref, row0, n_rows, *, rc=64):
    rc = min(rc, n_rows)
    first_tap = HALO - CONV_K // 2

    for ci in range(n_rows // rc):
        base = row0 + ci * rc
        acc = jnp.broadcast_to(bdw_ref[0], (rc, CONV_WIDTH))
        for j in range(CONV_K):
            off = first_tap + j
            a0 = base + (off // SUBLANES) * SUBLANES
            acc = acc + z_ref[off % SUBLANES, a0:a0 + rc, :] * wdw_ref[0, j:j + 1, :]
        mu = jnp.mean(acc, axis=-1, keepdims=True)
        cen = acc - mu
        var = jnp.mean(cen * cen, axis=-1, keepdims=True)
        y = cen * lax.rsqrt(var + EPS) * lng_ref[0] + lnb_ref[0]
        o_ref[pl.ds(base, rc), :] = (y * jax.nn.sigmoid(y)).astype(BF16)


def _channel_dft_mats(n_groups):
    c = np.arange(FOURIER_GROUP_DIM)
    ang = 2.0 * np.pi * ((c[:, None] * c[None, :]) % FOURIER_GROUP_DIM) / FOURIER_GROUP_DIM
    eye = np.eye(n_groups)
    return _bf16_const(np.kron(eye, np.cos(ang))), _bf16_const(np.kron(eye, np.sin(ang)))


def _fft_tables(n):
    assert n == FFT_N1 * FFT_N2
    k1 = np.arange(FFT_N1)
    n1 = np.arange(FFT_N1)
    n2 = np.arange(FFT_N2)
    prod = (k1[None, :, None] * (FFT_N2 * n1[None, None, :] + n2[:, None, None])) % n
    th = 2.0 * np.pi * prod / n
    cs, sn = np.cos(th), np.sin(th)
    m_big = np.concatenate([np.concatenate([cs, -sn], axis=2), np.concatenate([-sn, -cs], axis=2)], axis=1)
    k2 = np.arange(FFT_N2)
    psi = 2.0 * np.pi * ((k2[:, None] * n2[None, :]) % FFT_N2) / FFT_N2
    w_int = np.stack([np.cos(psi), np.sin(psi)], axis=2).reshape(FFT_N2, 2 * FFT_N2)
    return _bf16_const(m_big), _bf16_const(w_int)


def _fourier_kernel(x_ref, cc_ref, sc_ref, mbig_ref, wint_ref, o_ref, ab_ref, t_ref, *, n):
    xb = x_ref[0].astype(BF16)
    ab_ref[0] = jnp.dot(xb, cc_ref[...], preferred_element_type=F32)
    ab_ref[1] = jnp.dot(xb, sc_ref[...], preferred_element_type=F32)

    def stage1(n2, carry):
        za = ab_ref[0, pl.ds(n2, FFT_N1, stride=FFT_N2), :]
        zb = ab_ref[1, pl.ds(n2, FFT_N1, stride=FFT_N2), :]
        z = jnp.concatenate([za, zb], axis=0).astype(BF16)
        off = pl.multiple_of(n2 * (2 * FFT_N1), 2 * FFT_N1)
        t_ref[pl.ds(off, 2 * FFT_N1), :] = jnp.dot(mbig_ref[n2], z, preferred_element_type=F32)
        return carry

    lax.fori_loop(0, FFT_N2, stage1, 0, unroll=8)

    scale = 1.0 / np.sqrt(float(n * FOURIER_GROUP_DIM))

    def stage2(k1, carry):
        t = t_ref[pl.ds(k1, 2 * FFT_N2, stride=FFT_N1), :].astype(BF16)
        y = jnp.dot(wint_ref[...], t, preferred_element_type=F32) * scale
        o_ref[0, pl.ds(k1, FFT_N2, stride=FFT_N1), :] = y
        return carry

    lax.fori_loop(0, FFT_N1, stage2, 0, unroll=16)


def _fourier_mix(pfc):
    bsz, n, _ = pfc.shape
    cc, sc = _channel_dft_mats(LANES // FOURIER_GROUP_DIM)
    m_big, w_int = _fft_tables(n)
    col0 = POOL_WIDTH // LANES
    const2 = lambda b, h: (0, 0)
    return pl.pallas_call(
        functools.partial(_fourier_kernel, n=n),
        grid=(bsz, FOURIER_WIDTH // LANES),
        in_specs=[
            pl.BlockSpec((1, n, LANES), lambda b, h: (b, 0, col0 + h)),
            pl.BlockSpec((LANES, LANES), const2),
            pl.BlockSpec((LANES, LANES), const2),
            pl.BlockSpec(m_big.shape, lambda b, h: (0, 0, 0), pipeline_mode=pl.Buffered(1)),
            pl.BlockSpec(w_int.shape, const2),
        ],
        out_specs=pl.BlockSpec((1, n, LANES), lambda b, h: (b, 0, h)),
        out_shape=jax.ShapeDtypeStruct((bsz, n, FOURIER_WIDTH), F32),
        scratch_shapes=[pltpu.VMEM((2, n, LANES), F32), pltpu.VMEM((2 * n, LANES), F32)],
        compiler_params=_params(("arbitrary", "arbitrary"), 56),
        name="fourier",
    )(pfc, cc, sc, m_big, w_int)


def _fourier_small_kernel(x_ref, cc_ref, sc_ref, cn_ref, sn_ref, o_ref, *, n):
    xb = x_ref[0].astype(BF16)
    a = jnp.dot(xb, cc_ref[...], preferred_element_type=F32).astype(BF16)
    b = jnp.dot(xb, sc_ref[...], preferred_element_type=F32).astype(BF16)
    y = (jnp.dot(cn_ref[...], a, preferred_element_type=F32) - jnp.dot(sn_ref[...], b, preferred_element_type=F32))
    o_ref[0] = y * (1.0 / np.sqrt(float(n * FOURIER_GROUP_DIM)))


def _fourier_mix_small(pfc):
    bsz, n, _ = pfc.shape
    cc, sc = _channel_dft_mats(FOURIER_WIDTH // FOURIER_GROUP_DIM)
    k = np.arange(n)
    th = 2.0 * np.pi * ((k[:, None] * k[None, :]) % n) / n
    cn, sn = _bf16_const(np.cos(th)), _bf16_const(np.sin(th))
    const2 = lambda b: (0, 0)
    return pl.pallas_call(
        functools.partial(_fourier_small_kernel, n=n),
        grid=(bsz,),
        in_specs=[
            pl.BlockSpec((1, n, FOURIER_WIDTH), lambda b: (b, 0, POOL_WIDTH // FOURIER_WIDTH)),
            pl.BlockSpec((FOURIER_WIDTH, FOURIER_WIDTH), const2),
            pl.BlockSpec((FOURIER_WIDTH, FOURIER_WIDTH), const2),
            pl.BlockSpec((n, n), const2),
            pl.BlockSpec((n, n), const2),
        ],
        out_specs=pl.BlockSpec((1, n, FOURIER_WIDTH), lambda b: (b, 0, 0)),
        out_shape=jax.ShapeDtypeStruct((bsz, n, FOURIER_WIDTH), F32),
        compiler_params=_params(("arbitrary",), 32),
        name="fourier_small",
    )(pfc, cc, sc, cn, sn)


def _merge_kernel(attn_ref, four_ref, gate_ref, x_ref, pp_ref, pc_ref, pn_ref, cp_ref, cc_ref, cn_ref, gt_ref,
                  wbd_ref, ps_ref, wdw_ref, bdw_ref, lng_ref, lnb_ref,
                  wa_ref, wp_ref, wf_ref, wc_ref, wo_ref, o_ref, ext_ref, z_ref, pool_ref, conv_ref,
                  *, mod_row, rc, n_tok):
    d = D_MODEL
    tm = x_ref.shape[1]
    gt = _mod_row(gt_ref, mod_row)
    _pool_fill(ext_ref, pp_ref, pc_ref, pn_ref, tm)
    _conv_fill(z_ref, cp_ref, cc_ref, cn_ref, tm)

    def mixers(row0):
        _pool_rows(ext_ref, wbd_ref, ps_ref, pool_ref, row0, rc, tm=tm, n_tok=n_tok)
        _conv_rows(z_ref, wdw_ref, bdw_ref, lng_ref, lnb_ref, conv_ref, row0, rc)

    def branch_dots(rows):
        acts = (attn_ref[0, rows, :], pool_ref[rows, :], four_ref[0, rows, :].astype(BF16), conv_ref[rows, :])
        return [jnp.dot(a, w_ref[0], preferred_element_type=F32)
                for a, w_ref in zip(acts, (wa_ref, wp_ref, wf_ref, wc_ref))]

    def finish(rows, ys):
        merged = None
        for bi, y in enumerate(ys):
            term = gate_ref[0, rows, bi * d:(bi + 1) * d] * y.astype(BF16)
            merged = term if merged is None else merged + term
        out = jnp.dot(merged, wo_ref[0], preferred_element_type=F32)
        o_ref[0, rows, :] = x_ref[0, rows, :] + gt * out

    chunks = [slice(c * rc, (c + 1) * rc) for c in range(tm // rc)]
    mixers(0)
    pending = branch_dots(chunks[0])
    for ci, rows in enumerate(chunks):
        nxt = None
        if ci + 1 < len(chunks):
            mixers((ci + 1) * rc)
            nxt = branch_dots(chunks[ci + 1])
        finish(rows, pending)
        pending = nxt


def _merge(attn, four_o, gate, x, pfc, ada, l, p, *, mod_row):
    bsz, n, d = x.shape
    tm = min(n, 512)
    rc = min(tm, 256)
    act = lambda w: pl.BlockSpec((1, tm, w), lambda b, i: (b, i, 0))
    params = [p[k] for k in ("wbd", "pool_scale", "w_dw", "b_dw", "ln_g", "ln_b", "wa", "wp", "wf", "wc", "wo")]
    return pl.pallas_call(
        functools.partial(_merge_kernel, mod_row=mod_row, rc=rc, n_tok=n),
        grid=(bsz, n // tm),
        in_specs=[act(NA_WIDTH), act(FOURIER_WIDTH), act(GATE_WIDTH), act(d)]
        + _halo_specs(tm, n, POOL_WIDTH, 0) + _halo_specs(tm, n, CONV_WIDTH, MIX_WIDTH // CONV_WIDTH - 1)
        + [_ada_spec(ada, l, 2)] + [_layer_spec(w, l) for w in params],
        out_specs=act(d),
        out_shape=jax.ShapeDtypeStruct((bsz, n, d), F32),
        scratch_shapes=[
            pltpu.VMEM((tm + 2 * HALO, POOL_WIDTH), F32),
            pltpu.VMEM((SUBLANES, tm + 2 * HALO, CONV_WIDTH), F32),
            pltpu.VMEM((tm, POOL_WIDTH), BF16),
            pltpu.VMEM((tm, CONV_WIDTH), BF16),
        ],
        compiler_params=_params(("arbitrary", "arbitrary"), 56),
        name="merge",
    )(attn, four_o, gate, x, pfc, pfc, pfc, pfc, pfc, pfc, ada, *params)


def _mlp_kernel(x_ref, xn_ref, g_ref, sh_ref, sc_ref, gt_ref, w1_ref, w2_ref, gfin_ref, o_ref, ha_ref, hb_ref,
                *, final_norm, mod_row, tiles_per_batch, tf):
    step = pl.program_id(0)
    last = pl.num_programs(0) - 1
    norm = functools.partial(_norm_next_tile, g_ref=g_ref, sh_ref=sh_ref, sc_ref=sc_ref,
                             mod_row=mod_row, tiles_per_batch=tiles_per_batch)

    @pl.when(step == 0)
    def _():
        norm(ha_ref, x_ref, tile=0)

    def body(cur_ref, nxt_ref):
        def up(f):
            return jnp.dot(cur_ref[...], w1_ref[0, :, f * tf:(f + 1) * tf], preferred_element_type=F32)

        pending = up(0)
        norm(nxt_ref, xn_ref, tile=jnp.minimum(step + 1, last))
        acc = None
        for f in range(D_FF // tf):
            nxt = up(f + 1) if (f + 1) * tf < D_FF else None
            a = jnp.square(jnp.maximum(pending, 0.0)).astype(BF16)
            o = jnp.dot(a, w2_ref[0, f * tf:(f + 1) * tf, :], preferred_element_type=F32)
            acc = o if acc is None else acc + o
            pending = nxt
        row = mod_row if mod_row is not None else step // tiles_per_batch
        y = x_ref[0] + gt_ref[0, pl.ds(row, 1), :] * acc
        if final_norm:
            ms = jnp.mean(y * y, axis=-1, keepdims=True)
            y = y * lax.rsqrt(ms + EPS) * gfin_ref[...]
        o_ref[0] = y

    _ping_pong(step, body, ha_ref, hb_ref)


def _mlp(x, g_all, ada, l, w1_all, w2_all, g_final, *, final_norm, mod_row):
    bsz, n, d = x.shape
    rows = bsz * n
    tm = 512
    tf = 1024
    tiles_per_batch = n // tm if mod_row is None else 1
    n_tiles = rows // tm
    xf = x.reshape(1, rows, d)
    single = pl.Buffered(1)
    out = pl.pallas_call(
        functools.partial(_mlp_kernel, final_norm=final_norm, mod_row=mod_row,
                          tiles_per_batch=tiles_per_batch, tf=tf),
        grid=(n_tiles,),
        in_specs=[
            pl.BlockSpec((1, tm, d), lambda t: (0, t, 0)),
            pl.BlockSpec((1, tm, d), lambda t: (0, jnp.minimum(t + 1, n_tiles - 1), 0)),
            _layer_spec(g_all, l),
            _ada_spec(ada, l, 3), _ada_spec(ada, l, 4), _ada_spec(ada, l, 5),
            pl.BlockSpec((1,) + w1_all.shape[1:], lambda t: (l, 0, 0), pipeline_mode=single),
            pl.BlockSpec((1,) + w2_all.shape[1:], lambda t: (l, 0, 0), pipeline_mode=single),
            pl.BlockSpec((1, d), lambda t: (0, 0)),
        ],
        out_specs=pl.BlockSpec((1, tm, d), lambda t: (0, t, 0)),
        out_shape=jax.ShapeDtypeStruct((1, rows, d), F32),
        scratch_shapes=[pltpu.VMEM((tm, d), BF16), pltpu.VMEM((tm, d), BF16)],
        compiler_params=_params(("arbitrary",), 56),
        name="mlp",
    )(xf, xf, g_all, ada, ada, ada, w1_all, w2_all, g_final.reshape(1, d))
    return out.reshape(bsz, n, d)


def _rope_tables(n):
    t = jnp.arange(n)
    row = (t // GRID_W).astype(F32)
    col = (t % GRID_W).astype(F32)
    inv = ROPE_THETA ** (-jnp.arange(0, ROPE_AXIS_DIM, 2, dtype=F32) / ROPE_AXIS_DIM)
    ang_r = row[:, None] * inv
    ang_c = col[:, None] * inv
    cr, sr, cc, sc = jnp.cos(ang_r), jnp.sin(ang_r), jnp.cos(ang_c), jnp.sin(ang_c)
    cos64 = jnp.concatenate([cr, cr, cc, cc], axis=1)
    sin64 = jnp.concatenate([-sr, sr, -sc, sc], axis=1)
    return jnp.tile(cos64, (1, LANES // HEAD_DIM)), jnp.tile(sin64, (1, LANES // HEAD_DIM))


def _pool_block_diag(w_pool):
    eye = jnp.asarray(np.eye(POOL_GROUPS, dtype=np.float32))
    bd = w_pool[:, :, :, None, :] * eye[None, :, None, :, None]
    return bd.reshape(w_pool.shape[0], POOL_WIDTH, POOL_WIDTH)


def _mixer_tail(attn, pfc, gate, x, ada, l, p, *, small, mod_row):
    four_o = _fourier_mix_small(pfc) if small else _fourier_mix(pfc)
    return _merge(attn, four_o, gate, x, pfc, ada, l, p, mod_row=mod_row)


def kernel(x, c, ctx, c_ctx, w_mod, b_mod, g_mix, g_ff, w_in, rpb, w_pool, pool_scale, w_dw, b_dw, conv_ln_g,
           conv_ln_b, w_br_attn, w_br_pool, w_br_fourier, w_br_conv, w_out, w_ff1, w_ff2, g_final):
    bsz, n, d = x.shape
    n_ctx = ctx.shape[1]
    depth = w_in.shape[0]
    cos_t, sin_t = _rope_tables(n)

    ada_rows = SUBLANES
    cvecs = jnp.concatenate([c, c_ctx[None, :], jnp.zeros((ada_rows - bsz - 1, d), F32)], axis=0)
    ada = _ada_all(cvecs, w_mod, b_mod)

    vec = lambda a: a.reshape(depth, 1, a.shape[-1])
    p = dict(
        wbd=_pool_block_diag(w_pool).astype(BF16), pool_scale=vec(pool_scale),
        w_dw=w_dw, b_dw=vec(b_dw), ln_g=vec(conv_ln_g), ln_b=vec(conv_ln_b),
        wa=w_br_attn.astype(BF16), wp=w_br_pool.astype(BF16), wf=w_br_fourier.astype(BF16),
        wc=w_br_conv.astype(BF16), wo=w_out.astype(BF16),
    )
    w_in_bf = w_in.astype(BF16)
    w1_bf = w_ff1.astype(BF16)
    w2_bf = w_ff2.astype(BF16)
    g_mix_v, g_ff_v = vec(g_mix), vec(g_ff)
    bias_all = _attn_bias_tables(rpb)

    h_ctx = ctx
    for l in range(depth):
        last = l == depth - 1
        qkv_c, pfc_c, gate_c = _in_proj(h_ctx, g_mix_v, ada, l, cos_t, sin_t, w_in_bf, rope=False, mod_row=bsz)
        if not last:
            attn_c = _ctx_attn(qkv_c)
            ctx_mid = _mixer_tail(attn_c, pfc_c, gate_c, h_ctx, ada, l, p, small=True, mod_row=bsz)
            ctx_next = _mlp(ctx_mid, g_ff_v, ada, l, w1_bf, w2_bf, g_final, final_norm=False, mod_row=bsz)

        qkv, pfc, gate = _in_proj(x, g_mix_v, ada, l, cos_t, sin_t, w_in_bf, rope=True, mod_row=None)
        attn = _nattn(qkv, qkv_c, bias_all, l)
        x = _mixer_tail(attn, pfc, gate, x, ada, l, p, small=False, mod_row=None)
        x = _mlp(x, g_ff_v, ada, l, w1_bf, w2_bf, g_final, final_norm=last, mod_row=None)
        if not last:
            h_ctx = ctx_next
    return x
```

```python
import functools

import numpy as np
import jax
import jax.numpy as jnp
from jax import lax
from jax.experimental import pallas as pl
from jax.experimental.pallas import tpu as pltpu

F32 = jnp.float32
BF16 = jnp.bfloat16

D_MODEL = 1024
DEPTH = 4
GRID_W = 64
NA_HEADS = 8
HEAD_DIM = 64
NA_WIDTH = NA_HEADS * HEAD_DIM
NA_KH = 8
NA_KW = 16
ROPE_AXIS_DIM = HEAD_DIM // 2
ROPE_THETA = 10000.0
POOL_WIDTH = 256
POOL_GROUPS = 4
POOL_GROUP_DIM = 64
POOL_WINDOWS = (2, 4, 8, 16)
FOURIER_WIDTH = 256
FOURIER_GROUP_DIM = 64
CONV_WIDTH = 256
CONV_K = 31
N_BRANCH = 4
D_FF = 4 * D_MODEL
EPS = 1e-6
QKV_WIDTH = 3 * NA_WIDTH
PFC_WIDTH = POOL_WIDTH + FOURIER_WIDTH + 2 * CONV_WIDTH
GATE_WIDTH = N_BRANCH * D_MODEL
IN_WIDTH = QKV_WIDTH + PFC_WIDTH + GATE_WIDTH

LANES = 128
SUBLANES = 8
HALO = 16
MASK_VALUE = -1e30
LOG2E = 1.4426950408889634
Q_SCALE = HEAD_DIM ** -0.5 * LOG2E

IN_TILE = 512
N_IN_TILES = IN_WIDTH // IN_TILE
N_QKV_TILES = QKV_WIDTH // IN_TILE
N_PFC_TILES = PFC_WIDTH // IN_TILE
IN_DOT_TILES = 2
MIX_WIDTH = POOL_WIDTH + FOURIER_WIDTH + CONV_WIDTH

FFT_N1 = 128
FFT_N2 = 64


def _params(sem, vmem_mb):
    return pltpu.CompilerParams(dimension_semantics=sem, vmem_limit_bytes=vmem_mb * 1024 * 1024)


def _bf16_const(a):
    return jnp.asarray(a, F32).astype(BF16)


def _layer_spec(arr, l):
    nd = arr.ndim - 1
    return pl.BlockSpec((1,) + arr.shape[1:], lambda *_: (l,) + (0,) * nd)


def _ada_spec(ada, l, chunk):
    return pl.BlockSpec((1, ada.shape[1], D_MODEL), lambda *_: (l, 0, chunk))


def _mod_row(ada_ref, mod_row):
    row = pl.program_id(0) if mod_row is None else mod_row
    return ada_ref[0, pl.ds(row, 1), :]


def _ada_kernel(c_ref, w_ref, b_ref, o_ref):
    cv = c_ref[...]
    s = cv * jax.nn.sigmoid(cv)
    o_ref[0] = jnp.dot(s.astype(BF16), w_ref[0].astype(BF16), preferred_element_type=F32) + b_ref[0]


def _ada_all(cvecs, w_mod, b_mod):
    depth, d, width = w_mod.shape
    rows = cvecs.shape[0]
    tn = 1536
    return pl.pallas_call(
        _ada_kernel,
        grid=(depth, width // tn),
        in_specs=[
            pl.BlockSpec((rows, d), lambda l, j: (0, 0)),
            pl.BlockSpec((1, d, tn), lambda l, j: (l, 0, j)),
            pl.BlockSpec((1, 1, tn), lambda l, j: (l, 0, j)),
        ],
        out_specs=pl.BlockSpec((1, rows, tn), lambda l, j: (l, 0, j)),
        out_shape=jax.ShapeDtypeStruct((depth, rows, width), F32),
        compiler_params=_params(("arbitrary", "arbitrary"), 32),
        name="ada",
    )(cvecs, w_mod, b_mod.reshape(depth, 1, width))


def _norm_modulate(x, g, shift, scale):
    ms = jnp.mean(x * x, axis=-1, keepdims=True)
    y = x * lax.rsqrt(ms + EPS) * g
    return y * (1.0 + scale) + shift


def _norm_next_tile(dst_ref, x_ref, g_ref, sh_ref, sc_ref, tile, mod_row, tiles_per_batch):
    row = mod_row if mod_row is not None else tile // tiles_per_batch
    dst_ref[...] = _norm_modulate(x_ref[0], g_ref[0], sh_ref[0, pl.ds(row, 1), :],
                                  sc_ref[0, pl.ds(row, 1), :]).astype(BF16)


def _ping_pong(step, body, ref_a, ref_b):
    pl.when(step % 2 == 0)(lambda: body(ref_a, ref_b))
    pl.when(step % 2 == 1)(lambda: body(ref_b, ref_a))


def _in_proj_kernel(x_ref, xn_ref, g_ref, sh_ref, sc_ref, cos_ref, sin_ref, w_ref,
                    qkv_ref, pfc_ref, gate_ref, ha_ref, hb_ref, *, rope, mod_row, tiles_per_batch):
    step = pl.program_id(0)
    last = pl.num_programs(0) - 1
    norm = functools.partial(_norm_next_tile, g_ref=g_ref, sh_ref=sh_ref, sc_ref=sc_ref,
                             mod_row=mod_row, tiles_per_batch=tiles_per_batch)

    @pl.when(step == 0)
    def _():
        norm(ha_ref, x_ref, tile=0)

    def body(cur_ref, nxt_ref):
        _in_proj_tile(cur_ref, lambda: norm(nxt_ref, xn_ref, tile=jnp.minimum(step + 1, last)),
                      cos_ref, sin_ref, w_ref, qkv_ref, pfc_ref, gate_ref, rope=rope)

    _ping_pong(step, body, ha_ref, hb_ref)


def _in_proj_tile(h_ref, prepare_next, cos_ref, sin_ref, w_ref, qkv_ref, pfc_ref, gate_ref, *, rope):
    def mm(t0, n_sub):
        return jnp.dot(h_ref[...], w_ref[0, :, t0 * IN_TILE:(t0 + n_sub) * IN_TILE], preferred_element_type=F32)

    def qk_epilogue(t, acc):
        col0 = t * IN_TILE
        mult = Q_SCALE if t == 0 else None
        if rope:
            cos = cos_ref[...]
            sin = sin_ref[...]
            if mult is not None:
                cos, sin = cos * mult, sin * mult
            lane = lax.broadcasted_iota(jnp.int32, cos.shape, 1)
            first = (lane % (ROPE_AXIS_DIM)) < (ROPE_AXIS_DIM // 2)
            for cc in range(IN_TILE // LANES):
                a = acc[:, cc * LANES:(cc + 1) * LANES]
                partner = jnp.where(first,
                                    pltpu.roll(a, LANES - ROPE_AXIS_DIM // 2, 1),
                                    pltpu.roll(a, ROPE_AXIS_DIM // 2, 1))
                qkv_ref[0, :, col0 + cc * LANES:col0 + (cc + 1) * LANES] = (a * cos + partner * sin).astype(BF16)
        else:
            qkv_ref[0, :, col0:col0 + IN_TILE] = (acc if mult is None else acc * mult).astype(BF16)

    def epilogue(t, acc):
        if t < 2:
            qk_epilogue(t, acc)
        elif t < N_QKV_TILES:
            qkv_ref[0, :, t * IN_TILE:(t + 1) * IN_TILE] = acc.astype(BF16)
        elif t == N_QKV_TILES:
            pfc_ref[0, :, 0:IN_TILE] = acc
        elif t == N_QKV_TILES + 1:
            pfc_ref[0, :, IN_TILE:IN_TILE + CONV_WIDTH] = _glu(acc)
        else:
            c0 = (t - N_QKV_TILES - N_PFC_TILES) * IN_TILE
            gate_ref[0, :, c0:c0 + IN_TILE] = jax.nn.sigmoid(acc).astype(BF16)

    starts = list(range(0, N_IN_TILES, IN_DOT_TILES))
    width = lambda t0: min(IN_DOT_TILES, N_IN_TILES - t0)
    pending = mm(starts[0], width(starts[0]))
    prepare_next()
    for si, t0 in enumerate(starts):
        nxt = mm(starts[si + 1], width(starts[si + 1])) if si + 1 < len(starts) else None
        for sub in range(width(t0)):
            epilogue(t0 + sub, pending[:, sub * IN_TILE:(sub + 1) * IN_TILE])
        pending = nxt


def _in_proj(x, g_all, ada, l, cos_t, sin_t, w_all, *, rope, mod_row):
    bsz, n, d = x.shape
    rows = bsz * n
    tm = 512
    tiles_per_batch = n // tm if mod_row is None else 1
    n_tiles = rows // tm
    rope_tiles = cos_t.shape[0] // tm
    xf = x.reshape(1, rows, d)
    row_spec = lambda w: pl.BlockSpec((1, tm, w), lambda t: (0, t, 0))
    outs = pl.pallas_call(
        functools.partial(_in_proj_kernel, rope=rope, mod_row=mod_row, tiles_per_batch=tiles_per_batch),
        grid=(n_tiles,),
        in_specs=[
            row_spec(d),
            pl.BlockSpec((1, tm, d), lambda t: (0, jnp.minimum(t + 1, n_tiles - 1), 0)),
            _layer_spec(g_all, l),
            _ada_spec(ada, l, 0),
            _ada_spec(ada, l, 1),
            pl.BlockSpec((tm, LANES), lambda t: (t % rope_tiles, 0)),
            pl.BlockSpec((tm, LANES), lambda t: (t % rope_tiles, 0)),
            pl.BlockSpec((1,) + w_all.shape[1:], lambda t: (l, 0, 0), pipeline_mode=pl.Buffered(1)),
        ],
        out_specs=[row_spec(QKV_WIDTH), row_spec(MIX_WIDTH), row_spec(GATE_WIDTH)],
        out_shape=[
            jax.ShapeDtypeStruct((1, rows, QKV_WIDTH), BF16),
            jax.ShapeDtypeStruct((1, rows, MIX_WIDTH), F32),
            jax.ShapeDtypeStruct((1, rows, GATE_WIDTH), BF16),
        ],
        scratch_shapes=[pltpu.VMEM((tm, d), BF16), pltpu.VMEM((tm, d), BF16)],
        compiler_params=_params(("arbitrary",), 56),
        name="in_proj",
    )(xf, xf, g_all, ada, ada, cos_t, sin_t, w_all)
    return [o.reshape(bsz, n, o.shape[-1]) for o in outs]


def _stack_head_pair(qp):
    lane = lax.broadcasted_iota(jnp.int32, qp.shape, 1)
    zero = jnp.zeros_like(qp)
    return jnp.concatenate([jnp.where(lane < HEAD_DIM, qp, zero), jnp.where(lane < HEAD_DIM, zero, qp)], axis=0)


def _unstack_head_pair(o2):
    m = o2.shape[0] // 2
    lane = lax.broadcasted_iota(jnp.int32, (m, LANES), 1)
    return jnp.where(lane < HEAD_DIM, o2[:m], o2[m:])


_NT = (((1,), (1,)), ((), ()))


def _nattn_kernel(q_ref, k_ref, v_ref, kc_ref, vc_ref, bias_ref, o_ref, *, rows_per_step, n_rows):
    i = pl.program_id(1)
    win = NA_KH * GRID_W
    rows_per_iter = rows_per_step
    lookahead = 2

    def scores(rl, p):
        r = i * rows_per_step + rl
        rs = jnp.clip(r - NA_KH // 2, 0, n_rows - NA_KH)
        var = rs - r + (NA_KH - 1)
        qoff = pl.multiple_of(rl * GRID_W, GRID_W)
        koff = pl.multiple_of(rs * GRID_W, GRID_W)
        cs = slice(p * LANES, (p + 1) * LANES)
        q2 = _stack_head_pair(q_ref[0, pl.ds(qoff, GRID_W), cs])
        kp = k_ref[0, pl.ds(koff, win), cs]
        bias = jnp.concatenate([bias_ref[0, p, var + 2 * j] for j in range(NA_KH // 2)], axis=1)
        s_w = lax.dot_general(q2, kp, _NT, preferred_element_type=F32) + bias
        s_c = lax.dot_general(q2, kc_ref[0, :, cs], _NT, preferred_element_type=F32)
        return s_w, s_c, qoff, koff, cs

    def finish(s_w, s_c, qoff, koff, cs):
        m = jnp.maximum(jnp.max(s_w, axis=-1, keepdims=True), jnp.max(s_c, axis=-1, keepdims=True))
        p_w = jnp.exp2(s_w - m)
        p_c = jnp.exp2(s_c - m)
        l = jnp.sum(p_w, axis=-1, keepdims=True) + jnp.sum(p_c, axis=-1, keepdims=True)
        o2 = (jnp.dot(p_w.astype(BF16), v_ref[0, pl.ds(koff, win), cs], preferred_element_type=F32)
              + jnp.dot(p_c.astype(BF16), vc_ref[0, :, cs], preferred_element_type=F32))
        o2 = o2 / l
        o_ref[0, pl.ds(qoff, GRID_W), cs] = _unstack_head_pair(o2).astype(BF16)

    def body(it, carry):
        units = [(it * rows_per_iter + u, p) for u in range(rows_per_iter) for p in range(NA_HEADS // 2)]
        pending = [scores(*u) for u in units[:lookahead]]
        for idx in range(len(units)):
            if idx + lookahead < len(units):
                pending.append(scores(*units[idx + lookahead]))
            finish(*pending.pop(0))
        return carry

    lax.fori_loop(0, rows_per_step // rows_per_iter, body, 0)


def _nattn(qkv, qkv_ctx, bias_all, l):
    bsz, n, _ = qkv.shape
    n_ctx = qkv_ctx.shape[1]
    n_rows = n // GRID_W
    rows_per_step = 8
    tq = rows_per_step * GRID_W
    single = pl.Buffered(1)
    return pl.pallas_call(
        functools.partial(_nattn_kernel, rows_per_step=rows_per_step, n_rows=n_rows),
        grid=(bsz, n_rows // rows_per_step),
        in_specs=[
            pl.BlockSpec((1, tq, NA_WIDTH), lambda b, i: (b, i, 0)),
            pl.BlockSpec((1, n, NA_WIDTH), lambda b, i: (b, 0, 1), pipeline_mode=single),
            pl.BlockSpec((1, n, NA_WIDTH), lambda b, i: (b, 0, 2), pipeline_mode=single),
            pl.BlockSpec((1, n_ctx, NA_WIDTH), lambda b, i: (b, 0, 1)),
            pl.BlockSpec((1, n_ctx, NA_WIDTH), lambda b, i: (b, 0, 2)),
            pl.BlockSpec((1,) + bias_all.shape[1:], lambda b, i: (l, 0, 0, 0, 0), pipeline_mode=single),
        ],
        out_specs=pl.BlockSpec((1, tq, NA_WIDTH), lambda b, i: (b, i, 0)),
        out_shape=jax.ShapeDtypeStruct((bsz, n, NA_WIDTH), BF16),
        compiler_params=_params(("arbitrary", "arbitrary"), 48),
        name="nattn",
    )(qkv, qkv, qkv, qkv_ctx, qkv_ctx, bias_all)


def _ctx_attn_kernel(q_ref, k_ref, v_ref, o_ref):
    for p in range(NA_HEADS // 2):
        cs = slice(p * LANES, (p + 1) * LANES)
        q2 = _stack_head_pair(q_ref[0, :, cs])
        s = lax.dot_general(q2, k_ref[0, :, cs], _NT, preferred_element_type=F32)
        m = jnp.max(s, axis=-1, keepdims=True)
        e = jnp.exp2(s - m)
        l = jnp.sum(e, axis=-1, keepdims=True)
        o2 = jnp.dot(e.astype(BF16), v_ref[0, :, cs], preferred_element_type=F32) / l
        o_ref[0, :, cs] = _unstack_head_pair(o2).astype(BF16)


def _ctx_attn(qkv_ctx):
    bsz, n_ctx, _ = qkv_ctx.shape
    return pl.pallas_call(
        _ctx_attn_kernel,
        grid=(bsz,),
        in_specs=[pl.BlockSpec((1, n_ctx, NA_WIDTH), lambda b, c=c: (b, 0, c)) for c in range(3)],
        out_specs=pl.BlockSpec((1, n_ctx, NA_WIDTH), lambda b: (b, 0, 0)),
        out_shape=jax.ShapeDtypeStruct((bsz, n_ctx, NA_WIDTH), BF16),
        compiler_params=_params(("arbitrary",), 32),
        name="ctx_attn",
    )(qkv_ctx, qkv_ctx, qkv_ctx)


def _attn_bias_tables(rpb):
    depth = rpb.shape[0]
    cq = np.arange(GRID_W)
    kc = np.arange(GRID_W)
    col_start = np.clip(cq - NA_KW // 2, 0, GRID_W - NA_KW)
    valid = (kc[None, :] >= col_start[:, None]) & (kc[None, :] < col_start[:, None] + NA_KW)
    dcol = kc[None, :] - cq[:, None] + (NA_KW - 1)
    onehot = (valid[:, :, None] & (dcol[:, :, None] == np.arange(2 * NA_KW - 1)[None, None, :])).astype(np.float32)
    plane = jnp.einsum("lhrd,qkd->lhrqk", rpb, jnp.asarray(onehot), precision=lax.Precision.HIGHEST)
    plane = plane * LOG2E + jnp.asarray(np.where(valid, 0.0, MASK_VALUE).astype(np.float32))
    pairs = jnp.concatenate([plane[:, :, :-1], plane[:, :, 1:]], axis=-1)
    n_d = 2 * NA_KH - 2
    pairs = pairs.reshape(depth, NA_HEADS // 2, 2, n_d, GRID_W, 2 * GRID_W).transpose(0, 1, 3, 2, 4, 5)
    return pairs.reshape(depth, NA_HEADS // 2, n_d, 2 * GRID_W, 2 * GRID_W)


def _halo_specs(tm, n, width, col_block):
    per = tm // HALO
    last = n // HALO - 1
    return [
        pl.BlockSpec((1, HALO, width), lambda b, i: (b, jnp.maximum(i * per - 1, 0), col_block)),
        pl.BlockSpec((1, tm, width), lambda b, i: (b, i, col_block)),
        pl.BlockSpec((1, HALO, width), lambda b, i: (b, jnp.minimum((i + 1) * per, last), col_block)),
    ]


def _pool_fill(ext_ref, prev_ref, cur_ref, next_ref, tm):
    i = pl.program_id(1)
    nt = pl.num_programs(1)
    ext_ref[0:HALO] = jnp.where(i > 0, prev_ref[0], 0.0)
    ext_ref[HALO:HALO + tm] = cur_ref[0]
    ext_ref[HALO + tm:HALO + tm + HALO] = jnp.where(i < nt - 1, next_ref[0], 0.0)


def _pool_rows(ext_ref, wbd_ref, ps_ref, o_ref, row0, n_rows, *, tm, n_tok, rc=128):
    i = pl.program_id(1)
    rc = min(rc, n_rows)
    row = lax.broadcasted_iota(jnp.int32, (rc, LANES), 0)
    lane = lax.broadcasted_iota(jnp.int32, (rc, LANES), 1)
    first = lane < POOL_GROUP_DIM

    for ci in range(n_rows // rc):
        base = row0 + ci * rc
        t = i * tm + base + row
        halves = []
        for half in range(2):
            cs = slice(half * LANES, (half + 1) * LANES)
            w_a, w_b = POOL_WINDOWS[2 * half], POOL_WINDOWS[2 * half + 1]
            lo_a, lo_b = w_a // 2, w_b // 2
            hi_a, hi_b = w_a - lo_a - 1, w_b - lo_b - 1

            def ld(off, cs=cs):
                return ext_ref[pl.ds(base + (HALO + off), rc), cs]

            x0 = ld(0)
            s_a = x0
            for off in range(-lo_a, hi_a + 1):
                if off != 0:
                    s_a = s_a + ld(off)
            s_b = s_a
            for off in list(range(-lo_b, -lo_a)) + list(range(hi_a + 1, hi_b + 1)):
                s_b = s_b + ld(off)
            lo = jnp.where(first, lo_a, lo_b)
            hi = jnp.where(first, hi_a, hi_b)
            cnt = jnp.minimum(t + hi + 1, n_tok) - jnp.maximum(t - lo, 0)
            halves.append(jnp.where(first, s_a, s_b) / cnt.astype(F32) - x0)
        d = jnp.concatenate(halves, axis=1).astype(BF16)
        y = jnp.dot(d, wbd_ref[0], preferred_element_type=F32) * ps_ref[0]
        o_ref[pl.ds(base, rc), :] = y.astype(BF16)


def _glu(u):
    return u[:, :CONV_WIDTH] * jax.nn.sigmoid(u[:, CONV_WIDTH:])


def _conv_fill(z_ref, prev_ref, cur_ref, next_ref, tm):
    i = pl.program_id(1)
    nt = pl.num_programs(1)
    ext = tm + 2 * HALO
    z_ref[0, 0:HALO] = jnp.where(i > 0, prev_ref[0], 0.0)
    z_ref[0, HALO:HALO + tm] = cur_ref[0]
    z_ref[0, HALO + tm:ext] = jnp.where(i < nt - 1, next_ref[0], 0.0)
    shifted_rows = ext - SUBLANES
    copy_rows = 128
    for s in range(1, SUBLANES):
        for c0 in range(0, shifted_rows, copy_rows):
            size = min(copy_rows, shifted_rows - c0)
            z_ref[s, c0:c0 + size] = z_ref[0, c0 + s:c0 + s + size]


def _conv_rows(z_ref, wdw_ref, bdw_ref, lng_ref, lnb_ref, o_ref, row0, n_rows, *, rc=128):
    rc = min(rc, n_rows)
    first_tap = HALO - CONV_K // 2

    for ci in range(n_rows // rc):
        base = row0 + ci * rc
        acc = jnp.broadcast_to(bdw_ref[0], (rc, CONV_WIDTH))
        for j in range(CONV_K):
            off = first_tap + j
            a0 = base + (off // SUBLANES) * SUBLANES
            acc = acc + z_ref[off % SUBLANES, a0:a0 + rc, :] * wdw_ref[0, j:j + 1, :]
        mu = jnp.mean(acc, axis=-1, keepdims=True)
        cen = acc - mu
        var = jnp.mean(cen * cen, axis=-1, keepdims=True)
        y = cen * lax.rsqrt(var + EPS) * lng_ref[0] + lnb_ref[0]
        o_ref[pl.ds(base, rc), :] = (y * jax.nn.sigmoid(y)).astype(BF16)


def _channel_dft_mats(n_groups):
    c = np.arange(FOURIER_GROUP_DIM)
    ang = 2.0 * np.pi * ((c[:, None] * c[None, :]) % FOURIER_GROUP_DIM) / FOURIER_GROUP_DIM
    eye = np.eye(n_groups)
    return _bf16_const(np.kron(eye, np.cos(ang))), _bf16_const(np.kron(eye, np.sin(ang)))


def _fft_tables(n):
    assert n == FFT_N1 * FFT_N2
    k1 = np.arange(FFT_N1)
    n1 = np.arange(FFT_N1)
    n2 = np.arange(FFT_N2)
    prod = (k1[None, :, None] * (FFT_N2 * n1[None, None, :] + n2[:, None, None])) % n
    th = 2.0 * np.pi * prod / n
    cs, sn = np.cos(th), np.sin(th)
    m_big = np.concatenate([np.concatenate([cs, -sn], axis=2), np.concatenate([-sn, -cs], axis=2)], axis=1)
    k2 = np.arange(FFT_N2)
    psi = 2.0 * np.pi * ((k2[:, None] * n2[None, :]) % FFT_N2) / FFT_N2
    w_int = np.stack([np.cos(psi), np.sin(psi)], axis=2).reshape(FFT_N2, 2 * FFT_N2)
    return _bf16_const(m_big), _bf16_const(w_int)


def _fourier_kernel(x_ref, cc_ref, sc_ref, mbig_ref, wint_ref, o_ref, ab_ref, t_ref, *, n):
    xb = x_ref[0].astype(BF16)
    ab_ref[0] = jnp.dot(xb, cc_ref[...], preferred_element_type=F32)
    ab_ref[1] = jnp.dot(xb, sc_ref[...], preferred_element_type=F32)

    def stage1(n2, carry):
        za = ab_ref[0, pl.ds(n2, FFT_N1, stride=FFT_N2), :]
        zb = ab_ref[1, pl.ds(n2, FFT_N1, stride=FFT_N2), :]
        z = jnp.concatenate([za, zb], axis=0).astype(BF16)
        off = pl.multiple_of(n2 * (2 * FFT_N1), 2 * FFT_N1)
        t_ref[pl.ds(off, 2 * FFT_N1), :] = jnp.dot(mbig_ref[n2], z, preferred_element_type=F32)
        return carry

    lax.fori_loop(0, FFT_N2, stage1, 0, unroll=8)

    scale = 1.0 / np.sqrt(float(n * FOURIER_GROUP_DIM))

    def stage2(k1, carry):
        t = t_ref[pl.ds(k1, 2 * FFT_N2, stride=FFT_N1), :].astype(BF16)
        y = jnp.dot(wint_ref[...], t, preferred_element_type=F32) * scale
        o_ref[0, pl.ds(k1, FFT_N2, stride=FFT_N1), :] = y
        return carry

    lax.fori_loop(0, FFT_N1, stage2, 0, unroll=16)


def _fourier_mix(pfc):
    bsz, n, _ = pfc.shape
    cc, sc = _channel_dft_mats(LANES // FOURIER_GROUP_DIM)
    m_big, w_int = _fft_tables(n)
    col0 = POOL_WIDTH // LANES
    const2 = lambda b, h: (0, 0)
    return pl.pallas_call(
        functools.partial(_fourier_kernel, n=n),
        grid=(bsz, FOURIER_WIDTH // LANES),
        in_specs=[
            pl.BlockSpec((1, n, LANES), lambda b, h: (b, 0, col0 + h)),
            pl.BlockSpec((LANES, LANES), const2),
            pl.BlockSpec((LANES, LANES), const2),
            pl.BlockSpec(m_big.shape, lambda b, h: (0, 0, 0), pipeline_mode=pl.Buffered(1)),
            pl.BlockSpec(w_int.shape, const2),
        ],
        out_specs=pl.BlockSpec((1, n, LANES), lambda b, h: (b, 0, h)),
        out_shape=jax.ShapeDtypeStruct((bsz, n, FOURIER_WIDTH), F32),
        scratch_shapes=[pltpu.VMEM((2, n, LANES), F32), pltpu.VMEM((2 * n, LANES), F32)],
        compiler_params=_params(("arbitrary", "arbitrary"), 56),
        name="fourier",
    )(pfc, cc, sc, m_big, w_int)


def _fourier_small_kernel(x_ref, cc_ref, sc_ref, cn_ref, sn_ref, o_ref, *, n):
    xb = x_ref[0].astype(BF16)
    a = jnp.dot(xb, cc_ref[...], preferred_element_type=F32).astype(BF16)
    b = jnp.dot(xb, sc_ref[...], preferred_element_type=F32).astype(BF16)
    y = (jnp.dot(cn_ref[...], a, preferred_element_type=F32) - jnp.dot(sn_ref[...], b, preferred_element_type=F32))
    o_ref[0] = y * (1.0 / np.sqrt(float(n * FOURIER_GROUP_DIM)))


def _fourier_mix_small(pfc):
    bsz, n, _ = pfc.shape
    cc, sc = _channel_dft_mats(FOURIER_WIDTH // FOURIER_GROUP_DIM)
    k = np.arange(n)
    th = 2.0 * np.pi * ((k[:, None] * k[None, :]) % n) / n
    cn, sn = _bf16_const(np.cos(th)), _bf16_const(np.sin(th))
    const2 = lambda b: (0, 0)
    return pl.pallas_call(
        functools.partial(_fourier_small_kernel, n=n),
        grid=(bsz,),
        in_specs=[
            pl.BlockSpec((1, n, FOURIER_WIDTH), lambda b: (b, 0, POOL_WIDTH // FOURIER_WIDTH)),
            pl.BlockSpec((FOURIER_WIDTH, FOURIER_WIDTH), const2),
            pl.BlockSpec((FOURIER_WIDTH, FOURIER_WIDTH), const2),
            pl.BlockSpec((n, n), const2),
            pl.BlockSpec((n, n), const2),
        ],
        out_specs=pl.BlockSpec((1, n, FOURIER_WIDTH), lambda b: (b, 0, 0)),
        out_shape=jax.ShapeDtypeStruct((bsz, n, FOURIER_WIDTH), F32),
        compiler_params=_params(("arbitrary",), 32),
        name="fourier_small",
    )(pfc, cc, sc, cn, sn)


def _merge_kernel(attn_ref, four_ref, gate_ref, x_ref, pp_ref, pc_ref, pn_ref, cp_ref, cc_ref, cn_ref, gt_ref,
                  wbd_ref, ps_ref, wdw_ref, bdw_ref, lng_ref, lnb_ref,
                  wa_ref, wp_ref, wf_ref, wc_ref, wo_ref, o_ref, ext_ref, z_ref, pool_ref, conv_ref,
                  *, mod_row, rc, n_tok):
    d = D_MODEL
    tm = x_ref.shape[1]
    gt = _mod_row(gt_ref, mod_row)
    _pool_fill(ext_ref, pp_ref, pc_ref, pn_ref, tm)
    _conv_fill(z_ref, cp_ref, cc_ref, cn_ref, tm)

    def mixers(row0):
        _pool_rows(ext_ref, wbd_ref, ps_ref, pool_ref, row0, rc, tm=tm, n_tok=n_tok)
        _conv_rows(z_ref, wdw_ref, bdw_ref, lng_ref, lnb_ref, conv_ref, row0, rc)

    def branch_dots(rows):
        acts = (attn_ref[0, rows, :], pool_ref[rows, :], four_ref[0, rows, :].astype(BF16), conv_ref[rows, :])
        return [jnp.dot(a, w_ref[0], preferred_element_type=F32)
                for a, w_ref in zip(acts, (wa_ref, wp_ref, wf_ref, wc_ref))]

    def finish(rows, ys):
        merged = None
        for bi, y in enumerate(ys):
            term = gate_ref[0, rows, bi * d:(bi + 1) * d] * y.astype(BF16)
            merged = term if merged is None else merged + term
        out = jnp.dot(merged, wo_ref[0], preferred_element_type=F32)
        o_ref[0, rows, :] = x_ref[0, rows, :] + gt * out

    chunks = [slice(c * rc, (c + 1) * rc) for c in range(tm // rc)]
    mixers(0)
    pending = branch_dots(chunks[0])
    for ci, rows in enumerate(chunks):
        nxt = None
        if ci + 1 < len(chunks):
            mixers((ci + 1) * rc)
            nxt = branch_dots(chunks[ci + 1])
        finish(rows, pending)
        pending = nxt


def _merge(attn, four_o, gate, x, pfc, ada, l, p, *, mod_row):
    bsz, n, d = x.shape
    tm = min(n, 512)
    rc = min(tm, 256)
    act = lambda w: pl.BlockSpec((1, tm, w), lambda b, i: (b, i, 0))
    params = [p[k] for k in ("wbd", "pool_scale", "w_dw", "b_dw", "ln_g", "ln_b", "wa", "wp", "wf", "wc", "wo")]
    return pl.pallas_call(
        functools.partial(_merge_kernel, mod_row=mod_row, rc=rc, n_tok=n),
        grid=(bsz, n // tm),
        in_specs=[act(NA_WIDTH), act(FOURIER_WIDTH), act(GATE_WIDTH), act(d)]
        + _halo_specs(tm, n, POOL_WIDTH, 0) + _halo_specs(tm, n, CONV_WIDTH, MIX_WIDTH // CONV_WIDTH - 1)
        + [_ada_spec(ada, l, 2)] + [_layer_spec(w, l) for w in params],
        out_specs=act(d),
        out_shape=jax.ShapeDtypeStruct((bsz, n, d), F32),
        scratch_shapes=[
            pltpu.VMEM((tm + 2 * HALO, POOL_WIDTH), F32),
            pltpu.VMEM((SUBLANES, tm + 2 * HALO, CONV_WIDTH), F32),
            pltpu.VMEM((tm, POOL_WIDTH), BF16),
            pltpu.VMEM((tm, CONV_WIDTH), BF16),
        ],
        compiler_params=_params(("arbitrary", "arbitrary"), 56),
        name="merge",
    )(attn, four_o, gate, x, pfc, pfc, pfc, pfc, pfc, pfc, ada, *params)


def _mlp_kernel(x_ref, xn_ref, g_ref, sh_ref, sc_ref, gt_ref, w1_ref, w2_ref, gfin_ref, o_ref, ha_ref, hb_ref,
                *, final_norm, mod_row, tiles_per_batch, tf):
    step = pl.program_id(0)
    last = pl.num_programs(0) - 1
    norm = functools.partial(_norm_next_tile, g_ref=g_ref, sh_ref=sh_ref, sc_ref=sc_ref,
                             mod_row=mod_row, tiles_per_batch=tiles_per_batch)

    @pl.when(step == 0)
    def _():
        norm(ha_ref, x_ref, tile=0)

    def body(cur_ref, nxt_ref):
        def up(f):
            return jnp.dot(cur_ref[...], w1_ref[0, :, f * tf:(f + 1) * tf], preferred_element_type=F32)

        pending = up(0)
        norm(nxt_ref, xn_ref, tile=jnp.minimum(step + 1, last))
        acc = None
        for f in range(D_FF // tf):
            nxt = up(f + 1) if (f + 1) * tf < D_FF else None
            a = jnp.square(jnp.maximum(pending, 0.0)).astype(BF16)
            o = jnp.dot(a, w2_ref[0, f * tf:(f + 1) * tf, :], preferred_element_type=F32)
            acc = o if acc is None else acc + o
            pending = nxt
        row = mod_row if mod_row is not None else step // tiles_per_batch
        y = x_ref[0] + gt_ref[0, pl.ds(row, 1), :] * acc
        if final_norm:
            ms = jnp.mean(y * y, axis=-1, keepdims=True)
            y = y * lax.rsqrt(ms + EPS) * gfin_ref[...]
        o_ref[0] = y

    _ping_pong(step, body, ha_ref, hb_ref)


def _mlp(x, g_all, ada, l, w1_all, w2_all, g_final, *, final_norm, mod_row):
    bsz, n, d = x.shape
    rows = bsz * n
    tm = 512
    tf = 1024
    tiles_per_batch = n // tm if mod_row is None else 1
    n_tiles = rows // tm
    xf = x.reshape(1, rows, d)
    single = pl.Buffered(1)
    out = pl.pallas_call(
        functools.partial(_mlp_kernel, final_norm=final_norm, mod_row=mod_row,
                          tiles_per_batch=tiles_per_batch, tf=tf),
        grid=(n_tiles,),
        in_specs=[
            pl.BlockSpec((1, tm, d), lambda t: (0, t, 0)),
            pl.BlockSpec((1, tm, d), lambda t: (0, jnp.minimum(t + 1, n_tiles - 1), 0)),
            _layer_spec(g_all, l),
            _ada_spec(ada, l, 3), _ada_spec(ada, l, 4), _ada_spec(ada, l, 5),
            pl.BlockSpec((1,) + w1_all.shape[1:], lambda t: (l, 0, 0), pipeline_mode=single),
            pl.BlockSpec((1,) + w2_all.shape[1:], lambda t: (l, 0, 0), pipeline_mode=single),
            pl.BlockSpec((1, d), lambda t: (0, 0)),
        ],
        out_specs=pl.BlockSpec((1, tm, d), lambda t: (0, t, 0)),
        out_shape=jax.ShapeDtypeStruct((1, rows, d), F32),
        scratch_shapes=[pltpu.VMEM((tm, d), BF16), pltpu.VMEM((tm, d), BF16)],
        compiler_params=_params(("arbitrary",), 56),
        name="mlp",
    )(xf, xf, g_all, ada, ada, ada, w1_all, w2_all, g_final.reshape(1, d))
    return out.reshape(bsz, n, d)


def _rope_tables(n):
    t = jnp.arange(n)
    row = (t // GRID_W).astype(F32)
    col = (t % GRID_W).astype(F32)
    inv = ROPE_THETA ** (-jnp.arange(0, ROPE_AXIS_DIM, 2, dtype=F32) / ROPE_AXIS_DIM)
    ang_r = row[:, None] * inv
    ang_c = col[:, None] * inv
    cr, sr, cc, sc = jnp.cos(ang_r), jnp.sin(ang_r), jnp.cos(ang_c), jnp.sin(ang_c)
    cos64 = jnp.concatenate([cr, cr, cc, cc], axis=1)
    sin64 = jnp.concatenate([-sr, sr, -sc, sc], axis=1)
    return jnp.tile(cos64, (1, LANES // HEAD_DIM)), jnp.tile(sin64, (1, LANES // HEAD_DIM))


def _pool_block_diag(w_pool):
    eye = jnp.asarray(np.eye(POOL_GROUPS, dtype=np.float32))
    bd = w_pool[:, :, :, None, :] * eye[None, :, None, :, None]
    return bd.reshape(w_pool.shape[0], POOL_WIDTH, POOL_WIDTH)


def _mixer_tail(attn, pfc, gate, x, ada, l, p, *, small, mod_row):
    four_o = _fourier_mix_small(pfc) if small else _fourier_mix(pfc)
    return _merge(attn, four_o, gate, x, pfc, ada, l, p, mod_row=mod_row)


def kernel(x, c, ctx, c_ctx, w_mod, b_mod, g_mix, g_ff, w_in, rpb, w_pool, pool_scale, w_dw, b_dw, conv_ln_g,
           conv_ln_b, w_br_attn, w_br_pool, w_br_fourier, w_br_conv, w_out, w_ff1, w_ff2, g_final):
    bsz, n, d = x.shape
    n_ctx = ctx.shape[1]
    depth = w_in.shape[0]
    cos_t, sin_t = _rope_tables(n)

    ada_rows = SUBLANES
    cvecs = jnp.concatenate([c, c_ctx[None, :], jnp.zeros((ada_rows - bsz - 1, d), F32)], axis=0)
    ada = _ada_all(cvecs, w_mod, b_mod)

    vec = lambda a: a.reshape(depth, 1, a.shape[-1])
    p = dict(
        wbd=_pool_block_diag(w_pool).astype(BF16), pool_scale=vec(pool_scale),
        w_dw=w_dw, b_dw=vec(b_dw), ln_g=vec(conv_ln_g), ln_b=vec(conv_ln_b),
        wa=w_br_attn.astype(BF16), wp=w_br_pool.astype(BF16), wf=w_br_fourier.astype(BF16),
        wc=w_br_conv.astype(BF16), wo=w_out.astype(BF16),
    )
    w_in_bf = w_in.astype(BF16)
    w1_bf = w_ff1.astype(BF16)
    w2_bf = w_ff2.astype(BF16)
    g_mix_v, g_ff_v = vec(g_mix), vec(g_ff)
    bias_all = _attn_bias_tables(rpb)

    h_ctx = ctx
    for l in range(depth):
        last = l == depth - 1
        qkv_c, pfc_c, gate_c = _in_proj(h_ctx, g_mix_v, ada, l, cos_t, sin_t, w_in_bf, rope=False, mod_row=bsz)
        if not last:
            attn_c = _ctx_attn(qkv_c)
            ctx_mid = _mixer_tail(attn_c, pfc_c, gate_c, h_ctx, ada, l, p, small=True, mod_row=bsz)
            ctx_next = _mlp(ctx_mid, g_ff_v, ada, l, w1_bf, w2_bf, g_final, final_norm=False, mod_row=bsz)

        qkv, pfc, gate = _in_proj(x, g_mix_v, ada, l, cos_t, sin_t, w_in_bf, rope=True, mod_row=None)
        attn = _nattn(qkv, qkv_c, bias_all, l)
        x = _mixer_tail(attn, pfc, gate, x, ada, l, p, small=False, mod_row=None)
        x = _mlp(x, g_ff_v, ada, l, w1_bf, w2_bf, g_final, final_norm=last, mod_row=None)
        if not last:
            h_ctx = ctx_next
    return x
```

```python
import functools

import numpy as np
import jax
import jax.numpy as jnp
from jax import lax
from jax.experimental import pallas as pl
from jax.experimental.pallas import tpu as pltpu

F32 = jnp.float32
BF16 = jnp.bfloat16

D_MODEL = 1024
DEPTH = 4
GRID_W = 64
NA_HEADS = 8
HEAD_DIM = 64
NA_WIDTH = NA_HEADS * HEAD_DIM
NA_KH = 8
NA_KW = 16
ROPE_AXIS_DIM = HEAD_DIM // 2
ROPE_THETA = 10000.0
POOL_WIDTH = 256
POOL_GROUPS = 4
POOL_GROUP_DIM = 64
POOL_WINDOWS = (2, 4, 8, 16)
FOURIER_WIDTH = 256
FOURIER_GROUP_DIM = 64
CONV_WIDTH = 256
CONV_K = 31
N_BRANCH = 4
D_FF = 4 * D_MODEL
EPS = 1e-6
QKV_WIDTH = 3 * NA_WIDTH
PFC_WIDTH = POOL_WIDTH + FOURIER_WIDTH + 2 * CONV_WIDTH
GATE_WIDTH = N_BRANCH * D_MODEL
IN_WIDTH = QKV_WIDTH + PFC_WIDTH + GATE_WIDTH

LANES = 128
SUBLANES = 8
HALO = 16
MASK_VALUE = -1e30
LOG2E = 1.4426950408889634
Q_SCALE = HEAD_DIM ** -0.5 * LOG2E

IN_TILE = 512
N_IN_TILES = IN_WIDTH // IN_TILE
N_QKV_TILES = QKV_WIDTH // IN_TILE
N_PFC_TILES = PFC_WIDTH // IN_TILE
IN_DOT_TILES = 2
MIX_WIDTH = POOL_WIDTH + FOURIER_WIDTH + CONV_WIDTH

FFT_N1 = 128
FFT_N2 = 64


def _params(sem, vmem_mb):
    return pltpu.CompilerParams(dimension_semantics=sem, vmem_limit_bytes=vmem_mb * 1024 * 1024)


def _bf16_const(a):
    return jnp.asarray(a, F32).astype(BF16)


def _layer_spec(arr, l):
    nd = arr.ndim - 1
    return pl.BlockSpec((1,) + arr.shape[1:], lambda *_: (l,) + (0,) * nd)


def _ada_spec(ada, l, chunk):
    return pl.BlockSpec((1, ada.shape[1], D_MODEL), lambda *_: (l, 0, chunk))


def _mod_row(ada_ref, mod_row):
    row = pl.program_id(0) if mod_row is None else mod_row
    return ada_ref[0, pl.ds(row, 1), :]


def _ada_kernel(c_ref, w_ref, b_ref, o_ref):
    cv = c_ref[...]
    s = cv * jax.nn.sigmoid(cv)
    o_ref[0] = jnp.dot(s.astype(BF16), w_ref[0].astype(BF16), preferred_element_type=F32) + b_ref[0]


def _ada_all(cvecs, w_mod, b_mod):
    depth, d, width = w_mod.shape
    rows = cvecs.shape[0]
    tn = 1536
    return pl.pallas_call(
        _ada_kernel,
        grid=(depth, width // tn),
        in_specs=[
            pl.BlockSpec((rows, d), lambda l, j: (0, 0)),
            pl.BlockSpec((1, d, tn), lambda l, j: (l, 0, j)),
            pl.BlockSpec((1, 1, tn), lambda l, j: (l, 0, j)),
        ],
        out_specs=pl.BlockSpec((1, rows, tn), lambda l, j: (l, 0, j)),
        out_shape=jax.ShapeDtypeStruct((depth, rows, width), F32),
        compiler_params=_params(("arbitrary", "arbitrary"), 32),
        name="ada",
    )(cvecs, w_mod, b_mod.reshape(depth, 1, width))


def _norm_modulate(x, g, shift, scale):
    ms = jnp.mean(x * x, axis=-1, keepdims=True)
    y = x * lax.rsqrt(ms + EPS) * g
    return y * (1.0 + scale) + shift


def _norm_next_tile(dst_ref, x_ref, g_ref, sh_ref, sc_ref, tile, mod_row, tiles_per_batch):
    row = mod_row if mod_row is not None else tile // tiles_per_batch
    dst_ref[...] = _norm_modulate(x_ref[0], g_ref[0], sh_ref[0, pl.ds(row, 1), :],
                                  sc_ref[0, pl.ds(row, 1), :]).astype(BF16)


def _ping_pong(step, body, ref_a, ref_b):
    pl.when(step % 2 == 0)(lambda: body(ref_a, ref_b))
    pl.when(step % 2 == 1)(lambda: body(ref_b, ref_a))


def _in_proj_kernel(*refs, rope, mod_row, tiles_per_batch, n_tiles, host_mixers):
    if host_mixers:
        (x_ref, xn_ref, g_ref, sh_ref, sc_ref, cos_ref, sin_ref, w_ref,
         wbd_ref, ps_ref, wdw_ref, bdw_ref, lng_ref, lnb_ref,
         qkv_ref, pfc_ref, gate_ref, pool_ref, conv_ref,
         ha_ref, hb_ref, ring_ref, ext_ref, z_ref) = refs
    else:
        (x_ref, xn_ref, g_ref, sh_ref, sc_ref, cos_ref, sin_ref, w_ref,
         qkv_ref, pfc_ref, gate_ref, ha_ref, hb_ref) = refs
        ring_ref = None
    step = pl.program_id(0)
    tm = ha_ref.shape[0]
    norm = functools.partial(_norm_next_tile, g_ref=g_ref, sh_ref=sh_ref, sc_ref=sc_ref,
                             mod_row=mod_row, tiles_per_batch=tiles_per_batch)

    @pl.when(step == 0)
    def _():
        norm(ha_ref, x_ref, tile=0)
        if host_mixers:
            ring_ref[...] = jnp.zeros_like(ring_ref)

    def mixers_prev_tile():
        seq_tile = (step - 1) % tiles_per_batch
        is_first = seq_tile == 0
        is_last = seq_tile == tiles_per_batch - 1
        s_prev, s_cur, s_next = (step + 1) % 3, (step + 2) % 3, step % 3
        pw = POOL_WIDTH
        _pool_fill(ext_ref, ring_ref[s_prev, tm - HALO:tm, 0:pw], ring_ref[s_cur, :, 0:pw],
                   ring_ref[s_next, 0:HALO, 0:pw], is_first, is_last, tm)
        _conv_fill(z_ref, ring_ref[s_prev, tm - HALO:tm, pw:], ring_ref[s_cur, :, pw:],
                   ring_ref[s_next, 0:HALO, pw:], is_first, is_last, tm)
        _pool_rows(ext_ref, wbd_ref, ps_ref, pool_ref.at[0], 0, tm, tm=tm, n_tok=tiles_per_batch * tm,
                   seq_tile=seq_tile)
        _conv_rows(z_ref, wdw_ref, bdw_ref, lng_ref, lnb_ref, conv_ref.at[0], 0, tm)

    def body(cur_ref, nxt_ref, with_mixers=False):
        _in_proj_tile(cur_ref, lambda: norm(nxt_ref, xn_ref, tile=jnp.minimum(step + 1, n_tiles - 1)),
                      cos_ref, sin_ref, w_ref, qkv_ref, pfc_ref, gate_ref, rope=rope,
                      ring_slot=ring_ref.at[step % 3] if host_mixers else None,
                      after_mix=mixers_prev_tile if with_mixers else None)

    if not host_mixers:
        _ping_pong(step, body, ha_ref, hb_ref)
    else:
        mixed = functools.partial(body, with_mixers=True)
        busy = jnp.logical_and(step > 0, step < n_tiles)
        pl.when(step == 0)(lambda: body(ha_ref, hb_ref))
        pl.when(jnp.logical_and(busy, step % 2 == 0))(lambda: mixed(ha_ref, hb_ref))
        pl.when(jnp.logical_and(busy, step % 2 == 1))(lambda: mixed(hb_ref, ha_ref))
        pl.when(step == n_tiles)(mixers_prev_tile)


def _in_proj_tile(h_ref, prepare_next, cos_ref, sin_ref, w_ref, qkv_ref, pfc_ref, gate_ref, *, rope,
                  ring_slot=None, after_mix=None):
    def mm(t0, n_sub):
        return jnp.dot(h_ref[...], w_ref[0, :, t0 * IN_TILE:(t0 + n_sub) * IN_TILE], preferred_element_type=F32)

    def qk_epilogue(t, acc):
        col0 = t * IN_TILE
        mult = Q_SCALE if t == 0 else None
        if rope:
            cos = cos_ref[...]
            sin = sin_ref[...]
            if mult is not None:
                cos, sin = cos * mult, sin * mult
            lane = lax.broadcasted_iota(jnp.int32, cos.shape, 1)
            first = (lane % (ROPE_AXIS_DIM)) < (ROPE_AXIS_DIM // 2)
            for cc in range(IN_TILE // LANES):
                a = acc[:, cc * LANES:(cc + 1) * LANES]
                partner = jnp.where(first,
                                    pltpu.roll(a, LANES - ROPE_AXIS_DIM // 2, 1),
                                    pltpu.roll(a, ROPE_AXIS_DIM // 2, 1))
                qkv_ref[0, :, col0 + cc * LANES:col0 + (cc + 1) * LANES] = (a * cos + partner * sin).astype(BF16)
        else:
            qkv_ref[0, :, col0:col0 + IN_TILE] = (acc if mult is None else acc * mult).astype(BF16)

    def epilogue(t, acc):
        if t < 2:
            qk_epilogue(t, acc)
        elif t < N_QKV_TILES:
            qkv_ref[0, :, t * IN_TILE:(t + 1) * IN_TILE] = acc.astype(BF16)
        elif t == N_QKV_TILES:
            pfc_ref[0, :, 0:IN_TILE] = acc
            if ring_slot is not None:
                ring_slot[:, 0:POOL_WIDTH] = acc[:, 0:POOL_WIDTH]
        elif t == N_QKV_TILES + 1:
            z = _glu(acc)
            pfc_ref[0, :, IN_TILE:IN_TILE + CONV_WIDTH] = z
            if ring_slot is not None:
                ring_slot[:, POOL_WIDTH:] = z
        else:
            c0 = (t - N_QKV_TILES - N_PFC_TILES) * IN_TILE
            gate_ref[0, :, c0:c0 + IN_TILE] = jax.nn.sigmoid(acc).astype(BF16)

    starts = list(range(0, N_IN_TILES, IN_DOT_TILES))
    width = lambda t0: min(IN_DOT_TILES, N_IN_TILES - t0)
    pending = mm(starts[0], width(starts[0]))
    prepare_next()
    for si, t0 in enumerate(starts):
        nxt = mm(starts[si + 1], width(starts[si + 1])) if si + 1 < len(starts) else None
        for sub in range(width(t0)):
            epilogue(t0 + sub, pending[:, sub * IN_TILE:(sub + 1) * IN_TILE])
        if after_mix is not None and t0 <= N_QKV_TILES + 1 < t0 + width(t0):
            after_mix()
        pending = nxt


def _in_proj(x, g_all, ada, l, cos_t, sin_t, w_all, *, rope, mod_row, mixer_params=None):
    bsz, n, d = x.shape
    rows = bsz * n
    tm = 512
    host = mixer_params is not None
    tiles_per_batch = n // tm if mod_row is None else 1
    n_tiles = rows // tm
    rope_tiles = cos_t.shape[0] // tm
    xf = x.reshape(1, rows, d)
    cur = lambda t: jnp.minimum(t, n_tiles - 1)
    row_spec = lambda w: pl.BlockSpec((1, tm, w), lambda t: (0, cur(t), 0))
    prev_spec = lambda w: pl.BlockSpec((1, tm, w), lambda t: (0, jnp.maximum(t - 1, 0), 0))
    mix_in = [mixer_params[k] for k in ("wbd", "pool_scale", "w_dw", "b_dw", "ln_g", "ln_b")] if host else []
    out_specs = [row_spec(QKV_WIDTH), row_spec(MIX_WIDTH), row_spec(GATE_WIDTH)]
    out_shape = [jax.ShapeDtypeStruct((1, rows, QKV_WIDTH), BF16),
                 jax.ShapeDtypeStruct((1, rows, MIX_WIDTH), F32),
                 jax.ShapeDtypeStruct((1, rows, GATE_WIDTH), BF16)]
    scratch = [pltpu.VMEM((tm, d), BF16), pltpu.VMEM((tm, d), BF16)]
    if host:
        out_specs += [prev_spec(POOL_WIDTH), prev_spec(CONV_WIDTH)]
        out_shape += [jax.ShapeDtypeStruct((1, rows, POOL_WIDTH), BF16),
                      jax.ShapeDtypeStruct((1, rows, CONV_WIDTH), BF16)]
        scratch += [pltpu.VMEM((3, tm, POOL_WIDTH + CONV_WIDTH), F32),
                    pltpu.VMEM((tm + 2 * HALO, POOL_WIDTH), F32),
                    pltpu.VMEM((SUBLANES, tm + 2 * HALO, CONV_WIDTH), F32)]
    outs = pl.pallas_call(
        functools.partial(_in_proj_kernel, rope=rope, mod_row=mod_row, tiles_per_batch=tiles_per_batch,
                          n_tiles=n_tiles, host_mixers=host),
        grid=(n_tiles + 1 if host else n_tiles,),
        in_specs=[
            row_spec(d),
            pl.BlockSpec((1, tm, d), lambda t: (0, cur(t + 1), 0)),
            _layer_spec(g_all, l),
            _ada_spec(ada, l, 0),
            _ada_spec(ada, l, 1),
            pl.BlockSpec((tm, LANES), lambda t: (cur(t) % rope_tiles, 0)),
            pl.BlockSpec((tm, LANES), lambda t: (cur(t) % rope_tiles, 0)),
            pl.BlockSpec((1,) + w_all.shape[1:], lambda t: (l, 0, 0), pipeline_mode=pl.Buffered(1)),
        ] + [_layer_spec(w, l) for w in mix_in],
        out_specs=out_specs,
        out_shape=out_shape,
        scratch_shapes=scratch,
        compiler_params=_params(("arbitrary",), 60 if host else 56),
        name="in_proj",
    )(xf, xf, g_all, ada, ada, cos_t, sin_t, w_all, *mix_in)
    return [o.reshape(bsz, n, o.shape[-1]) for o in outs]


def _stack_head_pair(qp):
    lane = lax.broadcasted_iota(jnp.int32, qp.shape, 1)
    zero = jnp.zeros_like(qp)
    return jnp.concatenate([jnp.where(lane < HEAD_DIM, qp, zero), jnp.where(lane < HEAD_DIM, zero, qp)], axis=0)


def _unstack_head_pair(o2):
    m = o2.shape[0] // 2
    lane = lax.broadcasted_iota(jnp.int32, (m, LANES), 1)
    return jnp.where(lane < HEAD_DIM, o2[:m], o2[m:])


_NT = (((1,), (1,)), ((), ()))


def _nattn_kernel(q_ref, k_ref, v_ref, kc_ref, vc_ref, bias_ref, o_ref, *, rows_per_step, n_rows):
    i = pl.program_id(1)
    win = NA_KH * GRID_W
    rows_per_iter = rows_per_step
    lookahead = 2

    def scores(rl, p):
        r = i * rows_per_step + rl
        rs = jnp.clip(r - NA_KH // 2, 0, n_rows - NA_KH)
        var = rs - r + (NA_KH - 1)
        qoff = pl.multiple_of(rl * GRID_W, GRID_W)
        koff = pl.multiple_of(rs * GRID_W, GRID_W)
        cs = slice(p * LANES, (p + 1) * LANES)
        q2 = _stack_head_pair(q_ref[0, pl.ds(qoff, GRID_W), cs])
        kp = k_ref[0, pl.ds(koff, win), cs]
        bias = jnp.concatenate([bias_ref[0, p, var + 2 * j] for j in range(NA_KH // 2)], axis=1)
        s_w = lax.dot_general(q2, kp, _NT, preferred_element_type=F32) + bias
        s_c = lax.dot_general(q2, kc_ref[0, :, cs], _NT, preferred_element_type=F32)
        return s_w, s_c, qoff, koff, cs

    def finish(s_w, s_c, qoff, koff, cs):
        m = jnp.maximum(jnp.max(s_w, axis=-1, keepdims=True), jnp.max(s_c, axis=-1, keepdims=True))
        p_w = jnp.exp2(s_w - m)
        p_c = jnp.exp2(s_c - m)
        l = jnp.sum(p_w, axis=-1, keepdims=True) + jnp.sum(p_c, axis=-1, keepdims=True)
        o2 = (jnp.dot(p_w.astype(BF16), v_ref[0, pl.ds(koff, win), cs], preferred_element_type=F32)
              + jnp.dot(p_c.astype(BF16), vc_ref[0, :, cs], preferred_element_type=F32))
        o2 = o2 / l
        o_ref[0, pl.ds(qoff, GRID_W), cs] = _unstack_head_pair(o2).astype(BF16)

    def body(it, carry):
        units = [(it * rows_per_iter + u, p) for u in range(rows_per_iter) for p in range(NA_HEADS // 2)]
        pending = [scores(*u) for u in units[:lookahead]]
        for idx in range(len(units)):
            if idx + lookahead < len(units):
                pending.append(scores(*units[idx + lookahead]))
            finish(*pending.pop(0))
        return carry

    lax.fori_loop(0, rows_per_step // rows_per_iter, body, 0)


def _nattn(qkv, qkv_ctx, bias_all, l):
    bsz, n, _ = qkv.shape
    n_ctx = qkv_ctx.shape[1]
    n_rows = n // GRID_W
    rows_per_step = 8
    tq = rows_per_step * GRID_W
    single = pl.Buffered(1)
    return pl.pallas_call(
        functools.partial(_nattn_kernel, rows_per_step=rows_per_step, n_rows=n_rows),
        grid=(bsz, n_rows // rows_per_step),
        in_specs=[
            pl.BlockSpec((1, tq, NA_WIDTH), lambda b, i: (b, i, 0)),
            pl.BlockSpec((1, n, NA_WIDTH), lambda b, i: (b, 0, 1), pipeline_mode=single),
            pl.BlockSpec((1, n, NA_WIDTH), lambda b, i: (b, 0, 2), pipeline_mode=single),
            pl.BlockSpec((1, n_ctx, NA_WIDTH), lambda b, i: (b, 0, 1)),
            pl.BlockSpec((1, n_ctx, NA_WIDTH), lambda b, i: (b, 0, 2)),
            pl.BlockSpec((1,) + bias_all.shape[1:], lambda b, i: (l, 0, 0, 0, 0), pipeline_mode=single),
        ],
        out_specs=pl.BlockSpec((1, tq, NA_WIDTH), lambda b, i: (b, i, 0)),
        out_shape=jax.ShapeDtypeStruct((bsz, n, NA_WIDTH), BF16),
        compiler_params=_params(("arbitrary", "arbitrary"), 48),
        name="nattn",
    )(qkv, qkv, qkv, qkv_ctx, qkv_ctx, bias_all)


def _ctx_attn_kernel(q_ref, k_ref, v_ref, o_ref):
    for p in range(NA_HEADS // 2):
        cs = slice(p * LANES, (p + 1) * LANES)
        q2 = _stack_head_pair(q_ref[0, :, cs])
        s = lax.dot_general(q2, k_ref[0, :, cs], _NT, preferred_element_type=F32)
        m = jnp.max(s, axis=-1, keepdims=True)
        e = jnp.exp2(s - m)
        l = jnp.sum(e, axis=-1, keepdims=True)
        o2 = jnp.dot(e.astype(BF16), v_ref[0, :, cs], preferred_element_type=F32) / l
        o_ref[0, :, cs] = _unstack_head_pair(o2).astype(BF16)


def _ctx_attn(qkv_ctx):
    bsz, n_ctx, _ = qkv_ctx.shape
    return pl.pallas_call(
        _ctx_attn_kernel,
        grid=(bsz,),
        in_specs=[pl.BlockSpec((1, n_ctx, NA_WIDTH), lambda b, c=c: (b, 0, c)) for c in range(3)],
        out_specs=pl.BlockSpec((1, n_ctx, NA_WIDTH), lambda b: (b, 0, 0)),
        out_shape=jax.ShapeDtypeStruct((bsz, n_ctx, NA_WIDTH), BF16),
        compiler_params=_params(("arbitrary",), 32),
        name="ctx_attn",
    )(qkv_ctx, qkv_ctx, qkv_ctx)


def _attn_bias_tables(rpb):
    depth = rpb.shape[0]
    cq = np.arange(GRID_W)
    kc = np.arange(GRID_W)
    col_start = np.clip(cq - NA_KW // 2, 0, GRID_W - NA_KW)
    valid = (kc[None, :] >= col_start[:, None]) & (kc[None, :] < col_start[:, None] + NA_KW)
    dcol = kc[None, :] - cq[:, None] + (NA_KW - 1)
    onehot = (valid[:, :, None] & (dcol[:, :, None] == np.arange(2 * NA_KW - 1)[None, None, :])).astype(np.float32)
    plane = jnp.einsum("lhrd,qkd->lhrqk", rpb, jnp.asarray(onehot), precision=lax.Precision.HIGHEST)
    plane = plane * LOG2E + jnp.asarray(np.where(valid, 0.0, MASK_VALUE).astype(np.float32))
    pairs = jnp.concatenate([plane[:, :, :-1], plane[:, :, 1:]], axis=-1)
    n_d = 2 * NA_KH - 2
    pairs = pairs.reshape(depth, NA_HEADS // 2, 2, n_d, GRID_W, 2 * GRID_W).transpose(0, 1, 3, 2, 4, 5)
    return pairs.reshape(depth, NA_HEADS // 2, n_d, 2 * GRID_W, 2 * GRID_W)


def _halo_specs(tm, n, width, col_block):
    per = tm // HALO
    last = n // HALO - 1
    return [
        pl.BlockSpec((1, HALO, width), lambda b, i: (b, jnp.maximum(i * per - 1, 0), col_block)),
        pl.BlockSpec((1, tm, width), lambda b, i: (b, i, col_block)),
        pl.BlockSpec((1, HALO, width), lambda b, i: (b, jnp.minimum((i + 1) * per, last), col_block)),
    ]


def _pool_fill(ext_ref, prev, cur, nxt, is_first, is_last, tm):
    ext_ref[0:HALO] = jnp.where(is_first, 0.0, prev)
    ext_ref[HALO:HALO + tm] = cur
    ext_ref[HALO + tm:HALO + tm + HALO] = jnp.where(is_last, 0.0, nxt)


def _pool_rows(ext_ref, wbd_ref, ps_ref, o_ref, row0, n_rows, *, tm, n_tok, seq_tile, rc=128):
    i = seq_tile
    rc = min(rc, n_rows)
    row = lax.broadcasted_iota(jnp.int32, (rc, LANES), 0)
    lane = lax.broadcasted_iota(jnp.int32, (rc, LANES), 1)
    first = lane < POOL_GROUP_DIM

    for ci in range(n_rows // rc):
        base = row0 + ci * rc
        t = i * tm + base + row
        halves = []
        for half in range(2):
            cs = slice(half * LANES, (half + 1) * LANES)
            w_a, w_b = POOL_WINDOWS[2 * half], POOL_WINDOWS[2 * half + 1]
            lo_a, lo_b = w_a // 2, w_b // 2
            hi_a, hi_b = w_a - lo_a - 1, w_b - lo_b - 1

            def ld(off, cs=cs):
                return ext_ref[pl.ds(base + (HALO + off), rc), cs]

            x0 = ld(0)
            s_a = x0
            for off in range(-lo_a, hi_a + 1):
                if off != 0:
                    s_a = s_a + ld(off)
            s_b = s_a
            for off in list(range(-lo_b, -lo_a)) + list(range(hi_a + 1, hi_b + 1)):
                s_b = s_b + ld(off)
            lo = jnp.where(first, lo_a, lo_b)
            hi = jnp.where(first, hi_a, hi_b)
            cnt = jnp.minimum(t + hi + 1, n_tok) - jnp.maximum(t - lo, 0)
            halves.append(jnp.where(first, s_a, s_b) / cnt.astype(F32) - x0)
        d = jnp.concatenate(halves, axis=1).astype(BF16)
        y = jnp.dot(d, wbd_ref[0], preferred_element_type=F32) * ps_ref[0]
        o_ref[pl.ds(base, rc), :] = y.astype(BF16)


def _glu(u):
    return u[:, :CONV_WIDTH] * jax.nn.sigmoid(u[:, CONV_WIDTH:])


def _conv_fill(z_ref, prev, cur, nxt, is_first, is_last, tm):
    ext = tm + 2 * HALO
    z_ref[0, 0:HALO] = jnp.where(is_first, 0.0, prev)
    z_ref[0, HALO:HALO + tm] = cur
    z_ref[0, HALO + tm:ext] = jnp.where(is_last, 0.0, nxt)
    shifted_rows = ext - SUBLANES
    copy_rows = 128
    for s in range(1, SUBLANES):
        for c0 in range(0, shifted_rows, copy_rows):
            size = min(copy_rows, shifted_rows - c0)
            z_ref[s, c0:c0 + size] = z_ref[0, c0 + s:c0 + s + size]


def _conv_rows(z_ref, wdw_ref, bdw_ref, lng_ref, lnb_ref, o_ref, row0, n_rows, *, rc=128):
    rc = min(rc, n_rows)
    first_tap = HALO - CONV_K // 2

    for ci in range(n_rows // rc):
        base = row0 + ci * rc
        acc = jnp.broadcast_to(bdw_ref[0], (rc, CONV_WIDTH))
        for j in range(CONV_K):
            off = first_tap + j
            a0 = base + (off // SUBLANES) * SUBLANES
            acc = acc + z_ref[off % SUBLANES, a0:a0 + rc, :] * wdw_ref[0, j:j + 1, :]
        mu = jnp.mean(acc, axis=-1, keepdims=True)
        cen = acc - mu
        var = jnp.mean(cen * cen, axis=-1, keepdims=True)
        y = cen * lax.rsqrt(var + EPS) * lng_ref[0] + lnb_ref[0]
        o_ref[pl.ds(base, rc), :] = (y * jax.nn.sigmoid(y)).astype(BF16)


def _channel_dft_mats(n_groups):
    c = np.arange(FOURIER_GROUP_DIM)
    ang = 2.0 * np.pi * ((c[:, None] * c[None, :]) % FOURIER_GROUP_DIM) / FOURIER_GROUP_DIM
    eye = np.eye(n_groups)
    return _bf16_const(np.kron(eye, np.cos(ang))), _bf16_const(np.kron(eye, np.sin(ang)))


def _fft_tables(n):
    assert n == FFT_N1 * FFT_N2
    k1 = np.arange(FFT_N1)
    n1 = np.arange(FFT_N1)
    n2 = np.arange(FFT_N2)
    prod = (k1[None, :, None] * (FFT_N2 * n1[None, None, :] + n2[:, None, None])) % n
    th = 2.0 * np.pi * prod / n
    cs, sn = np.cos(th), np.sin(th)
    m_big = np.concatenate([np.concatenate([cs, -sn], axis=2), np.concatenate([-sn, -cs], axis=2)], axis=1)
    k2 = np.arange(FFT_N2)
    psi = 2.0 * np.pi * ((k2[:, None] * n2[None, :]) % FFT_N2) / FFT_N2
    w_int = np.stack([np.cos(psi), np.sin(psi)], axis=2).reshape(FFT_N2, 2 * FFT_N2)
    return _bf16_const(m_big), _bf16_const(w_int)


def _fourier_kernel(x_ref, cc_ref, sc_ref, mbig_ref, wint_ref, o_ref, ab_ref, t_ref, *, n):
    xb = x_ref[0].astype(BF16)
    ab_ref[0] = jnp.dot(xb, cc_ref[...], preferred_element_type=F32)
    ab_ref[1] = jnp.dot(xb, sc_ref[...], preferred_element_type=F32)

    def stage1(n2, carry):
        za = ab_ref[0, pl.ds(n2, FFT_N1, stride=FFT_N2), :]
        zb = ab_ref[1, pl.ds(n2, FFT_N1, stride=FFT_N2), :]
        z = jnp.concatenate([za, zb], axis=0).astype(BF16)
        off = pl.multiple_of(n2 * (2 * FFT_N1), 2 * FFT_N1)
        t_ref[pl.ds(off, 2 * FFT_N1), :] = jnp.dot(mbig_ref[n2], z, preferred_element_type=F32)
        return carry

    lax.fori_loop(0, FFT_N2, stage1, 0, unroll=8)

    scale = 1.0 / np.sqrt(float(n * FOURIER_GROUP_DIM))

    def stage2(k1, carry):
        t = t_ref[pl.ds(k1, 2 * FFT_N2, stride=FFT_N1), :].astype(BF16)
        y = jnp.dot(wint_ref[...], t, preferred_element_type=F32) * scale
        o_ref[0, pl.ds(k1, FFT_N2, stride=FFT_N1), :] = y
        return carry

    lax.fori_loop(0, FFT_N1, stage2, 0, unroll=16)


def _fourier_mix(pfc):
    bsz, n, _ = pfc.shape
    cc, sc = _channel_dft_mats(LANES // FOURIER_GROUP_DIM)
    m_big, w_int = _fft_tables(n)
    col0 = POOL_WIDTH // LANES
    const2 = lambda b, h: (0, 0)
    return pl.pallas_call(
        functools.partial(_fourier_kernel, n=n),
        grid=(bsz, FOURIER_WIDTH // LANES),
        in_specs=[
            pl.BlockSpec((1, n, LANES), lambda b, h: (b, 0, col0 + h)),
            pl.BlockSpec((LANES, LANES), const2),
            pl.BlockSpec((LANES, LANES), const2),
            pl.BlockSpec(m_big.shape, lambda b, h: (0, 0, 0), pipeline_mode=pl.Buffered(1)),
            pl.BlockSpec(w_int.shape, const2),
        ],
        out_specs=pl.BlockSpec((1, n, LANES), lambda b, h: (b, 0, h)),
        out_shape=jax.ShapeDtypeStruct((bsz, n, FOURIER_WIDTH), F32),
        scratch_shapes=[pltpu.VMEM((2, n, LANES), F32), pltpu.VMEM((2 * n, LANES), F32)],
        compiler_params=_params(("arbitrary", "arbitrary"), 56),
        name="fourier",
    )(pfc, cc, sc, m_big, w_int)


def _fourier_small_kernel(x_ref, cc_ref, sc_ref, cn_ref, sn_ref, o_ref, *, n):
    xb = x_ref[0].astype(BF16)
    a = jnp.dot(xb, cc_ref[...], preferred_element_type=F32).astype(BF16)
    b = jnp.dot(xb, sc_ref[...], preferred_element_type=F32).astype(BF16)
    y = (jnp.dot(cn_ref[...], a, preferred_element_type=F32) - jnp.dot(sn_ref[...], b, preferred_element_type=F32))
    o_ref[0] = y * (1.0 / np.sqrt(float(n * FOURIER_GROUP_DIM)))


def _fourier_mix_small(pfc):
    bsz, n, _ = pfc.shape
    cc, sc = _channel_dft_mats(FOURIER_WIDTH // FOURIER_GROUP_DIM)
    k = np.arange(n)
    th = 2.0 * np.pi * ((k[:, None] * k[None, :]) % n) / n
    cn, sn = _bf16_const(np.cos(th)), _bf16_const(np.sin(th))
    const2 = lambda b: (0, 0)
    return pl.pallas_call(
        functools.partial(_fourier_small_kernel, n=n),
        grid=(bsz,),
        in_specs=[
            pl.BlockSpec((1, n, FOURIER_WIDTH), lambda b: (b, 0, POOL_WIDTH // FOURIER_WIDTH)),
            pl.BlockSpec((FOURIER_WIDTH, FOURIER_WIDTH), const2),
            pl.BlockSpec((FOURIER_WIDTH, FOURIER_WIDTH), const2),
            pl.BlockSpec((n, n), const2),
            pl.BlockSpec((n, n), const2),
        ],
        out_specs=pl.BlockSpec((1, n, FOURIER_WIDTH), lambda b: (b, 0, 0)),
        out_shape=jax.ShapeDtypeStruct((bsz, n, FOURIER_WIDTH), F32),
        compiler_params=_params(("arbitrary",), 32),
        name="fourier_small",
    )(pfc, cc, sc, cn, sn)


def _merge_kernel(*refs, mod_row, rc, n_tok, mixers_inside):
    if mixers_inside:
        (attn_ref, four_ref, gate_ref, x_ref, pp_ref, pc_ref, pn_ref, cp_ref, cc_ref, cn_ref, gt_ref,
         wbd_ref, ps_ref, wdw_ref, bdw_ref, lng_ref, lnb_ref,
         wa_ref, wp_ref, wf_ref, wc_ref, wo_ref, o_ref, ext_ref, z_ref, pool_ref, conv_ref) = refs
    else:
        (attn_ref, four_ref, gate_ref, x_ref, pool_in_ref, conv_in_ref, gt_ref,
         wa_ref, wp_ref, wf_ref, wc_ref, wo_ref, o_ref) = refs
        pool_ref, conv_ref = pool_in_ref.at[0], conv_in_ref.at[0]
    d = D_MODEL
    tm = x_ref.shape[1]
    gt = _mod_row(gt_ref, mod_row)
    if mixers_inside:
        i = pl.program_id(1)
        is_first, is_last = i == 0, i == pl.num_programs(1) - 1
        _pool_fill(ext_ref, pp_ref[0], pc_ref[0], pn_ref[0], is_first, is_last, tm)
        _conv_fill(z_ref, cp_ref[0], cc_ref[0], cn_ref[0], is_first, is_last, tm)

    def mixers(row0):
        if mixers_inside:
            _pool_rows(ext_ref, wbd_ref, ps_ref, pool_ref, row0, rc, tm=tm, n_tok=n_tok, seq_tile=pl.program_id(1))
            _conv_rows(z_ref, wdw_ref, bdw_ref, lng_ref, lnb_ref, conv_ref, row0, rc)

    def branch_dots(rows):
        acts = (attn_ref[0, rows, :], pool_ref[rows, :], four_ref[0, rows, :].astype(BF16), conv_ref[rows, :])
        return [jnp.dot(a, w_ref[0], preferred_element_type=F32)
                for a, w_ref in zip(acts, (wa_ref, wp_ref, wf_ref, wc_ref))]

    def finish(rows, ys):
        merged = None
        for bi, y in enumerate(ys):
            term = gate_ref[0, rows, bi * d:(bi + 1) * d] * y.astype(BF16)
            merged = term if merged is None else merged + term
        out = jnp.dot(merged, wo_ref[0], preferred_element_type=F32)
        o_ref[0, rows, :] = x_ref[0, rows, :] + gt * out

    chunks = [slice(c * rc, (c + 1) * rc) for c in range(tm // rc)]
    mixers(0)
    pending = branch_dots(chunks[0])
    for ci, rows in enumerate(chunks):
        nxt = None
        if ci + 1 < len(chunks):
            mixers((ci + 1) * rc)
            nxt = branch_dots(chunks[ci + 1])
        finish(rows, pending)
        pending = nxt


def _merge(attn, four_o, gate, x, ada, l, p, *, mod_row, pfc=None, pool_o=None, conv_o=None):
    bsz, n, d = x.shape
    tm = min(n, 512)
    rc = min(tm, 256)
    inside = pfc is not None
    act = lambda w: pl.BlockSpec((1, tm, w), lambda b, i: (b, i, 0))
    weights = [p[k] for k in ("wa", "wp", "wf", "wc", "wo")]
    if inside:
        mix_params = [p[k] for k in ("wbd", "pool_scale", "w_dw", "b_dw", "ln_g", "ln_b")]
        mix_specs = (_halo_specs(tm, n, POOL_WIDTH, 0)
                     + _halo_specs(tm, n, CONV_WIDTH, MIX_WIDTH // CONV_WIDTH - 1))
        mix_args = [pfc] * 6
        scratch = [pltpu.VMEM((tm + 2 * HALO, POOL_WIDTH), F32),
                   pltpu.VMEM((SUBLANES, tm + 2 * HALO, CONV_WIDTH), F32),
                   pltpu.VMEM((tm, POOL_WIDTH), BF16),
                   pltpu.VMEM((tm, CONV_WIDTH), BF16)]
    else:
        mix_params, mix_specs, mix_args, scratch = [], [act(POOL_WIDTH), act(CONV_WIDTH)], [pool_o, conv_o], []
    return pl.pallas_call(
        functools.partial(_merge_kernel, mod_row=mod_row, rc=rc, n_tok=n, mixers_inside=inside),
        grid=(bsz, n // tm),
        in_specs=[act(NA_WIDTH), act(FOURIER_WIDTH), act(GATE_WIDTH), act(d)] + mix_specs
        + [_ada_spec(ada, l, 2)] + [_layer_spec(w, l) for w in mix_params + weights],
        out_specs=act(d),
        out_shape=jax.ShapeDtypeStruct((bsz, n, d), F32),
        scratch_shapes=scratch,
        compiler_params=_params(("arbitrary", "arbitrary"), 56),
        name="merge",
    )(attn, four_o, gate, x, *mix_args, ada, *mix_params, *weights)


def _mlp_kernel(x_ref, xn_ref, g_ref, sh_ref, sc_ref, gt_ref, w1_ref, w2_ref, gfin_ref, o_ref, ha_ref, hb_ref,
                *, final_norm, mod_row, tiles_per_batch, tf):
    step = pl.program_id(0)
    last = pl.num_programs(0) - 1
    norm = functools.partial(_norm_next_tile, g_ref=g_ref, sh_ref=sh_ref, sc_ref=sc_ref,
                             mod_row=mod_row, tiles_per_batch=tiles_per_batch)

    @pl.when(step == 0)
    def _():
        norm(ha_ref, x_ref, tile=0)

    def body(cur_ref, nxt_ref):
        def up(f):
            return jnp.dot(cur_ref[...], w1_ref[0, :, f * tf:(f + 1) * tf], preferred_element_type=F32)

        pending = up(0)
        norm(nxt_ref, xn_ref, tile=jnp.minimum(step + 1, last))
        acc = None
        for f in range(D_FF // tf):
            nxt = up(f + 1) if (f + 1) * tf < D_FF else None
            a = jnp.square(jnp.maximum(pending, 0.0)).astype(BF16)
            o = jnp.dot(a, w2_ref[0, f * tf:(f + 1) * tf, :], preferred_element_type=F32)
            acc = o if acc is None else acc + o
            pending = nxt
        row = mod_row if mod_row is not None else step // tiles_per_batch
        y = x_ref[0] + gt_ref[0, pl.ds(row, 1), :] * acc
        if final_norm:
            ms = jnp.mean(y * y, axis=-1, keepdims=True)
            y = y * lax.rsqrt(ms + EPS) * gfin_ref[...]
        o_ref[0] = y

    _ping_pong(step, body, ha_ref, hb_ref)


def _mlp(x, g_all, ada, l, w1_all, w2_all, g_final, *, final_norm, mod_row):
    bsz, n, d = x.shape
    rows = bsz * n
    tm = 512
    tf = 1024
    tiles_per_batch = n // tm if mod_row is None else 1
    n_tiles = rows // tm
    xf = x.reshape(1, rows, d)
    single = pl.Buffered(1)
    out = pl.pallas_call(
        functools.partial(_mlp_kernel, final_norm=final_norm, mod_row=mod_row,
                          tiles_per_batch=tiles_per_batch, tf=tf),
        grid=(n_tiles,),
        in_specs=[
            pl.BlockSpec((1, tm, d), lambda t: (0, t, 0)),
            pl.BlockSpec((1, tm, d), lambda t: (0, jnp.minimum(t + 1, n_tiles - 1), 0)),
            _layer_spec(g_all, l),
            _ada_spec(ada, l, 3), _ada_spec(ada, l, 4), _ada_spec(ada, l, 5),
            pl.BlockSpec((1,) + w1_all.shape[1:], lambda t: (l, 0, 0), pipeline_mode=single),
            pl.BlockSpec((1,) + w2_all.shape[1:], lambda t: (l, 0, 0), pipeline_mode=single),
            pl.BlockSpec((1, d), lambda t: (0, 0)),
        ],
        out_specs=pl.BlockSpec((1, tm, d), lambda t: (0, t, 0)),
        out_shape=jax.ShapeDtypeStruct((1, rows, d), F32),
        scratch_shapes=[pltpu.VMEM((tm, d), BF16), pltpu.VMEM((tm, d), BF16)],
        compiler_params=_params(("arbitrary",), 56),
        name="mlp",
    )(xf, xf, g_all, ada, ada, ada, w1_all, w2_all, g_final.reshape(1, d))
    return out.reshape(bsz, n, d)


def _rope_tables(n):
    t = jnp.arange(n)
    row = (t // GRID_W).astype(F32)
    col = (t % GRID_W).astype(F32)
    inv = ROPE_THETA ** (-jnp.arange(0, ROPE_AXIS_DIM, 2, dtype=F32) / ROPE_AXIS_DIM)
    ang_r = row[:, None] * inv
    ang_c = col[:, None] * inv
    cr, sr, cc, sc = jnp.cos(ang_r), jnp.sin(ang_r), jnp.cos(ang_c), jnp.sin(ang_c)
    cos64 = jnp.concatenate([cr, cr, cc, cc], axis=1)
    sin64 = jnp.concatenate([-sr, sr, -sc, sc], axis=1)
    return jnp.tile(cos64, (1, LANES // HEAD_DIM)), jnp.tile(sin64, (1, LANES // HEAD_DIM))


def _pool_block_diag(w_pool):
    eye = jnp.asarray(np.eye(POOL_GROUPS, dtype=np.float32))
    bd = w_pool[:, :, :, None, :] * eye[None, :, None, :, None]
    return bd.reshape(w_pool.shape[0], POOL_WIDTH, POOL_WIDTH)


def kernel(x, c, ctx, c_ctx, w_mod, b_mod, g_mix, g_ff, w_in, rpb, w_pool, pool_scale, w_dw, b_dw, conv_ln_g,
           conv_ln_b, w_br_attn, w_br_pool, w_br_fourier, w_br_conv, w_out, w_ff1, w_ff2, g_final):
    bsz, n, d = x.shape
    n_ctx = ctx.shape[1]
    depth = w_in.shape[0]
    cos_t, sin_t = _rope_tables(n)

    ada_rows = SUBLANES
    cvecs = jnp.concatenate([c, c_ctx[None, :], jnp.zeros((ada_rows - bsz - 1, d), F32)], axis=0)
    ada = _ada_all(cvecs, w_mod, b_mod)

    vec = lambda a: a.reshape(depth, 1, a.shape[-1])
    p = dict(
        wbd=_pool_block_diag(w_pool).astype(BF16), pool_scale=vec(pool_scale),
        w_dw=w_dw, b_dw=vec(b_dw), ln_g=vec(conv_ln_g), ln_b=vec(conv_ln_b),
        wa=w_br_attn.astype(BF16), wp=w_br_pool.astype(BF16), wf=w_br_fourier.astype(BF16),
        wc=w_br_conv.astype(BF16), wo=w_out.astype(BF16),
    )
    w_in_bf = w_in.astype(BF16)
    w1_bf = w_ff1.astype(BF16)
    w2_bf = w_ff2.astype(BF16)
    g_mix_v, g_ff_v = vec(g_mix), vec(g_ff)
    bias_all = _attn_bias_tables(rpb)

    h_ctx = ctx
    for l in range(depth):
        last = l == depth - 1
        qkv_c, pfc_c, gate_c = _in_proj(h_ctx, g_mix_v, ada, l, cos_t, sin_t, w_in_bf, rope=False, mod_row=bsz)
        if not last:
            attn_c = _ctx_attn(qkv_c)
            ctx_mid = _merge(attn_c, _fourier_mix_small(pfc_c), gate_c, h_ctx, ada, l, p, mod_row=bsz, pfc=pfc_c)
            ctx_next = _mlp(ctx_mid, g_ff_v, ada, l, w1_bf, w2_bf, g_final, final_norm=False, mod_row=bsz)

        qkv, pfc, gate, pool_o, conv_o = _in_proj(x, g_mix_v, ada, l, cos_t, sin_t, w_in_bf, rope=True,
                                                  mod_row=None, mixer_params=p)
        attn = _nattn(qkv, qkv_c, bias_all, l)
        x = _merge(attn, _fourier_mix(pfc), gate, x, ada, l, p, mod_row=None, pool_o=pool_o, conv_o=conv_o)
        x = _mlp(x, g_ff_v, ada, l, w1_bf, w2_bf, g_final, final_norm=last, mod_row=None)
        if not last:
            h_ctx = ctx_next
    return x
```

```python
import functools

import numpy as np
import jax
import jax.numpy as jnp
from jax import lax
from jax.experimental import pallas as pl
from jax.experimental.pallas import tpu as pltpu

F32 = jnp.float32
BF16 = jnp.bfloat16

D_MODEL = 1024
DEPTH = 4
GRID_W = 64
NA_HEADS = 8
HEAD_DIM = 64
NA_WIDTH = NA_HEADS * HEAD_DIM
NA_KH = 8
NA_KW = 16
ROPE_AXIS_DIM = HEAD_DIM // 2
ROPE_THETA = 10000.0
POOL_WIDTH = 256
POOL_GROUPS = 4
POOL_GROUP_DIM = 64
POOL_WINDOWS = (2, 4, 8, 16)
FOURIER_WIDTH = 256
FOURIER_GROUP_DIM = 64
CONV_WIDTH = 256
CONV_K = 31
N_BRANCH = 4
D_FF = 4 * D_MODEL
EPS = 1e-6
QKV_WIDTH = 3 * NA_WIDTH
PFC_WIDTH = POOL_WIDTH + FOURIER_WIDTH + 2 * CONV_WIDTH
GATE_WIDTH = N_BRANCH * D_MODEL
IN_WIDTH = QKV_WIDTH + PFC_WIDTH + GATE_WIDTH

LANES = 128
SUBLANES = 8
HALO = 16
MASK_VALUE = -1e30
LOG2E = 1.4426950408889634
Q_SCALE = HEAD_DIM ** -0.5 * LOG2E

IN_TILE = 512
N_IN_TILES = IN_WIDTH // IN_TILE
N_QKV_TILES = QKV_WIDTH // IN_TILE
N_PFC_TILES = PFC_WIDTH // IN_TILE
IN_DOT_TILES = 2
MIX_WIDTH = POOL_WIDTH + FOURIER_WIDTH + CONV_WIDTH

FFT_N1 = 128
FFT_N2 = 64


def _params(sem, vmem_mb):
    return pltpu.CompilerParams(dimension_semantics=sem, vmem_limit_bytes=vmem_mb * 1024 * 1024)


def _bf16_const(a):
    return jnp.asarray(a, F32).astype(BF16)


def _layer_spec(arr, l):
    nd = arr.ndim - 1
    return pl.BlockSpec((1,) + arr.shape[1:], lambda *_: (l,) + (0,) * nd)


def _ada_spec(ada, l, chunk):
    return pl.BlockSpec((1, ada.shape[1], D_MODEL), lambda *_: (l, 0, chunk))


def _sigmoid(x):
    return 0.5 * jnp.tanh(0.5 * x) + 0.5


def _mod_row(ada_ref, mod_row):
    row = pl.program_id(0) if mod_row is None else mod_row
    return ada_ref[0, pl.ds(row, 1), :]


def _ada_kernel(c_ref, w_ref, b_ref, o_ref):
    cv = c_ref[...]
    s = cv * _sigmoid(cv)
    o_ref[0] = jnp.dot(s.astype(BF16), w_ref[0].astype(BF16), preferred_element_type=F32) + b_ref[0]


def _ada_all(cvecs, w_mod, b_mod):
    depth, d, width = w_mod.shape
    rows = cvecs.shape[0]
    tn = 1536
    return pl.pallas_call(
        _ada_kernel,
        grid=(depth, width // tn),
        in_specs=[
            pl.BlockSpec((rows, d), lambda l, j: (0, 0)),
            pl.BlockSpec((1, d, tn), lambda l, j: (l, 0, j)),
            pl.BlockSpec((1, 1, tn), lambda l, j: (l, 0, j)),
        ],
        out_specs=pl.BlockSpec((1, rows, tn), lambda l, j: (l, 0, j)),
        out_shape=jax.ShapeDtypeStruct((depth, rows, width), F32),
        compiler_params=_params(("arbitrary", "arbitrary"), 32),
        name="ada",
    )(cvecs, w_mod, b_mod.reshape(depth, 1, width))


def _norm_modulate(x, g, shift, scale):
    ms = jnp.mean(x * x, axis=-1, keepdims=True)
    y = x * lax.rsqrt(ms + EPS) * g
    return y * (1.0 + scale) + shift


def _norm_next_tile(dst_ref, x_ref, g_ref, sh_ref, sc_ref, tile, mod_row, tiles_per_batch):
    row = mod_row if mod_row is not None else tile // tiles_per_batch
    dst_ref[...] = _norm_modulate(x_ref[0], g_ref[0], sh_ref[0, pl.ds(row, 1), :],
                                  sc_ref[0, pl.ds(row, 1), :]).astype(BF16)


def _ping_pong(step, body, ref_a, ref_b):
    pl.when(step % 2 == 0)(lambda: body(ref_a, ref_b))
    pl.when(step % 2 == 1)(lambda: body(ref_b, ref_a))


def _in_proj_kernel(x_ref, xn_ref, g_ref, sh_ref, sc_ref, cos_ref, sin_ref, w_ref,
                    qkv_ref, pfc_ref, gate_ref, ha_ref, hb_ref, *, rope, mod_row, tiles_per_batch):
    step = pl.program_id(0)
    last = pl.num_programs(0) - 1
    norm = functools.partial(_norm_next_tile, g_ref=g_ref, sh_ref=sh_ref, sc_ref=sc_ref,
                             mod_row=mod_row, tiles_per_batch=tiles_per_batch)

    @pl.when(step == 0)
    def _():
        norm(ha_ref, x_ref, tile=0)

    def body(cur_ref, nxt_ref):
        _in_proj_tile(cur_ref, lambda: norm(nxt_ref, xn_ref, tile=jnp.minimum(step + 1, last)),
                      cos_ref, sin_ref, w_ref, qkv_ref, pfc_ref, gate_ref, rope=rope)

    _ping_pong(step, body, ha_ref, hb_ref)


def _in_proj_tile(h_ref, prepare_next, cos_ref, sin_ref, w_ref, qkv_ref, pfc_ref, gate_ref, *, rope):
    def mm(t0, n_sub):
        return jnp.dot(h_ref[...], w_ref[0, :, t0 * IN_TILE:(t0 + n_sub) * IN_TILE], preferred_element_type=F32)

    def qk_epilogue(t, acc):
        col0 = t * IN_TILE
        mult = Q_SCALE if t == 0 else None
        if rope:
            cos = cos_ref[...]
            sin = sin_ref[...]
            if mult is not None:
                cos, sin = cos * mult, sin * mult
            lane = lax.broadcasted_iota(jnp.int32, cos.shape, 1)
            first = (lane % (ROPE_AXIS_DIM)) < (ROPE_AXIS_DIM // 2)
            for cc in range(IN_TILE // LANES):
                a = acc[:, cc * LANES:(cc + 1) * LANES]
                partner = jnp.where(first,
                                    pltpu.roll(a, LANES - ROPE_AXIS_DIM // 2, 1),
                                    pltpu.roll(a, ROPE_AXIS_DIM // 2, 1))
                qkv_ref[0, :, col0 + cc * LANES:col0 + (cc + 1) * LANES] = (a * cos + partner * sin).astype(BF16)
        else:
            qkv_ref[0, :, col0:col0 + IN_TILE] = (acc if mult is None else acc * mult).astype(BF16)

    def epilogue(t, acc):
        if t < 2:
            qk_epilogue(t, acc)
        elif t < N_QKV_TILES:
            qkv_ref[0, :, t * IN_TILE:(t + 1) * IN_TILE] = acc.astype(BF16)
        elif t == N_QKV_TILES:
            pfc_ref[0, :, 0:IN_TILE] = acc
        elif t == N_QKV_TILES + 1:
            pfc_ref[0, :, IN_TILE:IN_TILE + CONV_WIDTH] = _glu(acc)
        else:
            c0 = (t - N_QKV_TILES - N_PFC_TILES) * IN_TILE
            gate_ref[0, :, c0:c0 + IN_TILE] = _sigmoid(acc).astype(BF16)

    starts = list(range(0, N_IN_TILES, IN_DOT_TILES))
    width = lambda t0: min(IN_DOT_TILES, N_IN_TILES - t0)
    pending = mm(starts[0], width(starts[0]))
    prepare_next()
    for si, t0 in enumerate(starts):
        nxt = mm(starts[si + 1], width(starts[si + 1])) if si + 1 < len(starts) else None
        for sub in range(width(t0)):
            epilogue(t0 + sub, pending[:, sub * IN_TILE:(sub + 1) * IN_TILE])
        pending = nxt


def _in_proj(x, g_all, ada, l, cos_t, sin_t, w_all, *, rope, mod_row):
    bsz, n, d = x.shape
    rows = bsz * n
    tm = 512
    tiles_per_batch = n // tm if mod_row is None else 1
    n_tiles = rows // tm
    rope_tiles = cos_t.shape[0] // tm
    xf = x.reshape(1, rows, d)
    row_spec = lambda w: pl.BlockSpec((1, tm, w), lambda t: (0, t, 0))
    outs = pl.pallas_call(
        functools.partial(_in_proj_kernel, rope=rope, mod_row=mod_row, tiles_per_batch=tiles_per_batch),
        grid=(n_tiles,),
        in_specs=[
            row_spec(d),
            pl.BlockSpec((1, tm, d), lambda t: (0, jnp.minimum(t + 1, n_tiles - 1), 0)),
            _layer_spec(g_all, l),
            _ada_spec(ada, l, 0),
            _ada_spec(ada, l, 1),
            pl.BlockSpec((tm, LANES), lambda t: (t % rope_tiles, 0)),
            pl.BlockSpec((tm, LANES), lambda t: (t % rope_tiles, 0)),
            pl.BlockSpec((1,) + w_all.shape[1:], lambda t: (l, 0, 0), pipeline_mode=pl.Buffered(1)),
        ],
        out_specs=[row_spec(QKV_WIDTH), row_spec(MIX_WIDTH), row_spec(GATE_WIDTH)],
        out_shape=[
            jax.ShapeDtypeStruct((1, rows, QKV_WIDTH), BF16),
            jax.ShapeDtypeStruct((1, rows, MIX_WIDTH), F32),
            jax.ShapeDtypeStruct((1, rows, GATE_WIDTH), BF16),
        ],
        scratch_shapes=[pltpu.VMEM((tm, d), BF16), pltpu.VMEM((tm, d), BF16)],
        compiler_params=_params(("arbitrary",), 56),
        name="in_proj",
    )(xf, xf, g_all, ada, ada, cos_t, sin_t, w_all)
    return [o.reshape(bsz, n, o.shape[-1]) for o in outs]


def _stack_head_pair(qp):
    lane = lax.broadcasted_iota(jnp.int32, qp.shape, 1)
    zero = jnp.zeros_like(qp)
    return jnp.concatenate([jnp.where(lane < HEAD_DIM, qp, zero), jnp.where(lane < HEAD_DIM, zero, qp)], axis=0)


def _unstack_head_pair(o2):
    m = o2.shape[0] // 2
    lane = lax.broadcasted_iota(jnp.int32, (m, LANES), 1)
    return jnp.where(lane < HEAD_DIM, o2[:m], o2[m:])


_NT = (((1,), (1,)), ((), ()))


def _nattn_kernel(q_ref, k_ref, v_ref, kc_ref, vc_ref, bias_ref, o_ref, *, rows_per_step, n_rows):
    i = pl.program_id(1)
    win = NA_KH * GRID_W
    rows_per_iter = rows_per_step
    lookahead = 2

    def scores(rl, p):
        r = i * rows_per_step + rl
        rs = jnp.clip(r - NA_KH // 2, 0, n_rows - NA_KH)
        var = rs - r + (NA_KH - 1)
        qoff = pl.multiple_of(rl * GRID_W, GRID_W)
        koff = pl.multiple_of(rs * GRID_W, GRID_W)
        cs = slice(p * LANES, (p + 1) * LANES)
        q2 = _stack_head_pair(q_ref[0, pl.ds(qoff, GRID_W), cs])
        kp = k_ref[0, pl.ds(koff, win), cs]
        bias = jnp.concatenate([bias_ref[0, p, var + 2 * j] for j in range(NA_KH // 2)], axis=1)
        s_w = lax.dot_general(q2, kp, _NT, preferred_element_type=F32) + bias
        s_c = lax.dot_general(q2, kc_ref[0, :, cs], _NT, preferred_element_type=F32)
        return s_w, s_c, qoff, koff, cs

    def finish(s_w, s_c, qoff, koff, cs):
        m = jnp.maximum(jnp.max(s_w, axis=-1, keepdims=True), jnp.max(s_c, axis=-1, keepdims=True))
        p_w = jnp.exp2(s_w - m)
        p_c = jnp.exp2(s_c - m)
        l = jnp.sum(p_w, axis=-1, keepdims=True) + jnp.sum(p_c, axis=-1, keepdims=True)
        o2 = (jnp.dot(p_w.astype(BF16), v_ref[0, pl.ds(koff, win), cs], preferred_element_type=F32)
              + jnp.dot(p_c.astype(BF16), vc_ref[0, :, cs], preferred_element_type=F32))
        o2 = o2 / l
        o_ref[0, pl.ds(qoff, GRID_W), cs] = _unstack_head_pair(o2).astype(BF16)

    def body(it, carry):
        units = [(it * rows_per_iter + u, p) for u in range(rows_per_iter) for p in range(NA_HEADS // 2)]
        pending = [scores(*u) for u in units[:lookahead]]
        for idx in range(len(units)):
            if idx + lookahead < len(units):
                pending.append(scores(*units[idx + lookahead]))
            finish(*pending.pop(0))
        return carry

    lax.fori_loop(0, rows_per_step // rows_per_iter, body, 0)


def _nattn(qkv, qkv_ctx, bias_all, l):
    bsz, n, _ = qkv.shape
    n_ctx = qkv_ctx.shape[1]
    n_rows = n // GRID_W
    rows_per_step = 8
    tq = rows_per_step * GRID_W
    single = pl.Buffered(1)
    return pl.pallas_call(
        functools.partial(_nattn_kernel, rows_per_step=rows_per_step, n_rows=n_rows),
        grid=(bsz, n_rows // rows_per_step),
        in_specs=[
            pl.BlockSpec((1, tq, NA_WIDTH), lambda b, i: (b, i, 0)),
            pl.BlockSpec((1, n, NA_WIDTH), lambda b, i: (b, 0, 1)),
            pl.BlockSpec((1, n, NA_WIDTH), lambda b, i: (b, 0, 2)),
            pl.BlockSpec((1, n_ctx, NA_WIDTH), lambda b, i: (b, 0, 1)),
            pl.BlockSpec((1, n_ctx, NA_WIDTH), lambda b, i: (b, 0, 2)),
            pl.BlockSpec((1,) + bias_all.shape[1:], lambda b, i: (l, 0, 0, 0, 0), pipeline_mode=single),
        ],
        out_specs=pl.BlockSpec((1, tq, NA_WIDTH), lambda b, i: (b, i, 0)),
        out_shape=jax.ShapeDtypeStruct((bsz, n, NA_WIDTH), BF16),
        compiler_params=_params(("arbitrary", "arbitrary"), 48),
        name="nattn",
    )(qkv, qkv, qkv, qkv_ctx, qkv_ctx, bias_all)


def _ctx_attn_kernel(q_ref, k_ref, v_ref, o_ref):
    for p in range(NA_HEADS // 2):
        cs = slice(p * LANES, (p + 1) * LANES)
        q2 = _stack_head_pair(q_ref[0, :, cs])
        s = lax.dot_general(q2, k_ref[0, :, cs], _NT, preferred_element_type=F32)
        m = jnp.max(s, axis=-1, keepdims=True)
        e = jnp.exp2(s - m)
        l = jnp.sum(e, axis=-1, keepdims=True)
        o2 = jnp.dot(e.astype(BF16), v_ref[0, :, cs], preferred_element_type=F32) / l
        o_ref[0, :, cs] = _unstack_head_pair(o2).astype(BF16)


def _ctx_attn(qkv_ctx):
    bsz, n_ctx, _ = qkv_ctx.shape
    return pl.pallas_call(
        _ctx_attn_kernel,
        grid=(bsz,),
        in_specs=[pl.BlockSpec((1, n_ctx, NA_WIDTH), lambda b, c=c: (b, 0, c)) for c in range(3)],
        out_specs=pl.BlockSpec((1, n_ctx, NA_WIDTH), lambda b: (b, 0, 0)),
        out_shape=jax.ShapeDtypeStruct((bsz, n_ctx, NA_WIDTH), BF16),
        compiler_params=_params(("arbitrary",), 32),
        name="ctx_attn",
    )(qkv_ctx, qkv_ctx, qkv_ctx)


def _attn_bias_tables(rpb):
    depth = rpb.shape[0]
    cq = np.arange(GRID_W)
    kc = np.arange(GRID_W)
    col_start = np.clip(cq - NA_KW // 2, 0, GRID_W - NA_KW)
    valid = (kc[None, :] >= col_start[:, None]) & (kc[None, :] < col_start[:, None] + NA_KW)
    dcol = kc[None, :] - cq[:, None] + (NA_KW - 1)
    onehot = (valid[:, :, None] & (dcol[:, :, None] == np.arange(2 * NA_KW - 1)[None, None, :])).astype(np.float32)
    plane = jnp.einsum("lhrd,qkd->lhrqk", rpb, jnp.asarray(onehot), precision=lax.Precision.HIGHEST)
    plane = plane * LOG2E + jnp.asarray(np.where(valid, 0.0, MASK_VALUE).astype(np.float32))
    pairs = jnp.concatenate([plane[:, :, :-1], plane[:, :, 1:]], axis=-1)
    n_d = 2 * NA_KH - 2
    pairs = pairs.reshape(depth, NA_HEADS // 2, 2, n_d, GRID_W, 2 * GRID_W).transpose(0, 1, 3, 2, 4, 5)
    return pairs.reshape(depth, NA_HEADS // 2, n_d, 2 * GRID_W, 2 * GRID_W)


def _halo_specs(tm, n, width, col_block):
    per = tm // HALO
    last = n // HALO - 1
    return [
        pl.BlockSpec((1, HALO, width), lambda b, i: (b, jnp.maximum(i * per - 1, 0), col_block)),
        pl.BlockSpec((1, tm, width), lambda b, i: (b, i, col_block)),
        pl.BlockSpec((1, HALO, width), lambda b, i: (b, jnp.minimum((i + 1) * per, last), col_block)),
    ]


def _pool_fill(ext_ref, prev, cur, nxt, is_first, is_last, tm):
    ext_ref[0:HALO] = jnp.where(is_first, 0.0, prev)
    ext_ref[HALO:HALO + tm] = cur
    ext_ref[HALO + tm:HALO + tm + HALO] = jnp.where(is_last, 0.0, nxt)


def _pool_rows(ext_ref, wbd_ref, ps_ref, o_ref, row0, n_rows, *, tm, n_tok, seq_tile, rc=128):
    i = seq_tile
    rc = min(rc, n_rows)
    row = lax.broadcasted_iota(jnp.int32, (rc, LANES), 0)
    lane = lax.broadcasted_iota(jnp.int32, (rc, LANES), 1)
    first = lane < POOL_GROUP_DIM

    for ci in range(n_rows // rc):
        base = row0 + ci * rc
        t = i * tm + base + row
        halves = []
        for half in range(2):
            cs = slice(half * LANES, (half + 1) * LANES)
            w_a, w_b = POOL_WINDOWS[2 * half], POOL_WINDOWS[2 * half + 1]
            lo_a, lo_b = w_a // 2, w_b // 2
            hi_a, hi_b = w_a - lo_a - 1, w_b - lo_b - 1

            def ld(off, cs=cs):
                return ext_ref[pl.ds(base + (HALO + off), rc), cs]

            x0 = ld(0)
            s_a = x0
            for off in range(-lo_a, hi_a + 1):
                if off != 0:
                    s_a = s_a + ld(off)
            s_b = s_a
            for off in list(range(-lo_b, -lo_a)) + list(range(hi_a + 1, hi_b + 1)):
                s_b = s_b + ld(off)
            lo = jnp.where(first, lo_a, lo_b)
            hi = jnp.where(first, hi_a, hi_b)
            cnt = jnp.minimum(t + hi + 1, n_tok) - jnp.maximum(t - lo, 0)
            halves.append(jnp.where(first, s_a, s_b) / cnt.astype(F32) - x0)
        d = jnp.concatenate(halves, axis=1).astype(BF16)
        y = jnp.dot(d, wbd_ref[0], preferred_element_type=F32) * ps_ref[0]
        o_ref[pl.ds(base, rc), :] = y.astype(BF16)


def _glu(u):
    return u[:, :CONV_WIDTH] * _sigmoid(u[:, CONV_WIDTH:])


def _conv_fill(z_ref, prev, cur, nxt, is_first, is_last, tm):
    ext = tm + 2 * HALO
    z_ref[0, 0:HALO] = jnp.where(is_first, 0.0, prev)
    z_ref[0, HALO:HALO + tm] = cur
    z_ref[0, HALO + tm:ext] = jnp.where(is_last, 0.0, nxt)
    shifted_rows = ext - SUBLANES
    copy_rows = 128
    for s in range(1, SUBLANES):
        for c0 in range(0, shifted_rows, copy_rows):
            size = min(copy_rows, shifted_rows - c0)
            z_ref[s, c0:c0 + size] = z_ref[0, c0 + s:c0 + s + size]


def _conv_rows(z_ref, wdw_ref, bdw_ref, lng_ref, lnb_ref, o_ref, row0, n_rows, *, rc=128):
    rc = min(rc, n_rows)
    first_tap = HALO - CONV_K // 2

    for ci in range(n_rows // rc):
        base = row0 + ci * rc
        acc = jnp.broadcast_to(bdw_ref[0], (rc, CONV_WIDTH))
        for j in range(CONV_K):
            off = first_tap + j
            a0 = base + (off // SUBLANES) * SUBLANES
            acc = acc + z_ref[off % SUBLANES, a0:a0 + rc, :] * wdw_ref[0, j:j + 1, :]
        mu = jnp.mean(acc, axis=-1, keepdims=True)
        cen = acc - mu
        var = jnp.mean(cen * cen, axis=-1, keepdims=True)
        y = cen * lax.rsqrt(var + EPS) * lng_ref[0] + lnb_ref[0]
        o_ref[pl.ds(base, rc), :] = (y * _sigmoid(y)).astype(BF16)


def _channel_dft_mats(n_groups):
    c = np.arange(FOURIER_GROUP_DIM)
    ang = 2.0 * np.pi * ((c[:, None] * c[None, :]) % FOURIER_GROUP_DIM) / FOURIER_GROUP_DIM
    eye = np.eye(n_groups)
    return _bf16_const(np.kron(eye, np.cos(ang))), _bf16_const(np.kron(eye, np.sin(ang)))


def _fft_tables(n):
    assert n == FFT_N1 * FFT_N2
    k1 = np.arange(FFT_N1)
    n1 = np.arange(FFT_N1)
    n2 = np.arange(FFT_N2)
    prod = (k1[None, :, None] * (FFT_N2 * n1[None, None, :] + n2[:, None, None])) % n
    th = 2.0 * np.pi * prod / n
    cs, sn = np.cos(th), np.sin(th)
    m_big = np.concatenate([np.concatenate([cs, -sn], axis=2), np.concatenate([-sn, -cs], axis=2)], axis=1)
    k2 = np.arange(FFT_N2)
    psi = 2.0 * np.pi * ((k2[:, None] * n2[None, :]) % FFT_N2) / FFT_N2
    w_int = np.stack([np.cos(psi), np.sin(psi)], axis=2).reshape(FFT_N2, 2 * FFT_N2)
    return _bf16_const(m_big), _bf16_const(w_int)


def _fourier_kernel(x_ref, cc_ref, sc_ref, mbig_ref, wint_ref, o_ref, ab_ref, t_ref, *, n):
    xb = x_ref[0].astype(BF16)
    ab_ref[0] = jnp.dot(xb, cc_ref[...], preferred_element_type=F32)
    ab_ref[1] = jnp.dot(xb, sc_ref[...], preferred_element_type=F32)

    def stage1(n2, carry):
        za = ab_ref[0, pl.ds(n2, FFT_N1, stride=FFT_N2), :]
        zb = ab_ref[1, pl.ds(n2, FFT_N1, stride=FFT_N2), :]
        z = jnp.concatenate([za, zb], axis=0).astype(BF16)
        off = pl.multiple_of(n2 * (2 * FFT_N1), 2 * FFT_N1)
        t_ref[pl.ds(off, 2 * FFT_N1), :] = jnp.dot(mbig_ref[n2], z, preferred_element_type=F32)
        return carry

    lax.fori_loop(0, FFT_N2, stage1, 0, unroll=8)

    scale = 1.0 / np.sqrt(float(n * FOURIER_GROUP_DIM))

    def stage2(k1, carry):
        t = t_ref[pl.ds(k1, 2 * FFT_N2, stride=FFT_N1), :].astype(BF16)
        y = jnp.dot(wint_ref[...], t, preferred_element_type=F32) * scale
        o_ref[0, pl.ds(k1, FFT_N2, stride=FFT_N1), :] = y
        return carry

    lax.fori_loop(0, FFT_N1, stage2, 0, unroll=16)


def _fourier_mix(pfc):
    bsz, n, _ = pfc.shape
    cc, sc = _channel_dft_mats(LANES // FOURIER_GROUP_DIM)
    m_big, w_int = _fft_tables(n)
    col0 = POOL_WIDTH // LANES
    const2 = lambda b, h: (0, 0)
    return pl.pallas_call(
        functools.partial(_fourier_kernel, n=n),
        grid=(bsz, FOURIER_WIDTH // LANES),
        in_specs=[
            pl.BlockSpec((1, n, LANES), lambda b, h: (b, 0, col0 + h)),
            pl.BlockSpec((LANES, LANES), const2),
            pl.BlockSpec((LANES, LANES), const2),
            pl.BlockSpec(m_big.shape, lambda b, h: (0, 0, 0), pipeline_mode=pl.Buffered(1)),
            pl.BlockSpec(w_int.shape, const2),
        ],
        out_specs=pl.BlockSpec((1, n, LANES), lambda b, h: (b, 0, h)),
        out_shape=jax.ShapeDtypeStruct((bsz, n, FOURIER_WIDTH), F32),
        scratch_shapes=[pltpu.VMEM((2, n, LANES), F32), pltpu.VMEM((2 * n, LANES), F32)],
        compiler_params=_params(("arbitrary", "arbitrary"), 56),
        name="fourier",
    )(pfc, cc, sc, m_big, w_int)


def _fourier_small_kernel(x_ref, cc_ref, sc_ref, cn_ref, sn_ref, o_ref, *, n):
    xb = x_ref[0].astype(BF16)
    a = jnp.dot(xb, cc_ref[...], preferred_element_type=F32).astype(BF16)
    b = jnp.dot(xb, sc_ref[...], preferred_element_type=F32).astype(BF16)
    y = (jnp.dot(cn_ref[...], a, preferred_element_type=F32) - jnp.dot(sn_ref[...], b, preferred_element_type=F32))
    o_ref[0] = y * (1.0 / np.sqrt(float(n * FOURIER_GROUP_DIM)))


def _fourier_mix_small(pfc):
    bsz, n, _ = pfc.shape
    cc, sc = _channel_dft_mats(FOURIER_WIDTH // FOURIER_GROUP_DIM)
    k = np.arange(n)
    th = 2.0 * np.pi * ((k[:, None] * k[None, :]) % n) / n
    cn, sn = _bf16_const(np.cos(th)), _bf16_const(np.sin(th))
    const2 = lambda b: (0, 0)
    return pl.pallas_call(
        functools.partial(_fourier_small_kernel, n=n),
        grid=(bsz,),
        in_specs=[
            pl.BlockSpec((1, n, FOURIER_WIDTH), lambda b: (b, 0, POOL_WIDTH // FOURIER_WIDTH)),
            pl.BlockSpec((FOURIER_WIDTH, FOURIER_WIDTH), const2),
            pl.BlockSpec((FOURIER_WIDTH, FOURIER_WIDTH), const2),
            pl.BlockSpec((n, n), const2),
            pl.BlockSpec((n, n), const2),
        ],
        out_specs=pl.BlockSpec((1, n, FOURIER_WIDTH), lambda b: (b, 0, 0)),
        out_shape=jax.ShapeDtypeStruct((bsz, n, FOURIER_WIDTH), F32),
        compiler_params=_params(("arbitrary",), 32),
        name="fourier_small",
    )(pfc, cc, sc, cn, sn)


def _merge_kernel(attn_ref, four_ref, gate_ref, x_ref, pp_ref, pc_ref, pn_ref, cp_ref, cc_ref, cn_ref, gt_ref,
                  wbd_ref, ps_ref, wdw_ref, bdw_ref, lng_ref, lnb_ref,
                  wa_ref, wp_ref, wf_ref, wc_ref, wo_ref, o_ref, ext_ref, z_ref, pool_ref, conv_ref,
                  *, mod_row, rc, n_tok):
    d = D_MODEL
    tm = x_ref.shape[1]
    gt = _mod_row(gt_ref, mod_row)
    i = pl.program_id(1)
    is_first, is_last = i == 0, i == pl.num_programs(1) - 1
    _pool_fill(ext_ref, pp_ref[0], pc_ref[0], pn_ref[0], is_first, is_last, tm)
    _conv_fill(z_ref, cp_ref[0], cc_ref[0], cn_ref[0], is_first, is_last, tm)

    def mixers(row0):
        _pool_rows(ext_ref, wbd_ref, ps_ref, pool_ref, row0, rc, tm=tm, n_tok=n_tok, seq_tile=i)
        _conv_rows(z_ref, wdw_ref, bdw_ref, lng_ref, lnb_ref, conv_ref, row0, rc)

    def branch_dots(rows):
        acts = (attn_ref[0, rows, :], pool_ref[rows, :], four_ref[0, rows, :].astype(BF16), conv_ref[rows, :])
        return [jnp.dot(a, w_ref[0], preferred_element_type=F32)
                for a, w_ref in zip(acts, (wa_ref, wp_ref, wf_ref, wc_ref))]

    def finish(rows, ys):
        merged = None
        for bi, y in enumerate(ys):
            term = gate_ref[0, rows, bi * d:(bi + 1) * d] * y.astype(BF16)
            merged = term if merged is None else merged + term
        out = jnp.dot(merged, wo_ref[0], preferred_element_type=F32)
        o_ref[0, rows, :] = x_ref[0, rows, :] + gt * out

    chunks = [slice(c * rc, (c + 1) * rc) for c in range(tm // rc)]
    mixers(0)
    pending = branch_dots(chunks[0])
    for ci, rows in enumerate(chunks):
        nxt = None
        if ci + 1 < len(chunks):
            mixers((ci + 1) * rc)
            nxt = branch_dots(chunks[ci + 1])
        finish(rows, pending)
        pending = nxt


def _merge(attn, four_o, gate, x, pfc, ada, l, p, *, mod_row):
    bsz, n, d = x.shape
    tm = min(n, 512)
    rc = min(tm, 256)
    act = lambda w: pl.BlockSpec((1, tm, w), lambda b, i: (b, i, 0))
    params = [p[k] for k in ("wbd", "pool_scale", "w_dw", "b_dw", "ln_g", "ln_b", "wa", "wp", "wf", "wc", "wo")]
    return pl.pallas_call(
        functools.partial(_merge_kernel, mod_row=mod_row, rc=rc, n_tok=n),
        grid=(bsz, n // tm),
        in_specs=[act(NA_WIDTH), act(FOURIER_WIDTH), act(GATE_WIDTH), act(d)]
        + _halo_specs(tm, n, POOL_WIDTH, 0) + _halo_specs(tm, n, CONV_WIDTH, MIX_WIDTH // CONV_WIDTH - 1)
        + [_ada_spec(ada, l, 2)] + [_layer_spec(w, l) for w in params],
        out_specs=act(d),
        out_shape=jax.ShapeDtypeStruct((bsz, n, d), F32),
        scratch_shapes=[
            pltpu.VMEM((tm + 2 * HALO, POOL_WIDTH), F32),
            pltpu.VMEM((SUBLANES, tm + 2 * HALO, CONV_WIDTH), F32),
            pltpu.VMEM((tm, POOL_WIDTH), BF16),
            pltpu.VMEM((tm, CONV_WIDTH), BF16),
        ],
        compiler_params=_params(("arbitrary", "arbitrary"), 56),
        name="merge",
    )(attn, four_o, gate, x, pfc, pfc, pfc, pfc, pfc, pfc, ada, *params)


def _mlp_kernel(x_ref, xn_ref, g_ref, sh_ref, sc_ref, gt_ref, w1_ref, w2_ref, gfin_ref, o_ref, ha_ref, hb_ref,
                *, final_norm, mod_row, tiles_per_batch, tf):
    step = pl.program_id(0)
    last = pl.num_programs(0) - 1
    norm = functools.partial(_norm_next_tile, g_ref=g_ref, sh_ref=sh_ref, sc_ref=sc_ref,
                             mod_row=mod_row, tiles_per_batch=tiles_per_batch)

    @pl.when(step == 0)
    def _():
        norm(ha_ref, x_ref, tile=0)

    def body(cur_ref, nxt_ref):
        def up(f):
            return jnp.dot(cur_ref[...], w1_ref[0, :, f * tf:(f + 1) * tf], preferred_element_type=F32)

        pending = up(0)
        norm(nxt_ref, xn_ref, tile=jnp.minimum(step + 1, last))
        acc = None
        for f in range(D_FF // tf):
            nxt = up(f + 1) if (f + 1) * tf < D_FF else None
            a = jnp.square(jnp.maximum(pending, 0.0)).astype(BF16)
            o = jnp.dot(a, w2_ref[0, f * tf:(f + 1) * tf, :], preferred_element_type=F32)
            acc = o if acc is None else acc + o
            pending = nxt
        row = mod_row if mod_row is not None else step // tiles_per_batch
        y = x_ref[0] + gt_ref[0, pl.ds(row, 1), :] * acc
        if final_norm:
            ms = jnp.mean(y * y, axis=-1, keepdims=True)
            y = y * lax.rsqrt(ms + EPS) * gfin_ref[...]
        o_ref[0] = y

    _ping_pong(step, body, ha_ref, hb_ref)


def _mlp(x, g_all, ada, l, w1_all, w2_all, g_final, *, final_norm, mod_row):
    bsz, n, d = x.shape
    rows = bsz * n
    tm = 512
    tf = 1024
    tiles_per_batch = n // tm if mod_row is None else 1
    n_tiles = rows // tm
    xf = x.reshape(1, rows, d)
    single = pl.Buffered(1)
    out = pl.pallas_call(
        functools.partial(_mlp_kernel, final_norm=final_norm, mod_row=mod_row,
                          tiles_per_batch=tiles_per_batch, tf=tf),
        grid=(n_tiles,),
        in_specs=[
            pl.BlockSpec((1, tm, d), lambda t: (0, t, 0)),
            pl.BlockSpec((1, tm, d), lambda t: (0, jnp.minimum(t + 1, n_tiles - 1), 0)),
            _layer_spec(g_all, l),
            _ada_spec(ada, l, 3), _ada_spec(ada, l, 4), _ada_spec(ada, l, 5),
            pl.BlockSpec((1,) + w1_all.shape[1:], lambda t: (l, 0, 0), pipeline_mode=single),
            pl.BlockSpec((1,) + w2_all.shape[1:], lambda t: (l, 0, 0), pipeline_mode=single),
            pl.BlockSpec((1, d), lambda t: (0, 0)),
        ],
        out_specs=pl.BlockSpec((1, tm, d), lambda t: (0, t, 0)),
        out_shape=jax.ShapeDtypeStruct((1, rows, d), F32),
        scratch_shapes=[pltpu.VMEM((tm, d), BF16), pltpu.VMEM((tm, d), BF16)],
        compiler_params=_params(("arbitrary",), 56),
        name="mlp",
    )(xf, xf, g_all, ada, ada, ada, w1_all, w2_all, g_final.reshape(1, d))
    return out.reshape(bsz, n, d)


def _rope_tables(n):
    t = jnp.arange(n)
    row = (t // GRID_W).astype(F32)
    col = (t % GRID_W).astype(F32)
    inv = ROPE_THETA ** (-jnp.arange(0, ROPE_AXIS_DIM, 2, dtype=F32) / ROPE_AXIS_DIM)
    ang_r = row[:, None] * inv
    ang_c = col[:, None] * inv
    cr, sr, cc, sc = jnp.cos(ang_r), jnp.sin(ang_r), jnp.cos(ang_c), jnp.sin(ang_c)
    cos64 = jnp.concatenate([cr, cr, cc, cc], axis=1)
    sin64 = jnp.concatenate([-sr, sr, -sc, sc], axis=1)
    return jnp.tile(cos64, (1, LANES // HEAD_DIM)), jnp.tile(sin64, (1, LANES // HEAD_DIM))


def _pool_block_diag(w_pool):
    eye = jnp.asarray(np.eye(POOL_GROUPS, dtype=np.float32))
    bd = w_pool[:, :, :, None, :] * eye[None, :, None, :, None]
    return bd.reshape(w_pool.shape[0], POOL_WIDTH, POOL_WIDTH)


def kernel(x, c, ctx, c_ctx, w_mod, b_mod, g_mix, g_ff, w_in, rpb, w_pool, pool_scale, w_dw, b_dw, conv_ln_g,
           conv_ln_b, w_br_attn, w_br_pool, w_br_fourier, w_br_conv, w_out, w_ff1, w_ff2, g_final):
    bsz, n, d = x.shape
    n_ctx = ctx.shape[1]
    depth = w_in.shape[0]
    cos_t, sin_t = _rope_tables(n)

    ada_rows = SUBLANES
    cvecs = jnp.concatenate([c, c_ctx[None, :], jnp.zeros((ada_rows - bsz - 1, d), F32)], axis=0)
    ada = _ada_all(cvecs, w_mod, b_mod)

    vec = lambda a: a.reshape(depth, 1, a.shape[-1])
    p = dict(
        wbd=_pool_block_diag(w_pool).astype(BF16), pool_scale=vec(pool_scale),
        w_dw=w_dw, b_dw=vec(b_dw), ln_g=vec(conv_ln_g), ln_b=vec(conv_ln_b),
        wa=w_br_attn.astype(BF16), wp=w_br_pool.astype(BF16), wf=w_br_fourier.astype(BF16),
        wc=w_br_conv.astype(BF16), wo=w_out.astype(BF16),
    )
    w_in_bf = w_in.astype(BF16)
    w1_bf = w_ff1.astype(BF16)
    w2_bf = w_ff2.astype(BF16)
    g_mix_v, g_ff_v = vec(g_mix), vec(g_ff)
    bias_all = _attn_bias_tables(rpb)

    h_ctx = ctx
    for l in range(depth):
        last = l == depth - 1
        qkv_c, pfc_c, gate_c = _in_proj(h_ctx, g_mix_v, ada, l, cos_t, sin_t, w_in_bf, rope=False, mod_row=bsz)
        if not last:
            attn_c = _ctx_attn(qkv_c)
            ctx_mid = _merge(attn_c, _fourier_mix_small(pfc_c), gate_c, h_ctx, pfc_c, ada, l, p, mod_row=bsz)
            ctx_next = _mlp(ctx_mid, g_ff_v, ada, l, w1_bf, w2_bf, g_final, final_norm=False, mod_row=bsz)

        qkv, pfc, gate = _in_proj(x, g_mix_v, ada, l, cos_t, sin_t, w_in_bf, rope=True, mod_row=None)
        attn = _nattn(qkv, qkv_c, bias_all, l)
        x = _merge(attn, _fourier_mix(pfc), gate, x, pfc, ada, l, p, mod_row=None)
        x = _mlp(x, g_ff_v, ada, l, w1_bf, w2_bf, g_final, final_norm=last, mod_row=None)
        if not last:
            h_ctx = ctx_next
    return x
```

```python
import functools

import numpy as np
import jax
import jax.numpy as jnp
from jax import lax
from jax.experimental import pallas as pl
from jax.experimental.pallas import tpu as pltpu

F32 = jnp.float32
BF16 = jnp.bfloat16

D_MODEL = 1024
DEPTH = 4
GRID_W = 64
NA_HEADS = 8
HEAD_DIM = 64
NA_WIDTH = NA_HEADS * HEAD_DIM
NA_KH = 8
NA_KW = 16
ROPE_AXIS_DIM = HEAD_DIM // 2
ROPE_THETA = 10000.0
POOL_WIDTH = 256
POOL_GROUPS = 4
POOL_GROUP_DIM = 64
POOL_WINDOWS = (2, 4, 8, 16)
FOURIER_WIDTH = 256
FOURIER_GROUP_DIM = 64
CONV_WIDTH = 256
CONV_K = 31
N_BRANCH = 4
D_FF = 4 * D_MODEL
EPS = 1e-6
QKV_WIDTH = 3 * NA_WIDTH
PFC_WIDTH = POOL_WIDTH + FOURIER_WIDTH + 2 * CONV_WIDTH
GATE_WIDTH = N_BRANCH * D_MODEL
IN_WIDTH = QKV_WIDTH + PFC_WIDTH + GATE_WIDTH

LANES = 128
SUBLANES = 8
HALO = 16
MASK_VALUE = -1e30
LOG2E = 1.4426950408889634
Q_SCALE = HEAD_DIM ** -0.5 * LOG2E

IN_TILE = 512
N_IN_TILES = IN_WIDTH // IN_TILE
N_QKV_TILES = QKV_WIDTH // IN_TILE
N_PFC_TILES = PFC_WIDTH // IN_TILE
IN_DOT_TILES = 2
MIX_WIDTH = POOL_WIDTH + FOURIER_WIDTH + CONV_WIDTH

FFT_N1 = 128
FFT_N2 = 64


V7X_VMEM_MIB = 64
VMEM_LARGE_MIB = V7X_VMEM_MIB - 8
VMEM_SMALL_MIB = V7X_VMEM_MIB // 2


def _params(sem, vmem_mib):
    return pltpu.CompilerParams(dimension_semantics=sem, vmem_limit_bytes=vmem_mib * 1024 * 1024)


def _bf16_const(a):
    return jnp.asarray(a, F32).astype(BF16)


def _layer_spec(arr, l):
    nd = arr.ndim - 1
    return pl.BlockSpec((1,) + arr.shape[1:], lambda *_: (l,) + (0,) * nd)


def _ada_spec(ada, l, chunk):
    return pl.BlockSpec((1, ada.shape[1], D_MODEL), lambda *_: (l, 0, chunk))


def _sigmoid(x):
    return 0.5 * jnp.tanh(0.5 * x) + 0.5


def _mod_row(ada_ref, mod_row):
    row = pl.program_id(0) if mod_row is None else mod_row
    return ada_ref[0, pl.ds(row, 1), :]


def _ada_kernel(c_ref, w_ref, b_ref, o_ref):
    cv = c_ref[...]
    s = cv * _sigmoid(cv)
    o_ref[0] = jnp.dot(s.astype(BF16), w_ref[0].astype(BF16), preferred_element_type=F32) + b_ref[0]


def _ada_all(cvecs, w_mod, b_mod):
    depth, d, width = w_mod.shape
    rows = cvecs.shape[0]
    tn = 1536
    return pl.pallas_call(
        _ada_kernel,
        grid=(depth, width // tn),
        in_specs=[
            pl.BlockSpec((rows, d), lambda l, j: (0, 0)),
            pl.BlockSpec((1, d, tn), lambda l, j: (l, 0, j)),
            pl.BlockSpec((1, 1, tn), lambda l, j: (l, 0, j)),
        ],
        out_specs=pl.BlockSpec((1, rows, tn), lambda l, j: (l, 0, j)),
        out_shape=jax.ShapeDtypeStruct((depth, rows, width), F32),
        compiler_params=_params(("arbitrary", "arbitrary"), VMEM_SMALL_MIB),
        name="ada",
    )(cvecs, w_mod, b_mod.reshape(depth, 1, width))


def _norm_modulate(x, g, shift, scale):
    ms = jnp.mean(x * x, axis=-1, keepdims=True)
    y = x * lax.rsqrt(ms + EPS) * g
    return y * (1.0 + scale) + shift


def _norm_next_tile(dst_ref, x_ref, g_ref, sh_ref, sc_ref, tile, mod_row, tiles_per_batch):
    row = mod_row if mod_row is not None else tile // tiles_per_batch
    dst_ref[...] = _norm_modulate(x_ref[0], g_ref[0], sh_ref[0, pl.ds(row, 1), :],
                                  sc_ref[0, pl.ds(row, 1), :]).astype(BF16)


def _ping_pong(step, body, ref_a, ref_b):
    pl.when(step % 2 == 0)(lambda: body(ref_a, ref_b))
    pl.when(step % 2 == 1)(lambda: body(ref_b, ref_a))


def _in_proj_kernel(x_ref, xn_ref, g_ref, sh_ref, sc_ref, cos_ref, sin_ref, w_ref,
                    qkv_ref, pfc_ref, gate_ref, ha_ref, hb_ref, *, rope, mod_row, tiles_per_batch):
    step = pl.program_id(0)
    last = pl.num_programs(0) - 1
    norm = functools.partial(_norm_next_tile, g_ref=g_ref, sh_ref=sh_ref, sc_ref=sc_ref,
                             mod_row=mod_row, tiles_per_batch=tiles_per_batch)

    @pl.when(step == 0)
    def _():
        norm(ha_ref, x_ref, tile=0)

    def body(cur_ref, nxt_ref):
        _in_proj_tile(cur_ref, lambda: norm(nxt_ref, xn_ref, tile=jnp.minimum(step + 1, last)),
                      cos_ref, sin_ref, w_ref, qkv_ref, pfc_ref, gate_ref, rope=rope)

    _ping_pong(step, body, ha_ref, hb_ref)


def _in_proj_tile(h_ref, prepare_next, cos_ref, sin_ref, w_ref, qkv_ref, pfc_ref, gate_ref, *, rope):
    def mm(t0, n_sub):
        return jnp.dot(h_ref[...], w_ref[0, :, t0 * IN_TILE:(t0 + n_sub) * IN_TILE], preferred_element_type=F32)

    def qk_epilogue(t, acc):
        col0 = t * IN_TILE
        mult = Q_SCALE if t == 0 else None
        if rope:
            cos = cos_ref[...]
            sin = sin_ref[...]
            if mult is not None:
                cos, sin = cos * mult, sin * mult
            lane = lax.broadcasted_iota(jnp.int32, cos.shape, 1)
            first = (lane % (ROPE_AXIS_DIM)) < (ROPE_AXIS_DIM // 2)
            for cc in range(IN_TILE // LANES):
                a = acc[:, cc * LANES:(cc + 1) * LANES]
                partner = jnp.where(first,
                                    pltpu.roll(a, LANES - ROPE_AXIS_DIM // 2, 1),
                                    pltpu.roll(a, ROPE_AXIS_DIM // 2, 1))
                qkv_ref[0, :, col0 + cc * LANES:col0 + (cc + 1) * LANES] = (a * cos + partner * sin).astype(BF16)
        else:
            qkv_ref[0, :, col0:col0 + IN_TILE] = (acc if mult is None else acc * mult).astype(BF16)

    def epilogue(t, acc):
        if t < 2:
            qk_epilogue(t, acc)
        elif t < N_QKV_TILES:
            qkv_ref[0, :, t * IN_TILE:(t + 1) * IN_TILE] = acc.astype(BF16)
        elif t == N_QKV_TILES:
            pfc_ref[0, :, 0:IN_TILE] = acc
        elif t == N_QKV_TILES + 1:
            pfc_ref[0, :, IN_TILE:IN_TILE + CONV_WIDTH] = _glu(acc)
        else:
            c0 = (t - N_QKV_TILES - N_PFC_TILES) * IN_TILE
            gate_ref[0, :, c0:c0 + IN_TILE] = _sigmoid(acc).astype(BF16)

    starts = list(range(0, N_IN_TILES, IN_DOT_TILES))
    width = lambda t0: min(IN_DOT_TILES, N_IN_TILES - t0)
    pending = mm(starts[0], width(starts[0]))
    prepare_next()
    for si, t0 in enumerate(starts):
        nxt = mm(starts[si + 1], width(starts[si + 1])) if si + 1 < len(starts) else None
        for sub in range(width(t0)):
            epilogue(t0 + sub, pending[:, sub * IN_TILE:(sub + 1) * IN_TILE])
        pending = nxt


def _in_proj(x, g_all, ada, l, cos_t, sin_t, w_all, *, rope, mod_row):
    bsz, n, d = x.shape
    rows = bsz * n
    tm = 512
    tiles_per_batch = n // tm if mod_row is None else 1
    n_tiles = rows // tm
    rope_tiles = cos_t.shape[0] // tm
    xf = x.reshape(1, rows, d)
    row_spec = lambda w: pl.BlockSpec((1, tm, w), lambda t: (0, t, 0))
    outs = pl.pallas_call(
        functools.partial(_in_proj_kernel, rope=rope, mod_row=mod_row, tiles_per_batch=tiles_per_batch),
        grid=(n_tiles,),
        in_specs=[
            row_spec(d),
            pl.BlockSpec((1, tm, d), lambda t: (0, jnp.minimum(t + 1, n_tiles - 1), 0)),
            _layer_spec(g_all, l),
            _ada_spec(ada, l, 0),
            _ada_spec(ada, l, 1),
            pl.BlockSpec((tm, LANES), lambda t: (t % rope_tiles, 0)),
            pl.BlockSpec((tm, LANES), lambda t: (t % rope_tiles, 0)),
            pl.BlockSpec((1,) + w_all.shape[1:], lambda t: (l, 0, 0), pipeline_mode=pl.Buffered(1)),
        ],
        out_specs=[row_spec(QKV_WIDTH), row_spec(MIX_WIDTH), row_spec(GATE_WIDTH)],
        out_shape=[
            jax.ShapeDtypeStruct((1, rows, QKV_WIDTH), BF16),
            jax.ShapeDtypeStruct((1, rows, MIX_WIDTH), F32),
            jax.ShapeDtypeStruct((1, rows, GATE_WIDTH), BF16),
        ],
        scratch_shapes=[pltpu.VMEM((tm, d), BF16), pltpu.VMEM((tm, d), BF16)],
        compiler_params=_params(("arbitrary",), VMEM_LARGE_MIB),
        name="in_proj",
    )(xf, xf, g_all, ada, ada, cos_t, sin_t, w_all)
    return [o.reshape(bsz, n, o.shape[-1]) for o in outs]


def _stack_head_pair(qp):
    lane = lax.broadcasted_iota(jnp.int32, qp.shape, 1)
    zero = jnp.zeros_like(qp)
    return jnp.concatenate([jnp.where(lane < HEAD_DIM, qp, zero), jnp.where(lane < HEAD_DIM, zero, qp)], axis=0)


def _unstack_head_pair(o2):
    m = o2.shape[0] // 2
    lane = lax.broadcasted_iota(jnp.int32, (m, LANES), 1)
    return jnp.where(lane < HEAD_DIM, o2[:m], o2[m:])


_NT = (((1,), (1,)), ((), ()))


def _nattn_kernel(q_ref, k_ref, v_ref, kc_ref, vc_ref, bias_ref, o_ref, *, rows_per_step, n_rows):
    i = pl.program_id(1)
    win = NA_KH * GRID_W
    rows_per_iter = rows_per_step
    lookahead = 2

    def scores(rl, p):
        r = i * rows_per_step + rl
        rs = jnp.clip(r - NA_KH // 2, 0, n_rows - NA_KH)
        var = rs - r + (NA_KH - 1)
        qoff = pl.multiple_of(rl * GRID_W, GRID_W)
        koff = pl.multiple_of(rs * GRID_W, GRID_W)
        cs = slice(p * LANES, (p + 1) * LANES)
        q2 = _stack_head_pair(q_ref[0, pl.ds(qoff, GRID_W), cs])
        kp = k_ref[0, pl.ds(koff, win), cs]
        bias = jnp.concatenate([bias_ref[0, p, var + 2 * j] for j in range(NA_KH // 2)], axis=1)
        s_w = lax.dot_general(q2, kp, _NT, preferred_element_type=F32) + bias
        s_c = lax.dot_general(q2, kc_ref[0, :, cs], _NT, preferred_element_type=F32)
        return s_w, s_c, qoff, koff, cs

    def finish(s_w, s_c, qoff, koff, cs):
        m = jnp.maximum(jnp.max(s_w, axis=-1, keepdims=True), jnp.max(s_c, axis=-1, keepdims=True))
        p_w = jnp.exp2(s_w - m)
        p_c = jnp.exp2(s_c - m)
        l = jnp.sum(p_w, axis=-1, keepdims=True) + jnp.sum(p_c, axis=-1, keepdims=True)
        o2 = (jnp.dot(p_w.astype(BF16), v_ref[0, pl.ds(koff, win), cs], preferred_element_type=F32)
              + jnp.dot(p_c.astype(BF16), vc_ref[0, :, cs], preferred_element_type=F32))
        o2 = o2 / l
        o_ref[0, pl.ds(qoff, GRID_W), cs] = _unstack_head_pair(o2).astype(BF16)

    def body(it, carry):
        units = [(it * rows_per_iter + u, p) for u in range(rows_per_iter) for p in range(NA_HEADS // 2)]
        pending = [scores(*u) for u in units[:lookahead]]
        for idx in range(len(units)):
            if idx + lookahead < len(units):
                pending.append(scores(*units[idx + lookahead]))
            finish(*pending.pop(0))
        return carry

    lax.fori_loop(0, rows_per_step // rows_per_iter, body, 0)


def _nattn(qkv, qkv_ctx, bias_all, l):
    bsz, n, _ = qkv.shape
    n_ctx = qkv_ctx.shape[1]
    n_rows = n // GRID_W
    rows_per_step = 16
    tq = rows_per_step * GRID_W
    single = pl.Buffered(1)
    return pl.pallas_call(
        functools.partial(_nattn_kernel, rows_per_step=rows_per_step, n_rows=n_rows),
        grid=(bsz, n_rows // rows_per_step),
        in_specs=[
            pl.BlockSpec((1, tq, NA_WIDTH), lambda b, i: (b, i, 0)),
            pl.BlockSpec((1, n, NA_WIDTH), lambda b, i: (b, 0, 1)),
            pl.BlockSpec((1, n, NA_WIDTH), lambda b, i: (b, 0, 2)),
            pl.BlockSpec((1, n_ctx, NA_WIDTH), lambda b, i: (b, 0, 1)),
            pl.BlockSpec((1, n_ctx, NA_WIDTH), lambda b, i: (b, 0, 2)),
            pl.BlockSpec((1,) + bias_all.shape[1:], lambda b, i: (l, 0, 0, 0, 0), pipeline_mode=single),
        ],
        out_specs=pl.BlockSpec((1, tq, NA_WIDTH), lambda b, i: (b, i, 0)),
        out_shape=jax.ShapeDtypeStruct((bsz, n, NA_WIDTH), BF16),
        compiler_params=_params(("arbitrary", "arbitrary"), VMEM_LARGE_MIB),
        name="nattn",
    )(qkv, qkv, qkv, qkv_ctx, qkv_ctx, bias_all)


def _ctx_attn_kernel(q_ref, k_ref, v_ref, o_ref):
    for p in range(NA_HEADS // 2):
        cs = slice(p * LANES, (p + 1) * LANES)
        q2 = _stack_head_pair(q_ref[0, :, cs])
        s = lax.dot_general(q2, k_ref[0, :, cs], _NT, preferred_element_type=F32)
        m = jnp.max(s, axis=-1, keepdims=True)
        e = jnp.exp2(s - m)
        l = jnp.sum(e, axis=-1, keepdims=True)
        o2 = jnp.dot(e.astype(BF16), v_ref[0, :, cs], preferred_element_type=F32) / l
        o_ref[0, :, cs] = _unstack_head_pair(o2).astype(BF16)


def _ctx_attn(qkv_ctx):
    bsz, n_ctx, _ = qkv_ctx.shape
    return pl.pallas_call(
        _ctx_attn_kernel,
        grid=(bsz,),
        in_specs=[pl.BlockSpec((1, n_ctx, NA_WIDTH), lambda b, c=c: (b, 0, c)) for c in range(3)],
        out_specs=pl.BlockSpec((1, n_ctx, NA_WIDTH), lambda b: (b, 0, 0)),
        out_shape=jax.ShapeDtypeStruct((bsz, n_ctx, NA_WIDTH), BF16),
        compiler_params=_params(("arbitrary",), VMEM_SMALL_MIB),
        name="ctx_attn",
    )(qkv_ctx, qkv_ctx, qkv_ctx)


def _attn_bias_tables(rpb):
    depth = rpb.shape[0]
    cq = np.arange(GRID_W)
    kc = np.arange(GRID_W)
    col_start = np.clip(cq - NA_KW // 2, 0, GRID_W - NA_KW)
    valid = (kc[None, :] >= col_start[:, None]) & (kc[None, :] < col_start[:, None] + NA_KW)
    dcol = kc[None, :] - cq[:, None] + (NA_KW - 1)
    onehot = (valid[:, :, None] & (dcol[:, :, None] == np.arange(2 * NA_KW - 1)[None, None, :])).astype(np.float32)
    plane = jnp.einsum("lhrd,qkd->lhrqk", rpb, jnp.asarray(onehot), precision=lax.Precision.HIGHEST)
    plane = plane * LOG2E + jnp.asarray(np.where(valid, 0.0, MASK_VALUE).astype(np.float32))
    pairs = jnp.concatenate([plane[:, :, :-1], plane[:, :, 1:]], axis=-1)
    n_d = 2 * NA_KH - 2
    pairs = pairs.reshape(depth, NA_HEADS // 2, 2, n_d, GRID_W, 2 * GRID_W).transpose(0, 1, 3, 2, 4, 5)
    return pairs.reshape(depth, NA_HEADS // 2, n_d, 2 * GRID_W, 2 * GRID_W)


def _halo_specs(tm, n, width, col_block):
    per = tm // HALO
    last = n // HALO - 1
    return [
        pl.BlockSpec((1, HALO, width), lambda b, i: (b, jnp.maximum(i * per - 1, 0), col_block)),
        pl.BlockSpec((1, tm, width), lambda b, i: (b, i, col_block)),
        pl.BlockSpec((1, HALO, width), lambda b, i: (b, jnp.minimum((i + 1) * per, last), col_block)),
    ]


def _pool_fill(ext_ref, prev, cur, nxt, is_first, is_last, tm):
    ext_ref[0:HALO] = jnp.where(is_first, 0.0, prev)
    ext_ref[HALO:HALO + tm] = cur
    ext_ref[HALO + tm:HALO + tm + HALO] = jnp.where(is_last, 0.0, nxt)


def _pool_rows(ext_ref, wbd_ref, ps_ref, o_ref, row0, n_rows, *, tm, n_tok, seq_tile, rc=128):
    i = seq_tile
    rc = min(rc, n_rows)
    row = lax.broadcasted_iota(jnp.int32, (rc, LANES), 0)
    lane = lax.broadcasted_iota(jnp.int32, (rc, LANES), 1)
    first = lane < POOL_GROUP_DIM

    for ci in range(n_rows // rc):
        base = row0 + ci * rc
        t = i * tm + base + row
        halves = []
        for half in range(2):
            cs = slice(half * LANES, (half + 1) * LANES)
            w_a, w_b = POOL_WINDOWS[2 * half], POOL_WINDOWS[2 * half + 1]
            lo_a, lo_b = w_a // 2, w_b // 2
            hi_a, hi_b = w_a - lo_a - 1, w_b - lo_b - 1

            def ld(off, cs=cs):
                return ext_ref[pl.ds(base + (HALO + off), rc), cs]

            x0 = ld(0)
            s_a = x0
            for off in range(-lo_a, hi_a + 1):
                if off != 0:
                    s_a = s_a + ld(off)
            s_b = s_a
            for off in list(range(-lo_b, -lo_a)) + list(range(hi_a + 1, hi_b + 1)):
                s_b = s_b + ld(off)
            lo = jnp.where(first, lo_a, lo_b)
            hi = jnp.where(first, hi_a, hi_b)
            cnt = jnp.minimum(t + hi + 1, n_tok) - jnp.maximum(t - lo, 0)
            halves.append(jnp.where(first, s_a, s_b) / cnt.astype(F32) - x0)
        d = jnp.concatenate(halves, axis=1).astype(BF16)
        y = jnp.dot(d, wbd_ref[0], preferred_element_type=F32) * ps_ref[0]
        o_ref[pl.ds(base, rc), :] = y.astype(BF16)


def _glu(u):
    return u[:, :CONV_WIDTH] * _sigmoid(u[:, CONV_WIDTH:])


def _conv_fill(z_ref, prev, cur, nxt, is_first, is_last, tm):
    ext = tm + 2 * HALO
    z_ref[0, 0:HALO] = jnp.where(is_first, 0.0, prev)
    z_ref[0, HALO:HALO + tm] = cur
    z_ref[0, HALO + tm:ext] = jnp.where(is_last, 0.0, nxt)
    shifted_rows = ext - SUBLANES
    copy_rows = 128
    for s in range(1, SUBLANES):
        for c0 in range(0, shifted_rows, copy_rows):
            size = min(copy_rows, shifted_rows - c0)
            z_ref[s, c0:c0 + size] = z_ref[0, c0 + s:c0 + s + size]


def _conv_rows(z_ref, wdw_ref, bdw_ref, lng_ref, lnb_ref, o_ref, row0, n_rows, *, rc=128):
    rc = min(rc, n_rows)
    first_tap = HALO - CONV_K // 2

    for ci in range(n_rows // rc):
        base = row0 + ci * rc
        acc = jnp.broadcast_to(bdw_ref[0], (rc, CONV_WIDTH))
        for j in range(CONV_K):
            off = first_tap + j
            a0 = base + (off // SUBLANES) * SUBLANES
            acc = acc + z_ref[off % SUBLANES, a0:a0 + rc, :] * wdw_ref[0, j:j + 1, :]
        mu = jnp.mean(acc, axis=-1, keepdims=True)
        cen = acc - mu
        var = jnp.mean(cen * cen, axis=-1, keepdims=True)
        y = cen * lax.rsqrt(var + EPS) * lng_ref[0] + lnb_ref[0]
        o_ref[pl.ds(base, rc), :] = (y * _sigmoid(y)).astype(BF16)


def _channel_dft_mats(n_groups):
    c = np.arange(FOURIER_GROUP_DIM)
    ang = 2.0 * np.pi * ((c[:, None] * c[None, :]) % FOURIER_GROUP_DIM) / FOURIER_GROUP_DIM
    eye = np.eye(n_groups)
    return _bf16_const(np.kron(eye, np.cos(ang))), _bf16_const(np.kron(eye, np.sin(ang)))


def _fft_tables(n):
    assert n == FFT_N1 * FFT_N2
    k1 = np.arange(FFT_N1)
    n1 = np.arange(FFT_N1)
    n2 = np.arange(FFT_N2)
    prod = (k1[None, :, None] * (FFT_N2 * n1[None, None, :] + n2[:, None, None])) % n
    th = 2.0 * np.pi * prod / n
    cs, sn = np.cos(th), np.sin(th)
    m_big = np.concatenate([np.concatenate([cs, -sn], axis=2), np.concatenate([-sn, -cs], axis=2)], axis=1)
    k2 = np.arange(FFT_N2)
    psi = 2.0 * np.pi * ((k2[:, None] * n2[None, :]) % FFT_N2) / FFT_N2
    w_int = np.stack([np.cos(psi), np.sin(psi)], axis=2).reshape(FFT_N2, 2 * FFT_N2)
    return _bf16_const(m_big), _bf16_const(w_int)


def _fourier_kernel(x_ref, cc_ref, sc_ref, mbig_ref, wint_ref, o_ref, ab_ref, t_ref, *, n):
    xb = x_ref[0].astype(BF16)
    ab_ref[0] = jnp.dot(xb, cc_ref[...], preferred_element_type=F32)
    ab_ref[1] = jnp.dot(xb, sc_ref[...], preferred_element_type=F32)

    def stage1(n2, carry):
        za = ab_ref[0, pl.ds(n2, FFT_N1, stride=FFT_N2), :]
        zb = ab_ref[1, pl.ds(n2, FFT_N1, stride=FFT_N2), :]
        z = jnp.concatenate([za, zb], axis=0).astype(BF16)
        off = pl.multiple_of(n2 * (2 * FFT_N1), 2 * FFT_N1)
        t_ref[pl.ds(off, 2 * FFT_N1), :] = jnp.dot(mbig_ref[n2], z, preferred_element_type=F32)
        return carry

    lax.fori_loop(0, FFT_N2, stage1, 0, unroll=16)

    scale = 1.0 / np.sqrt(float(n * FOURIER_GROUP_DIM))

    def stage2(k1, carry):
        t = t_ref[pl.ds(k1, 2 * FFT_N2, stride=FFT_N1), :].astype(BF16)
        y = jnp.dot(wint_ref[...], t, preferred_element_type=F32) * scale
        o_ref[0, pl.ds(k1, FFT_N2, stride=FFT_N1), :] = y
        return carry

    lax.fori_loop(0, FFT_N1, stage2, 0, unroll=16)


def _fourier_mix(pfc):
    bsz, n, _ = pfc.shape
    cc, sc = _channel_dft_mats(LANES // FOURIER_GROUP_DIM)
    m_big, w_int = _fft_tables(n)
    col0 = POOL_WIDTH // LANES
    const2 = lambda b, h: (0, 0)
    return pl.pallas_call(
        functools.partial(_fourier_kernel, n=n),
        grid=(bsz, FOURIER_WIDTH // LANES),
        in_specs=[
            pl.BlockSpec((1, n, LANES), lambda b, h: (b, 0, col0 + h)),
            pl.BlockSpec((LANES, LANES), const2),
            pl.BlockSpec((LANES, LANES), const2),
            pl.BlockSpec(m_big.shape, lambda b, h: (0, 0, 0), pipeline_mode=pl.Buffered(1)),
            pl.BlockSpec(w_int.shape, const2),
        ],
        out_specs=pl.BlockSpec((1, n, LANES), lambda b, h: (b, 0, h)),
        out_shape=jax.ShapeDtypeStruct((bsz, n, FOURIER_WIDTH), F32),
        scratch_shapes=[pltpu.VMEM((2, n, LANES), F32), pltpu.VMEM((2 * n, LANES), F32)],
        compiler_params=_params(("arbitrary", "arbitrary"), VMEM_LARGE_MIB),
        name="fourier",
    )(pfc, cc, sc, m_big, w_int)


def _fourier_small_kernel(x_ref, cc_ref, sc_ref, cn_ref, sn_ref, o_ref, *, n):
    xb = x_ref[0].astype(BF16)
    a = jnp.dot(xb, cc_ref[...], preferred_element_type=F32).astype(BF16)
    b = jnp.dot(xb, sc_ref[...], preferred_element_type=F32).astype(BF16)
    y = (jnp.dot(cn_ref[...], a, preferred_element_type=F32) - jnp.dot(sn_ref[...], b, preferred_element_type=F32))
    o_ref[0] = y * (1.0 / np.sqrt(float(n * FOURIER_GROUP_DIM)))


def _fourier_mix_small(pfc):
    bsz, n, _ = pfc.shape
    cc, sc = _channel_dft_mats(FOURIER_WIDTH // FOURIER_GROUP_DIM)
    k = np.arange(n)
    th = 2.0 * np.pi * ((k[:, None] * k[None, :]) % n) / n
    cn, sn = _bf16_const(np.cos(th)), _bf16_const(np.sin(th))
    const2 = lambda b: (0, 0)
    return pl.pallas_call(
        functools.partial(_fourier_small_kernel, n=n),
        grid=(bsz,),
        in_specs=[
            pl.BlockSpec((1, n, FOURIER_WIDTH), lambda b: (b, 0, POOL_WIDTH // FOURIER_WIDTH)),
            pl.BlockSpec((FOURIER_WIDTH, FOURIER_WIDTH), const2),
            pl.BlockSpec((FOURIER_WIDTH, FOURIER_WIDTH), const2),
            pl.BlockSpec((n, n), const2),
            pl.BlockSpec((n, n), const2),
        ],
        out_specs=pl.BlockSpec((1, n, FOURIER_WIDTH), lambda b: (b, 0, 0)),
        out_shape=jax.ShapeDtypeStruct((bsz, n, FOURIER_WIDTH), F32),
        compiler_params=_params(("arbitrary",), VMEM_SMALL_MIB),
        name="fourier_small",
    )(pfc, cc, sc, cn, sn)


def _merge_kernel(attn_ref, four_ref, gate_ref, x_ref, pp_ref, pc_ref, pn_ref, cp_ref, cc_ref, cn_ref, gt_ref,
                  wbd_ref, ps_ref, wdw_ref, bdw_ref, lng_ref, lnb_ref,
                  wa_ref, wp_ref, wf_ref, wc_ref, wo_ref, o_ref, ext_ref, z_ref, pool_ref, conv_ref,
                  *, mod_row, rc, n_tok):
    d = D_MODEL
    tm = x_ref.shape[1]
    gt = _mod_row(gt_ref, mod_row)
    i = pl.program_id(1)
    is_first, is_last = i == 0, i == pl.num_programs(1) - 1
    _pool_fill(ext_ref, pp_ref[0], pc_ref[0], pn_ref[0], is_first, is_last, tm)
    _conv_fill(z_ref, cp_ref[0], cc_ref[0], cn_ref[0], is_first, is_last, tm)

    def mixers(row0):
        _pool_rows(ext_ref, wbd_ref, ps_ref, pool_ref, row0, rc, tm=tm, n_tok=n_tok, seq_tile=i)
        _conv_rows(z_ref, wdw_ref, bdw_ref, lng_ref, lnb_ref, conv_ref, row0, rc)

    def branch_dots(rows):
        acts = (attn_ref[0, rows, :], pool_ref[rows, :], four_ref[0, rows, :].astype(BF16), conv_ref[rows, :])
        return [jnp.dot(a, w_ref[0], preferred_element_type=F32)
                for a, w_ref in zip(acts, (wa_ref, wp_ref, wf_ref, wc_ref))]

    def finish(rows, ys):
        merged = None
        for bi, y in enumerate(ys):
            term = gate_ref[0, rows, bi * d:(bi + 1) * d] * y.astype(BF16)
            merged = term if merged is None else merged + term
        out = jnp.dot(merged, wo_ref[0], preferred_element_type=F32)
        o_ref[0, rows, :] = x_ref[0, rows, :] + gt * out

    chunks = [slice(c * rc, (c + 1) * rc) for c in range(tm // rc)]
    mixers(0)
    pending = branch_dots(chunks[0])
    for ci, rows in enumerate(chunks):
        nxt = None
        if ci + 1 < len(chunks):
            mixers((ci + 1) * rc)
            nxt = branch_dots(chunks[ci + 1])
        finish(rows, pending)
        pending = nxt


def _merge(attn, four_o, gate, x, pfc, ada, l, p, *, mod_row):
    bsz, n, d = x.shape
    tm = min(n, 512)
    rc = min(tm, 256)
    act = lambda w: pl.BlockSpec((1, tm, w), lambda b, i: (b, i, 0))
    params = [p[k] for k in ("wbd", "pool_scale", "w_dw", "b_dw", "ln_g", "ln_b", "wa", "wp", "wf", "wc", "wo")]
    return pl.pallas_call(
        functools.partial(_merge_kernel, mod_row=mod_row, rc=rc, n_tok=n),
        grid=(bsz, n // tm),
        in_specs=[act(NA_WIDTH), act(FOURIER_WIDTH), act(GATE_WIDTH), act(d)]
        + _halo_specs(tm, n, POOL_WIDTH, 0) + _halo_specs(tm, n, CONV_WIDTH, MIX_WIDTH // CONV_WIDTH - 1)
        + [_ada_spec(ada, l, 2)] + [_layer_spec(w, l) for w in params],
        out_specs=act(d),
        out_shape=jax.ShapeDtypeStruct((bsz, n, d), F32),
        scratch_shapes=[
            pltpu.VMEM((tm + 2 * HALO, POOL_WIDTH), F32),
            pltpu.VMEM((SUBLANES, tm + 2 * HALO, CONV_WIDTH), F32),
            pltpu.VMEM((tm, POOL_WIDTH), BF16),
            pltpu.VMEM((tm, CONV_WIDTH), BF16),
        ],
        compiler_params=_params(("arbitrary", "arbitrary"), VMEM_LARGE_MIB),
        name="merge",
    )(attn, four_o, gate, x, pfc, pfc, pfc, pfc, pfc, pfc, ada, *params)


def _mlp_kernel(x_ref, xn_ref, g_ref, sh_ref, sc_ref, gt_ref, w1_ref, w2_ref, gfin_ref, o_ref, ha_ref, hb_ref,
                *, final_norm, mod_row, tiles_per_batch, tf):
    step = pl.program_id(0)
    last = pl.num_programs(0) - 1
    norm = functools.partial(_norm_next_tile, g_ref=g_ref, sh_ref=sh_ref, sc_ref=sc_ref,
                             mod_row=mod_row, tiles_per_batch=tiles_per_batch)

    @pl.when(step == 0)
    def _():
        norm(ha_ref, x_ref, tile=0)

    def body(cur_ref, nxt_ref):
        def up(f):
            return jnp.dot(cur_ref[...], w1_ref[0, :, f * tf:(f + 1) * tf], preferred_element_type=F32)

        pending = up(0)
        norm(nxt_ref, xn_ref, tile=jnp.minimum(step + 1, last))
        acc = None
        for f in range(D_FF // tf):
            nxt = up(f + 1) if (f + 1) * tf < D_FF else None
            a = jnp.square(jnp.maximum(pending, 0.0)).astype(BF16)
            o = jnp.dot(a, w2_ref[0, f * tf:(f + 1) * tf, :], preferred_element_type=F32)
            acc = o if acc is None else acc + o
            pending = nxt
        row = mod_row if mod_row is not None else step // tiles_per_batch
        y = x_ref[0] + gt_ref[0, pl.ds(row, 1), :] * acc
        if final_norm:
            ms = jnp.mean(y * y, axis=-1, keepdims=True)
            y = y * lax.rsqrt(ms + EPS) * gfin_ref[...]
        o_ref[0] = y

    _ping_pong(step, body, ha_ref, hb_ref)


def _mlp(x, g_all, ada, l, w1_all, w2_all, g_final, *, final_norm, mod_row):
    bsz, n, d = x.shape
    rows = bsz * n
    tm = 512
    tf = 1024
    tiles_per_batch = n // tm if mod_row is None else 1
    n_tiles = rows // tm
    xf = x.reshape(1, rows, d)
    single = pl.Buffered(1)
    out = pl.pallas_call(
        functools.partial(_mlp_kernel, final_norm=final_norm, mod_row=mod_row,
                          tiles_per_batch=tiles_per_batch, tf=tf),
        grid=(n_tiles,),
        in_specs=[
            pl.BlockSpec((1, tm, d), lambda t: (0, t, 0)),
            pl.BlockSpec((1, tm, d), lambda t: (0, jnp.minimum(t + 1, n_tiles - 1), 0)),
            _layer_spec(g_all, l),
            _ada_spec(ada, l, 3), _ada_spec(ada, l, 4), _ada_spec(ada, l, 5),
            pl.BlockSpec((1,) + w1_all.shape[1:], lambda t: (l, 0, 0), pipeline_mode=single),
            pl.BlockSpec((1,) + w2_all.shape[1:], lambda t: (l, 0, 0), pipeline_mode=single),
            pl.BlockSpec((1, d), lambda t: (0, 0)),
        ],
        out_specs=pl.BlockSpec((1, tm, d), lambda t: (0, t, 0)),
        out_shape=jax.ShapeDtypeStruct((1, rows, d), F32),
        scratch_shapes=[pltpu.VMEM((tm, d), BF16), pltpu.VMEM((tm, d), BF16)],
        compiler_params=_params(("arbitrary",), VMEM_LARGE_MIB),
        name="mlp",
    )(xf, xf, g_all, ada, ada, ada, w1_all, w2_all, g_final.reshape(1, d))
    return out.reshape(bsz, n, d)


def _rope_tables(n):
    t = jnp.arange(n)
    row = (t // GRID_W).astype(F32)
    col = (t % GRID_W).astype(F32)
    inv = ROPE_THETA ** (-jnp.arange(0, ROPE_AXIS_DIM, 2, dtype=F32) / ROPE_AXIS_DIM)
    ang_r = row[:, None] * inv
    ang_c = col[:, None] * inv
    cr, sr, cc, sc = jnp.cos(ang_r), jnp.sin(ang_r), jnp.cos(ang_c), jnp.sin(ang_c)
    cos64 = jnp.concatenate([cr, cr, cc, cc], axis=1)
    sin64 = jnp.concatenate([-sr, sr, -sc, sc], axis=1)
    return jnp.tile(cos64, (1, LANES // HEAD_DIM)), jnp.tile(sin64, (1, LANES // HEAD_DIM))


def _pool_block_diag(w_pool):
    eye = jnp.asarray(np.eye(POOL_GROUPS, dtype=np.float32))
    bd = w_pool[:, :, :, None, :] * eye[None, :, None, :, None]
    return bd.reshape(w_pool.shape[0], POOL_WIDTH, POOL_WIDTH)


def kernel(x, c, ctx, c_ctx, w_mod, b_mod, g_mix, g_ff, w_in, rpb, w_pool, pool_scale, w_dw, b_dw, conv_ln_g,
           conv_ln_b, w_br_attn, w_br_pool, w_br_fourier, w_br_conv, w_out, w_ff1, w_ff2, g_final):
    bsz, n, d = x.shape
    depth = w_in.shape[0]
    cos_t, sin_t = _rope_tables(n)

    ada_rows = SUBLANES
    cvecs = jnp.concatenate([c, c_ctx[None, :], jnp.zeros((ada_rows - bsz - 1, d), F32)], axis=0)
    ada = _ada_all(cvecs, w_mod, b_mod)

    vec = lambda a: a.reshape(depth, 1, a.shape[-1])
    p = dict(
        wbd=_pool_block_diag(w_pool).astype(BF16), pool_scale=vec(pool_scale),
        w_dw=w_dw, b_dw=vec(b_dw), ln_g=vec(conv_ln_g), ln_b=vec(conv_ln_b),
        wa=w_br_attn.astype(BF16), wp=w_br_pool.astype(BF16), wf=w_br_fourier.astype(BF16),
        wc=w_br_conv.astype(BF16), wo=w_out.astype(BF16),
    )
    w_in_bf = w_in.astype(BF16)
    w1_bf = w_ff1.astype(BF16)
    w2_bf = w_ff2.astype(BF16)
    g_mix_v, g_ff_v = vec(g_mix), vec(g_ff)
    bias_all = _attn_bias_tables(rpb)

    h_ctx = ctx
    for l in range(depth):
        last = l == depth - 1
        qkv_c, pfc_c, gate_c = _in_proj(h_ctx, g_mix_v, ada, l, cos_t, sin_t, w_in_bf, rope=False, mod_row=bsz)
        if not last:
            attn_c = _ctx_attn(qkv_c)
            ctx_mid = _merge(attn_c, _fourier_mix_small(pfc_c), gate_c, h_ctx, pfc_c, ada, l, p, mod_row=bsz)
            ctx_next = _mlp(ctx_mid, g_ff_v, ada, l, w1_bf, w2_bf, g_final, final_norm=False, mod_row=bsz)

        qkv, pfc, gate = _in_proj(x, g_mix_v, ada, l, cos_t, sin_t, w_in_bf, rope=True, mod_row=None)
        attn = _nattn(qkv, qkv_c, bias_all, l)
        x = _merge(attn, _fourier_mix(pfc), gate, x, pfc, ada, l, p, mod_row=None)
        x = _mlp(x, g_ff_v, ada, l, w1_bf, w2_bf, g_final, final_norm=last, mod_row=None)
        if not last:
            h_ctx = ctx_next
    return x
```

```python
import functools

import numpy as np
import jax
import jax.numpy as jnp
from jax import lax
from jax.experimental import pallas as pl
from jax.experimental.pallas import tpu as pltpu

F32 = jnp.float32
BF16 = jnp.bfloat16

D_MODEL = 1024
DEPTH = 4
GRID_W = 64
NA_HEADS = 8
HEAD_DIM = 64
NA_WIDTH = NA_HEADS * HEAD_DIM
NA_KH = 8
NA_KW = 16
ROPE_AXIS_DIM = HEAD_DIM // 2
ROPE_THETA = 10000.0
POOL_WIDTH = 256
POOL_GROUPS = 4
POOL_GROUP_DIM = 64
POOL_WINDOWS = (2, 4, 8, 16)
FOURIER_WIDTH = 256
FOURIER_GROUP_DIM = 64
CONV_WIDTH = 256
CONV_K = 31
N_BRANCH = 4
D_FF = 4 * D_MODEL
EPS = 1e-6
QKV_WIDTH = 3 * NA_WIDTH
PFC_WIDTH = POOL_WIDTH + FOURIER_WIDTH + 2 * CONV_WIDTH
GATE_WIDTH = N_BRANCH * D_MODEL
IN_WIDTH = QKV_WIDTH + PFC_WIDTH + GATE_WIDTH

LANES = 128
SUBLANES = 8
HALO = 16
MASK_VALUE = -1e30
LOG2E = 1.4426950408889634
Q_SCALE = HEAD_DIM ** -0.5 * LOG2E

IN_TILE = 512
N_IN_TILES = IN_WIDTH // IN_TILE
N_QKV_TILES = QKV_WIDTH // IN_TILE
N_PFC_TILES = PFC_WIDTH // IN_TILE
IN_DOT_TILES = 2
MIX_WIDTH = POOL_WIDTH + FOURIER_WIDTH + CONV_WIDTH

FFT_N1 = 128
FFT_N2 = 64


V7X_VMEM_MIB = 64
VMEM_LARGE_MIB = V7X_VMEM_MIB - 8
VMEM_SMALL_MIB = V7X_VMEM_MIB // 2


def _params(sem, vmem_mib):
    return pltpu.CompilerParams(dimension_semantics=sem, vmem_limit_bytes=vmem_mib * 1024 * 1024)


def _bf16_const(a):
    return jnp.asarray(a, F32).astype(BF16)


def _layer_spec(arr, l):
    nd = arr.ndim - 1
    return pl.BlockSpec((1,) + arr.shape[1:], lambda *_: (l,) + (0,) * nd)


def _ada_spec(ada, l, chunk):
    return pl.BlockSpec((1, ada.shape[1], D_MODEL), lambda *_: (l, 0, chunk))


def _sigmoid(x):
    return 0.5 * jnp.tanh(0.5 * x) + 0.5


def _mod_row(ada_ref, mod_row):
    row = pl.program_id(0) if mod_row is None else mod_row
    return ada_ref[0, pl.ds(row, 1), :]


def _ada_kernel(c_ref, w_ref, b_ref, o_ref):
    cv = c_ref[...]
    s = cv * _sigmoid(cv)
    o_ref[0] = jnp.dot(s.astype(BF16), w_ref[0].astype(BF16), preferred_element_type=F32) + b_ref[0]


def _ada_all(cvecs, w_mod, b_mod):
    depth, d, width = w_mod.shape
    rows = cvecs.shape[0]
    tn = 1536
    return pl.pallas_call(
        _ada_kernel,
        grid=(depth, width // tn),
        in_specs=[
            pl.BlockSpec((rows, d), lambda l, j: (0, 0)),
            pl.BlockSpec((1, d, tn), lambda l, j: (l, 0, j)),
            pl.BlockSpec((1, 1, tn), lambda l, j: (l, 0, j)),
        ],
        out_specs=pl.BlockSpec((1, rows, tn), lambda l, j: (l, 0, j)),
        out_shape=jax.ShapeDtypeStruct((depth, rows, width), F32),
        compiler_params=_params(("arbitrary", "arbitrary"), VMEM_SMALL_MIB),
        name="ada",
    )(cvecs, w_mod, b_mod.reshape(depth, 1, width))


def _norm_modulate(x, g, shift, scale):
    ms = jnp.mean(x * x, axis=-1, keepdims=True)
    y = x * lax.rsqrt(ms + EPS) * g
    return y * (1.0 + scale) + shift


def _norm_next_tile(dst_ref, x_ref, g_ref, sh_ref, sc_ref, tile, mod_row, tiles_per_batch):
    row = mod_row if mod_row is not None else tile // tiles_per_batch
    dst_ref[...] = _norm_modulate(x_ref[0], g_ref[0], sh_ref[0, pl.ds(row, 1), :],
                                  sc_ref[0, pl.ds(row, 1), :]).astype(BF16)


def _ping_pong(step, body, ref_a, ref_b):
    pl.when(step % 2 == 0)(lambda: body(ref_a, ref_b))
    pl.when(step % 2 == 1)(lambda: body(ref_b, ref_a))


def _in_proj_kernel(x_ref, xn_ref, g_ref, sh_ref, sc_ref, cos_ref, sin_ref, w_ref,
                    qkv_ref, pfc_ref, gate_ref, ha_ref, hb_ref, *, rope, mod_row, tiles_per_batch):
    step = pl.program_id(0)
    last = pl.num_programs(0) - 1
    norm = functools.partial(_norm_next_tile, g_ref=g_ref, sh_ref=sh_ref, sc_ref=sc_ref,
                             mod_row=mod_row, tiles_per_batch=tiles_per_batch)

    @pl.when(step == 0)
    def _():
        norm(ha_ref, x_ref, tile=0)

    def body(cur_ref, nxt_ref):
        _in_proj_tile(cur_ref, lambda: norm(nxt_ref, xn_ref, tile=jnp.minimum(step + 1, last)),
                      cos_ref, sin_ref, w_ref, qkv_ref, pfc_ref, gate_ref, rope=rope)

    _ping_pong(step, body, ha_ref, hb_ref)


def _in_proj_tile(h_ref, prepare_next, cos_ref, sin_ref, w_ref, qkv_ref, pfc_ref, gate_ref, *, rope):
    def mm(t0, n_sub):
        return jnp.dot(h_ref[...], w_ref[0, :, t0 * IN_TILE:(t0 + n_sub) * IN_TILE], preferred_element_type=F32)

    def qk_epilogue(t, acc):
        col0 = t * IN_TILE
        mult = Q_SCALE if t == 0 else None
        if rope:
            cos = cos_ref[...]
            sin = sin_ref[...]
            if mult is not None:
                cos, sin = cos * mult, sin * mult
            lane = lax.broadcasted_iota(jnp.int32, cos.shape, 1)
            first = (lane % (ROPE_AXIS_DIM)) < (ROPE_AXIS_DIM // 2)
            for cc in range(IN_TILE // LANES):
                a = acc[:, cc * LANES:(cc + 1) * LANES]
                partner = jnp.where(first,
                                    pltpu.roll(a, LANES - ROPE_AXIS_DIM // 2, 1),
                                    pltpu.roll(a, ROPE_AXIS_DIM // 2, 1))
                qkv_ref[0, :, col0 + cc * LANES:col0 + (cc + 1) * LANES] = (a * cos + partner * sin).astype(BF16)
        else:
            qkv_ref[0, :, col0:col0 + IN_TILE] = (acc if mult is None else acc * mult).astype(BF16)

    def epilogue(t, acc):
        if t < 2:
            qk_epilogue(t, acc)
        elif t < N_QKV_TILES:
            qkv_ref[0, :, t * IN_TILE:(t + 1) * IN_TILE] = acc.astype(BF16)
        elif t == N_QKV_TILES:
            pfc_ref[0, :, 0:IN_TILE] = acc
        elif t == N_QKV_TILES + 1:
            pfc_ref[0, :, IN_TILE:IN_TILE + CONV_WIDTH] = _glu(acc)
        else:
            c0 = (t - N_QKV_TILES - N_PFC_TILES) * IN_TILE
            gate_ref[0, :, c0:c0 + IN_TILE] = _sigmoid(acc).astype(BF16)

    starts = list(range(0, N_IN_TILES, IN_DOT_TILES))
    width = lambda t0: min(IN_DOT_TILES, N_IN_TILES - t0)
    pending = mm(starts[0], width(starts[0]))
    prepare_next()
    for si, t0 in enumerate(starts):
        nxt = mm(starts[si + 1], width(starts[si + 1])) if si + 1 < len(starts) else None
        for sub in range(width(t0)):
            epilogue(t0 + sub, pending[:, sub * IN_TILE:(sub + 1) * IN_TILE])
        pending = nxt


def _in_proj(x, g_all, ada, l, cos_t, sin_t, w_all, *, rope, mod_row):
    bsz, n, d = x.shape
    rows = bsz * n
    tm = 512
    tiles_per_batch = n // tm if mod_row is None else 1
    n_tiles = rows // tm
    rope_tiles = cos_t.shape[0] // tm
    xf = x.reshape(1, rows, d)
    row_spec = lambda w: pl.BlockSpec((1, tm, w), lambda t: (0, t, 0))
    outs = pl.pallas_call(
        functools.partial(_in_proj_kernel, rope=rope, mod_row=mod_row, tiles_per_batch=tiles_per_batch),
        grid=(n_tiles,),
        in_specs=[
            row_spec(d),
            pl.BlockSpec((1, tm, d), lambda t: (0, jnp.minimum(t + 1, n_tiles - 1), 0)),
            _layer_spec(g_all, l),
            _ada_spec(ada, l, 0),
            _ada_spec(ada, l, 1),
            pl.BlockSpec((tm, LANES), lambda t: (t % rope_tiles, 0)),
            pl.BlockSpec((tm, LANES), lambda t: (t % rope_tiles, 0)),
            pl.BlockSpec((1,) + w_all.shape[1:], lambda t: (l, 0, 0), pipeline_mode=pl.Buffered(1)),
        ],
        out_specs=[row_spec(QKV_WIDTH), row_spec(MIX_WIDTH), row_spec(GATE_WIDTH)],
        out_shape=[
            jax.ShapeDtypeStruct((1, rows, QKV_WIDTH), BF16),
            jax.ShapeDtypeStruct((1, rows, MIX_WIDTH), F32),
            jax.ShapeDtypeStruct((1, rows, GATE_WIDTH), BF16),
        ],
        scratch_shapes=[pltpu.VMEM((tm, d), BF16), pltpu.VMEM((tm, d), BF16)],
        compiler_params=_params(("arbitrary",), VMEM_LARGE_MIB),
        name="in_proj",
    )(xf, xf, g_all, ada, ada, cos_t, sin_t, w_all)
    return [o.reshape(bsz, n, o.shape[-1]) for o in outs]


def _stack_head_pair(qp):
    lane = lax.broadcasted_iota(jnp.int32, qp.shape, 1)
    zero = jnp.zeros_like(qp)
    return jnp.concatenate([jnp.where(lane < HEAD_DIM, qp, zero), jnp.where(lane < HEAD_DIM, zero, qp)], axis=0)


def _unstack_head_pair(o2):
    m = o2.shape[0] // 2
    lane = lax.broadcasted_iota(jnp.int32, (m, LANES), 1)
    return jnp.where(lane < HEAD_DIM, o2[:m], o2[m:])


_NT = (((1,), (1,)), ((), ()))


def _nattn_kernel(q_ref, k_ref, v_ref, kc_ref, vc_ref, bias_ref, o_ref, *, rows_per_step, n_rows):
    i = pl.program_id(1)
    win = NA_KH * GRID_W
    rows_per_iter = rows_per_step
    lookahead = 2

    def scores(rl, p):
        r = i * rows_per_step + rl
        rs = jnp.clip(r - NA_KH // 2, 0, n_rows - NA_KH)
        var = rs - r + (NA_KH - 1)
        qoff = pl.multiple_of(rl * GRID_W, GRID_W)
        koff = pl.multiple_of(rs * GRID_W, GRID_W)
        cs = slice(p * LANES, (p + 1) * LANES)
        q2 = _stack_head_pair(q_ref[0, pl.ds(qoff, GRID_W), cs])
        kp = k_ref[0, pl.ds(koff, win), cs]
        bias = jnp.concatenate([bias_ref[0, p, var + 2 * j] for j in range(NA_KH // 2)], axis=1)
        s_w = lax.dot_general(q2, kp, _NT, preferred_element_type=F32) + bias
        s_c = lax.dot_general(q2, kc_ref[0, :, cs], _NT, preferred_element_type=F32)
        return s_w, s_c, qoff, koff, cs

    def finish(s_w, s_c, qoff, koff, cs):
        m = jnp.maximum(jnp.max(s_w, axis=-1, keepdims=True), jnp.max(s_c, axis=-1, keepdims=True))
        p_w = jnp.exp2(s_w - m)
        p_c = jnp.exp2(s_c - m)
        l = jnp.sum(p_w, axis=-1, keepdims=True) + jnp.sum(p_c, axis=-1, keepdims=True)
        o2 = (jnp.dot(p_w.astype(BF16), v_ref[0, pl.ds(koff, win), cs], preferred_element_type=F32)
              + jnp.dot(p_c.astype(BF16), vc_ref[0, :, cs], preferred_element_type=F32))
        o2 = o2 / l
        o_ref[0, pl.ds(qoff, GRID_W), cs] = _unstack_head_pair(o2).astype(BF16)

    def body(it, carry):
        units = [(it * rows_per_iter + u, p) for u in range(rows_per_iter) for p in range(NA_HEADS // 2)]
        pending = [scores(*u) for u in units[:lookahead]]
        for idx in range(len(units)):
            if idx + lookahead < len(units):
                pending.append(scores(*units[idx + lookahead]))
            finish(*pending.pop(0))
        return carry

    lax.fori_loop(0, rows_per_step // rows_per_iter, body, 0)


def _nattn(qkv, qkv_ctx, bias_all, l):
    bsz, n, _ = qkv.shape
    n_ctx = qkv_ctx.shape[1]
    n_rows = n // GRID_W
    rows_per_step = 16
    tq = rows_per_step * GRID_W
    single = pl.Buffered(1)
    return pl.pallas_call(
        functools.partial(_nattn_kernel, rows_per_step=rows_per_step, n_rows=n_rows),
        grid=(bsz, n_rows // rows_per_step),
        in_specs=[
            pl.BlockSpec((1, tq, NA_WIDTH), lambda b, i: (b, i, 0)),
            pl.BlockSpec((1, n, NA_WIDTH), lambda b, i: (b, 0, 1)),
            pl.BlockSpec((1, n, NA_WIDTH), lambda b, i: (b, 0, 2)),
            pl.BlockSpec((1, n_ctx, NA_WIDTH), lambda b, i: (b, 0, 1)),
            pl.BlockSpec((1, n_ctx, NA_WIDTH), lambda b, i: (b, 0, 2)),
            pl.BlockSpec((1,) + bias_all.shape[1:], lambda b, i: (l, 0, 0, 0, 0), pipeline_mode=single),
        ],
        out_specs=pl.BlockSpec((1, tq, NA_WIDTH), lambda b, i: (b, i, 0)),
        out_shape=jax.ShapeDtypeStruct((bsz, n, NA_WIDTH), BF16),
        compiler_params=_params(("arbitrary", "arbitrary"), VMEM_LARGE_MIB),
        name="nattn",
    )(qkv, qkv, qkv, qkv_ctx, qkv_ctx, bias_all)


def _ctx_attn_kernel(q_ref, k_ref, v_ref, o_ref):
    for p in range(NA_HEADS // 2):
        cs = slice(p * LANES, (p + 1) * LANES)
        q2 = _stack_head_pair(q_ref[0, :, cs])
        s = lax.dot_general(q2, k_ref[0, :, cs], _NT, preferred_element_type=F32)
        m = jnp.max(s, axis=-1, keepdims=True)
        e = jnp.exp2(s - m)
        l = jnp.sum(e, axis=-1, keepdims=True)
        o2 = jnp.dot(e.astype(BF16), v_ref[0, :, cs], preferred_element_type=F32) / l
        o_ref[0, :, cs] = _unstack_head_pair(o2).astype(BF16)


def _ctx_attn(qkv_ctx):
    bsz, n_ctx, _ = qkv_ctx.shape
    return pl.pallas_call(
        _ctx_attn_kernel,
        grid=(bsz,),
        in_specs=[pl.BlockSpec((1, n_ctx, NA_WIDTH), lambda b, c=c: (b, 0, c)) for c in range(3)],
        out_specs=pl.BlockSpec((1, n_ctx, NA_WIDTH), lambda b: (b, 0, 0)),
        out_shape=jax.ShapeDtypeStruct((bsz, n_ctx, NA_WIDTH), BF16),
        compiler_params=_params(("arbitrary",), VMEM_SMALL_MIB),
        name="ctx_attn",
    )(qkv_ctx, qkv_ctx, qkv_ctx)


def _attn_bias_tables(rpb):
    depth = rpb.shape[0]
    cq = np.arange(GRID_W)
    kc = np.arange(GRID_W)
    col_start = np.clip(cq - NA_KW // 2, 0, GRID_W - NA_KW)
    valid = (kc[None, :] >= col_start[:, None]) & (kc[None, :] < col_start[:, None] + NA_KW)
    dcol = kc[None, :] - cq[:, None] + (NA_KW - 1)
    onehot = (valid[:, :, None] & (dcol[:, :, None] == np.arange(2 * NA_KW - 1)[None, None, :])).astype(np.float32)
    plane = jnp.einsum("lhrd,qkd->lhrqk", rpb, jnp.asarray(onehot), precision=lax.Precision.HIGHEST)
    plane = plane * LOG2E + jnp.asarray(np.where(valid, 0.0, MASK_VALUE).astype(np.float32))
    pairs = jnp.concatenate([plane[:, :, :-1], plane[:, :, 1:]], axis=-1)
    n_d = 2 * NA_KH - 2
    pairs = pairs.reshape(depth, NA_HEADS // 2, 2, n_d, GRID_W, 2 * GRID_W).transpose(0, 1, 3, 2, 4, 5)
    return pairs.reshape(depth, NA_HEADS // 2, n_d, 2 * GRID_W, 2 * GRID_W)


def _halo_specs(tm, n, width, col_block):
    per = tm // HALO
    last = n // HALO - 1
    return [
        pl.BlockSpec((1, HALO, width), lambda b, i: (b, jnp.maximum(i * per - 1, 0), col_block)),
        pl.BlockSpec((1, tm, width), lambda b, i: (b, i, col_block)),
        pl.BlockSpec((1, HALO, width), lambda b, i: (b, jnp.minimum((i + 1) * per, last), col_block)),
    ]


def _pool_fill(ext_ref, prev, cur, nxt, is_first, is_last, tm):
    ext_ref[0:HALO] = jnp.where(is_first, 0.0, prev)
    ext_ref[HALO:HALO + tm] = cur
    ext_ref[HALO + tm:HALO + tm + HALO] = jnp.where(is_last, 0.0, nxt)


def _pool_rows(ext_ref, wbd_ref, ps_ref, o_ref, row0, n_rows, *, tm, n_tok, seq_tile, rc=128):
    i = seq_tile
    rc = min(rc, n_rows)
    row = lax.broadcasted_iota(jnp.int32, (rc, LANES), 0)
    lane = lax.broadcasted_iota(jnp.int32, (rc, LANES), 1)
    first = lane < POOL_GROUP_DIM

    for ci in range(n_rows // rc):
        base = row0 + ci * rc
        t = i * tm + base + row
        halves = []
        for half in range(2):
            cs = slice(half * LANES, (half + 1) * LANES)
            w_a, w_b = POOL_WINDOWS[2 * half], POOL_WINDOWS[2 * half + 1]
            lo_a, lo_b = w_a // 2, w_b // 2
            hi_a, hi_b = w_a - lo_a - 1, w_b - lo_b - 1

            def ld(off, cs=cs):
                return ext_ref[pl.ds(base + (HALO + off), rc), cs]

            x0 = ld(0)
            s_a = x0
            for off in range(-lo_a, hi_a + 1):
                if off != 0:
                    s_a = s_a + ld(off)
            s_b = s_a
            for off in list(range(-lo_b, -lo_a)) + list(range(hi_a + 1, hi_b + 1)):
                s_b = s_b + ld(off)
            lo = jnp.where(first, lo_a, lo_b)
            hi = jnp.where(first, hi_a, hi_b)
            cnt = jnp.minimum(t + hi + 1, n_tok) - jnp.maximum(t - lo, 0)
            halves.append(jnp.where(first, s_a, s_b) / cnt.astype(F32) - x0)
        d = jnp.concatenate(halves, axis=1).astype(BF16)
        y = jnp.dot(d, wbd_ref[0], preferred_element_type=F32) * ps_ref[0]
        o_ref[pl.ds(base, rc), :] = y.astype(BF16)


def _glu(u):
    return u[:, :CONV_WIDTH] * _sigmoid(u[:, CONV_WIDTH:])


def _conv_fill(z_ref, prev, cur, nxt, is_first, is_last, tm):
    ext = tm + 2 * HALO
    z_ref[0, 0:HALO] = jnp.where(is_first, 0.0, prev)
    z_ref[0, HALO:HALO + tm] = cur
    z_ref[0, HALO + tm:ext] = jnp.where(is_last, 0.0, nxt)
    shifted_rows = ext - SUBLANES
    copy_rows = 128
    for s in range(1, SUBLANES):
        for c0 in range(0, shifted_rows, copy_rows):
            size = min(copy_rows, shifted_rows - c0)
            z_ref[s, c0:c0 + size] = z_ref[0, c0 + s:c0 + s + size]


def _conv_rows(z_ref, wdw_ref, bdw_ref, lng_ref, lnb_ref, o_ref, row0, n_rows, *, rc=128):
    rc = min(rc, n_rows)
    first_tap = HALO - CONV_K // 2

    for ci in range(n_rows // rc):
        base = row0 + ci * rc
        acc = jnp.broadcast_to(bdw_ref[0], (rc, CONV_WIDTH))
        for j in range(CONV_K):
            off = first_tap + j
            a0 = base + (off // SUBLANES) * SUBLANES
            acc = acc + z_ref[off % SUBLANES, a0:a0 + rc, :] * wdw_ref[0, j:j + 1, :]
        mu = jnp.mean(acc, axis=-1, keepdims=True)
        cen = acc - mu
        var = jnp.mean(cen * cen, axis=-1, keepdims=True)
        y = cen * lax.rsqrt(var + EPS) * lng_ref[0] + lnb_ref[0]
        o_ref[pl.ds(base, rc), :] = (y * _sigmoid(y)).astype(BF16)


def _channel_dft_mats(n_groups):
    c = np.arange(FOURIER_GROUP_DIM)
    ang = 2.0 * np.pi * ((c[:, None] * c[None, :]) % FOURIER_GROUP_DIM) / FOURIER_GROUP_DIM
    eye = np.eye(n_groups)
    return _bf16_const(np.kron(eye, np.cos(ang))), _bf16_const(np.kron(eye, np.sin(ang)))


def _fft_tables(n):
    assert n == FFT_N1 * FFT_N2
    k1 = np.arange(FFT_N1)
    n1 = np.arange(FFT_N1)
    n2 = np.arange(FFT_N2)
    prod = (k1[None, :, None] * (FFT_N2 * n1[None, None, :] + n2[:, None, None])) % n
    th = 2.0 * np.pi * prod / n
    cs, sn = np.cos(th), np.sin(th)
    m_big = np.concatenate([np.concatenate([cs, -sn], axis=2), np.concatenate([-sn, -cs], axis=2)], axis=1)
    k2 = np.arange(FFT_N2)
    psi = 2.0 * np.pi * ((k2[:, None] * n2[None, :]) % FFT_N2) / FFT_N2
    w_int = np.stack([np.cos(psi), np.sin(psi)], axis=2).reshape(FFT_N2, 2 * FFT_N2)
    return _bf16_const(m_big), _bf16_const(w_int)


def _fourier_kernel(x_ref, cc_ref, sc_ref, mbig_ref, wint_ref, o_ref, ab_ref, t_ref, *, n):
    xb = x_ref[0].astype(BF16)
    ab_ref[0] = jnp.dot(xb, cc_ref[...], preferred_element_type=F32)
    ab_ref[1] = jnp.dot(xb, sc_ref[...], preferred_element_type=F32)

    def stage1(n2, carry):
        za = ab_ref[0, pl.ds(n2, FFT_N1, stride=FFT_N2), :]
        zb = ab_ref[1, pl.ds(n2, FFT_N1, stride=FFT_N2), :]
        z = jnp.concatenate([za, zb], axis=0).astype(BF16)
        off = pl.multiple_of(n2 * (2 * FFT_N1), 2 * FFT_N1)
        t_ref[pl.ds(off, 2 * FFT_N1), :] = jnp.dot(mbig_ref[n2], z, preferred_element_type=F32)
        return carry

    lax.fori_loop(0, FFT_N2, stage1, 0, unroll=16)

    scale = 1.0 / np.sqrt(float(n * FOURIER_GROUP_DIM))

    def stage2(k1, carry):
        t = t_ref[pl.ds(k1, 2 * FFT_N2, stride=FFT_N1), :].astype(BF16)
        y = jnp.dot(wint_ref[...], t, preferred_element_type=F32) * scale
        o_ref[0, pl.ds(k1, FFT_N2, stride=FFT_N1), :] = y
        return carry

    lax.fori_loop(0, FFT_N1, stage2, 0, unroll=16)


def _fourier_mix(pfc):
    bsz, n, _ = pfc.shape
    cc, sc = _channel_dft_mats(LANES // FOURIER_GROUP_DIM)
    m_big, w_int = _fft_tables(n)
    col0 = POOL_WIDTH // LANES
    const2 = lambda b, h: (0, 0)
    return pl.pallas_call(
        functools.partial(_fourier_kernel, n=n),
        grid=(bsz, FOURIER_WIDTH // LANES),
        in_specs=[
            pl.BlockSpec((1, n, LANES), lambda b, h: (b, 0, col0 + h)),
            pl.BlockSpec((LANES, LANES), const2),
            pl.BlockSpec((LANES, LANES), const2),
            pl.BlockSpec(m_big.shape, lambda b, h: (0, 0, 0), pipeline_mode=pl.Buffered(1)),
            pl.BlockSpec(w_int.shape, const2),
        ],
        out_specs=pl.BlockSpec((1, n, LANES), lambda b, h: (b, 0, h)),
        out_shape=jax.ShapeDtypeStruct((bsz, n, FOURIER_WIDTH), F32),
        scratch_shapes=[pltpu.VMEM((2, n, LANES), F32), pltpu.VMEM((2 * n, LANES), F32)],
        compiler_params=_params(("arbitrary", "arbitrary"), VMEM_LARGE_MIB),
        name="fourier",
    )(pfc, cc, sc, m_big, w_int)


def _fourier_small_kernel(x_ref, cc_ref, sc_ref, cn_ref, sn_ref, o_ref, *, n):
    xb = x_ref[0].astype(BF16)
    a = jnp.dot(xb, cc_ref[...], preferred_element_type=F32).astype(BF16)
    b = jnp.dot(xb, sc_ref[...], preferred_element_type=F32).astype(BF16)
    y = (jnp.dot(cn_ref[...], a, preferred_element_type=F32) - jnp.dot(sn_ref[...], b, preferred_element_type=F32))
    o_ref[0] = y * (1.0 / np.sqrt(float(n * FOURIER_GROUP_DIM)))


def _fourier_mix_small(pfc):
    bsz, n, _ = pfc.shape
    cc, sc = _channel_dft_mats(FOURIER_WIDTH // FOURIER_GROUP_DIM)
    k = np.arange(n)
    th = 2.0 * np.pi * ((k[:, None] * k[None, :]) % n) / n
    cn, sn = _bf16_const(np.cos(th)), _bf16_const(np.sin(th))
    const2 = lambda b: (0, 0)
    return pl.pallas_call(
        functools.partial(_fourier_small_kernel, n=n),
        grid=(bsz,),
        in_specs=[
            pl.BlockSpec((1, n, FOURIER_WIDTH), lambda b: (b, 0, POOL_WIDTH // FOURIER_WIDTH)),
            pl.BlockSpec((FOURIER_WIDTH, FOURIER_WIDTH), const2),
            pl.BlockSpec((FOURIER_WIDTH, FOURIER_WIDTH), const2),
            pl.BlockSpec((n, n), const2),
            pl.BlockSpec((n, n), const2),
        ],
        out_specs=pl.BlockSpec((1, n, FOURIER_WIDTH), lambda b: (b, 0, 0)),
        out_shape=jax.ShapeDtypeStruct((bsz, n, FOURIER_WIDTH), F32),
        compiler_params=_params(("arbitrary",), VMEM_SMALL_MIB),
        name="fourier_small",
    )(pfc, cc, sc, cn, sn)


def _merge_kernel(attn_ref, four_ref, gate_ref, x_ref, mp_ref, mc_ref, mn_ref, gt_ref,
                  wbd_ref, ps_ref, wdw_ref, bdw_ref, lng_ref, lnb_ref,
                  wa_ref, wp_ref, wf_ref, wc_ref, wo_ref, o_ref, ext_ref, z_ref, pool_ref, conv_ref,
                  *, mod_row, rc, n_tok):
    d = D_MODEL
    tm = x_ref.shape[1]
    gt = _mod_row(gt_ref, mod_row)
    i = pl.program_id(1)
    is_first, is_last = i == 0, i == pl.num_programs(1) - 1
    pcols = slice(0, POOL_WIDTH)
    ccols = slice(MIX_WIDTH - CONV_WIDTH, MIX_WIDTH)
    _pool_fill(ext_ref, mp_ref[0, :, pcols], mc_ref[0, :, pcols], mn_ref[0, :, pcols], is_first, is_last, tm)
    _conv_fill(z_ref, mp_ref[0, :, ccols], mc_ref[0, :, ccols], mn_ref[0, :, ccols], is_first, is_last, tm)

    def mixers(row0):
        _pool_rows(ext_ref, wbd_ref, ps_ref, pool_ref, row0, rc, tm=tm, n_tok=n_tok, seq_tile=i)
        _conv_rows(z_ref, wdw_ref, bdw_ref, lng_ref, lnb_ref, conv_ref, row0, rc)

    def branch_dots(rows):
        acts = (attn_ref[0, rows, :], pool_ref[rows, :], four_ref[0, rows, :].astype(BF16), conv_ref[rows, :])
        return [jnp.dot(a, w_ref[0], preferred_element_type=F32)
                for a, w_ref in zip(acts, (wa_ref, wp_ref, wf_ref, wc_ref))]

    def finish(rows, ys):
        merged = None
        for bi, y in enumerate(ys):
            term = gate_ref[0, rows, bi * d:(bi + 1) * d] * y.astype(BF16)
            merged = term if merged is None else merged + term
        out = jnp.dot(merged, wo_ref[0], preferred_element_type=F32)
        o_ref[0, rows, :] = x_ref[0, rows, :] + gt * out

    chunks = [slice(c * rc, (c + 1) * rc) for c in range(tm // rc)]
    mixers(0)
    pending = branch_dots(chunks[0])
    for ci, rows in enumerate(chunks):
        nxt = None
        if ci + 1 < len(chunks):
            mixers((ci + 1) * rc)
            nxt = branch_dots(chunks[ci + 1])
        finish(rows, pending)
        pending = nxt


def _merge(attn, four_o, gate, x, pfc, ada, l, p, *, mod_row):
    bsz, n, d = x.shape
    tm = min(n, 512)
    rc = min(tm, 256)
    act = lambda w: pl.BlockSpec((1, tm, w), lambda b, i: (b, i, 0))
    params = [p[k] for k in ("wbd", "pool_scale", "w_dw", "b_dw", "ln_g", "ln_b", "wa", "wp", "wf", "wc", "wo")]
    return pl.pallas_call(
        functools.partial(_merge_kernel, mod_row=mod_row, rc=rc, n_tok=n),
        grid=(bsz, n // tm),
        in_specs=[act(NA_WIDTH), act(FOURIER_WIDTH), act(GATE_WIDTH), act(d)]
        + _halo_specs(tm, n, MIX_WIDTH, 0)
        + [_ada_spec(ada, l, 2)] + [_layer_spec(w, l) for w in params],
        out_specs=act(d),
        out_shape=jax.ShapeDtypeStruct((bsz, n, d), F32),
        scratch_shapes=[
            pltpu.VMEM((tm + 2 * HALO, POOL_WIDTH), F32),
            pltpu.VMEM((SUBLANES, tm + 2 * HALO, CONV_WIDTH), F32),
            pltpu.VMEM((tm, POOL_WIDTH), BF16),
            pltpu.VMEM((tm, CONV_WIDTH), BF16),
        ],
        compiler_params=_params(("arbitrary", "arbitrary"), VMEM_LARGE_MIB),
        name="merge",
    )(attn, four_o, gate, x, pfc, pfc, pfc, ada, *params)


def _mlp_kernel(x_ref, xn_ref, g_ref, sh_ref, sc_ref, gt_ref, w1_ref, w2_ref, gfin_ref, o_ref, ha_ref, hb_ref,
                *, final_norm, mod_row, tiles_per_batch, tf):
    step = pl.program_id(0)
    last = pl.num_programs(0) - 1
    norm = functools.partial(_norm_next_tile, g_ref=g_ref, sh_ref=sh_ref, sc_ref=sc_ref,
                             mod_row=mod_row, tiles_per_batch=tiles_per_batch)

    @pl.when(step == 0)
    def _():
        norm(ha_ref, x_ref, tile=0)

    def body(cur_ref, nxt_ref):
        def up(f):
            return jnp.dot(cur_ref[...], w1_ref[0, :, f * tf:(f + 1) * tf], preferred_element_type=F32)

        pending = up(0)
        norm(nxt_ref, xn_ref, tile=jnp.minimum(step + 1, last))
        acc = None
        for f in range(D_FF // tf):
            nxt = up(f + 1) if (f + 1) * tf < D_FF else None
            a = jnp.square(jnp.maximum(pending, 0.0)).astype(BF16)
            o = jnp.dot(a, w2_ref[0, f * tf:(f + 1) * tf, :], preferred_element_type=F32)
            acc = o if acc is None else acc + o
            pending = nxt
        row = mod_row if mod_row is not None else step // tiles_per_batch
        y = x_ref[0] + gt_ref[0, pl.ds(row, 1), :] * acc
        if final_norm:
            ms = jnp.mean(y * y, axis=-1, keepdims=True)
            y = y * lax.rsqrt(ms + EPS) * gfin_ref[...]
        o_ref[0] = y

    _ping_pong(step, body, ha_ref, hb_ref)


def _mlp(x, g_all, ada, l, w1_all, w2_all, g_final, *, final_norm, mod_row):
    bsz, n, d = x.shape
    rows = bsz * n
    tm = 512
    tf = 1024
    tiles_per_batch = n // tm if mod_row is None else 1
    n_tiles = rows // tm
    xf = x.reshape(1, rows, d)
    single = pl.Buffered(1)
    out = pl.pallas_call(
        functools.partial(_mlp_kernel, final_norm=final_norm, mod_row=mod_row,
                          tiles_per_batch=tiles_per_batch, tf=tf),
        grid=(n_tiles,),
        in_specs=[
            pl.BlockSpec((1, tm, d), lambda t: (0, t, 0)),
            pl.BlockSpec((1, tm, d), lambda t: (0, jnp.minimum(t + 1, n_tiles - 1), 0)),
            _layer_spec(g_all, l),
            _ada_spec(ada, l, 3), _ada_spec(ada, l, 4), _ada_spec(ada, l, 5),
            pl.BlockSpec((1,) + w1_all.shape[1:], lambda t: (l, 0, 0), pipeline_mode=single),
            pl.BlockSpec((1,) + w2_all.shape[1:], lambda t: (l, 0, 0), pipeline_mode=single),
            pl.BlockSpec((1, d), lambda t: (0, 0)),
        ],
        out_specs=pl.BlockSpec((1, tm, d), lambda t: (0, t, 0)),
        out_shape=jax.ShapeDtypeStruct((1, rows, d), F32),
        scratch_shapes=[pltpu.VMEM((tm, d), BF16), pltpu.VMEM((tm, d), BF16)],
        compiler_params=_params(("arbitrary",), VMEM_LARGE_MIB),
        name="mlp",
    )(xf, xf, g_all, ada, ada, ada, w1_all, w2_all, g_final.reshape(1, d))
    return out.reshape(bsz, n, d)


def _rope_tables(n):
    t = jnp.arange(n)
    row = (t // GRID_W).astype(F32)
    col = (t % GRID_W).astype(F32)
    inv = ROPE_THETA ** (-jnp.arange(0, ROPE_AXIS_DIM, 2, dtype=F32) / ROPE_AXIS_DIM)
    ang_r = row[:, None] * inv
    ang_c = col[:, None] * inv
    cr, sr, cc, sc = jnp.cos(ang_r), jnp.sin(ang_r), jnp.cos(ang_c), jnp.sin(ang_c)
    cos64 = jnp.concatenate([cr, cr, cc, cc], axis=1)
    sin64 = jnp.concatenate([-sr, sr, -sc, sc], axis=1)
    return jnp.tile(cos64, (1, LANES // HEAD_DIM)), jnp.tile(sin64, (1, LANES // HEAD_DIM))


def _pool_block_diag(w_pool):
    eye = jnp.asarray(np.eye(POOL_GROUPS, dtype=np.float32))
    bd = w_pool[:, :, :, None, :] * eye[None, :, None, :, None]
    return bd.reshape(w_pool.shape[0], POOL_WIDTH, POOL_WIDTH)


def kernel(x, c, ctx, c_ctx, w_mod, b_mod, g_mix, g_ff, w_in, rpb, w_pool, pool_scale, w_dw, b_dw, conv_ln_g,
           conv_ln_b, w_br_attn, w_br_pool, w_br_fourier, w_br_conv, w_out, w_ff1, w_ff2, g_final):
    bsz, n, d = x.shape
    depth = w_in.shape[0]
    cos_t, sin_t = _rope_tables(n)

    ada_rows = SUBLANES
    cvecs = jnp.concatenate([c, c_ctx[None, :], jnp.zeros((ada_rows - bsz - 1, d), F32)], axis=0)
    ada = _ada_all(cvecs, w_mod, b_mod)

    vec = lambda a: a.reshape(depth, 1, a.shape[-1])
    p = dict(
        wbd=_pool_block_diag(w_pool).astype(BF16), pool_scale=vec(pool_scale),
        w_dw=w_dw, b_dw=vec(b_dw), ln_g=vec(conv_ln_g), ln_b=vec(conv_ln_b),
        wa=w_br_attn.astype(BF16), wp=w_br_pool.astype(BF16), wf=w_br_fourier.astype(BF16),
        wc=w_br_conv.astype(BF16), wo=w_out.astype(BF16),
    )
    w_in_bf = w_in.astype(BF16)
    w1_bf = w_ff1.astype(BF16)
    w2_bf = w_ff2.astype(BF16)
    g_mix_v, g_ff_v = vec(g_mix), vec(g_ff)
    bias_all = _attn_bias_tables(rpb)

    h_ctx = ctx
    for l in range(depth):
        last = l == depth - 1
        qkv_c, pfc_c, gate_c = _in_proj(h_ctx, g_mix_v, ada, l, cos_t, sin_t, w_in_bf, rope=False, mod_row=bsz)
        if not last:
            attn_c = _ctx_attn(qkv_c)
            ctx_mid = _merge(attn_c, _fourier_mix_small(pfc_c), gate_c, h_ctx, pfc_c, ada, l, p, mod_row=bsz)
            ctx_next = _mlp(ctx_mid, g_ff_v, ada, l, w1_bf, w2_bf, g_final, final_norm=False, mod_row=bsz)

        qkv, pfc, gate = _in_proj(x, g_mix_v, ada, l, cos_t, sin_t, w_in_bf, rope=True, mod_row=None)
        attn = _nattn(qkv, qkv_c, bias_all, l)
        x = _merge(attn, _fourier_mix(pfc), gate, x, pfc, ada, l, p, mod_row=None)
        x = _mlp(x, g_ff_v, ada, l, w1_bf, w2_bf, g_final, final_norm=last, mod_row=None)
        if not last:
            h_ctx = ctx_next
    return x
```

```python
import functools

import numpy as np
import jax
import jax.numpy as jnp
from jax import lax
from jax.experimental import pallas as pl
from jax.experimental.pallas import tpu as pltpu

F32 = jnp.float32
BF16 = jnp.bfloat16

D_MODEL = 1024
DEPTH = 4
GRID_W = 64
NA_HEADS = 8
HEAD_DIM = 64
NA_WIDTH = NA_HEADS * HEAD_DIM
NA_KH = 8
NA_KW = 16
ROPE_AXIS_DIM = HEAD_DIM // 2
ROPE_THETA = 10000.0
POOL_WIDTH = 256
POOL_GROUPS = 4
POOL_GROUP_DIM = 64
POOL_WINDOWS = (2, 4, 8, 16)
FOURIER_WIDTH = 256
FOURIER_GROUP_DIM = 64
CONV_WIDTH = 256
CONV_K = 31
N_BRANCH = 4
D_FF = 4 * D_MODEL
EPS = 1e-6
QKV_WIDTH = 3 * NA_WIDTH
PFC_WIDTH = POOL_WIDTH + FOURIER_WIDTH + 2 * CONV_WIDTH
GATE_WIDTH = N_BRANCH * D_MODEL
IN_WIDTH = QKV_WIDTH + PFC_WIDTH + GATE_WIDTH

LANES = 128
SUBLANES = 8
HALO = 16
MASK_VALUE = -1e30
LOG2E = 1.4426950408889634
Q_SCALE = HEAD_DIM ** -0.5 * LOG2E

IN_TILE = 512
N_IN_TILES = IN_WIDTH // IN_TILE
N_QKV_TILES = QKV_WIDTH // IN_TILE
N_PFC_TILES = PFC_WIDTH // IN_TILE
IN_DOT_TILES = 2
MIX_WIDTH = POOL_WIDTH + FOURIER_WIDTH + CONV_WIDTH

FFT_N1 = 128
FFT_N2 = 64


V7X_VMEM_MIB = 64
VMEM_LARGE_MIB = V7X_VMEM_MIB - 8
VMEM_SMALL_MIB = V7X_VMEM_MIB // 2


def _params(sem, vmem_mib):
    return pltpu.CompilerParams(dimension_semantics=sem, vmem_limit_bytes=vmem_mib * 1024 * 1024)


def _bf16_const(a):
    return jnp.asarray(a, F32).astype(BF16)


def _layer_spec(arr, l):
    nd = arr.ndim - 1
    return pl.BlockSpec((1,) + arr.shape[1:], lambda *_: (l,) + (0,) * nd)


def _ada_spec(ada, l, chunk):
    return pl.BlockSpec((1, ada.shape[1], D_MODEL), lambda *_: (l, 0, chunk))


def _sigmoid(x):
    return 0.5 * jnp.tanh(0.5 * x) + 0.5


def _mod_row(ada_ref, mod_row):
    row = pl.program_id(0) if mod_row is None else mod_row
    return ada_ref[0, pl.ds(row, 1), :]


def _ada_kernel(c_ref, w_ref, b_ref, o_ref):
    cv = c_ref[...]
    s = cv * _sigmoid(cv)
    o_ref[0] = jnp.dot(s.astype(BF16), w_ref[0].astype(BF16), preferred_element_type=F32) + b_ref[0]


def _ada_all(cvecs, w_mod, b_mod):
    depth, d, width = w_mod.shape
    rows = cvecs.shape[0]
    tn = 1536
    return pl.pallas_call(
        _ada_kernel,
        grid=(depth, width // tn),
        in_specs=[
            pl.BlockSpec((rows, d), lambda l, j: (0, 0)),
            pl.BlockSpec((1, d, tn), lambda l, j: (l, 0, j)),
            pl.BlockSpec((1, 1, tn), lambda l, j: (l, 0, j)),
        ],
        out_specs=pl.BlockSpec((1, rows, tn), lambda l, j: (l, 0, j)),
        out_shape=jax.ShapeDtypeStruct((depth, rows, width), F32),
        compiler_params=_params(("arbitrary", "arbitrary"), VMEM_SMALL_MIB),
        name="ada",
    )(cvecs, w_mod, b_mod.reshape(depth, 1, width))


def _norm_modulate(x, g, shift, scale):
    ms = jnp.mean(x * x, axis=-1, keepdims=True)
    y = x * lax.rsqrt(ms + EPS) * g
    return y * (1.0 + scale) + shift


def _norm_next_tile(dst_ref, x_ref, g_ref, sh_ref, sc_ref, tile, mod_row, tiles_per_batch):
    row = mod_row if mod_row is not None else tile // tiles_per_batch
    dst_ref[...] = _norm_modulate(x_ref[0], g_ref[0], sh_ref[0, pl.ds(row, 1), :],
                                  sc_ref[0, pl.ds(row, 1), :]).astype(BF16)


def _ping_pong(step, body, ref_a, ref_b):
    pl.when(step % 2 == 0)(lambda: body(ref_a, ref_b))
    pl.when(step % 2 == 1)(lambda: body(ref_b, ref_a))


def _in_proj_kernel(x_ref, xn_ref, g_ref, sh_ref, sc_ref, cos_ref, sin_ref, w_ref,
                    qkv_ref, pfc_ref, gate_ref, ha_ref, hb_ref, *, rope, mod_row, tiles_per_batch):
    step = pl.program_id(0)
    last = pl.num_programs(0) - 1
    norm = functools.partial(_norm_next_tile, g_ref=g_ref, sh_ref=sh_ref, sc_ref=sc_ref,
                             mod_row=mod_row, tiles_per_batch=tiles_per_batch)

    @pl.when(step == 0)
    def _():
        norm(ha_ref, x_ref, tile=0)

    def body(cur_ref, nxt_ref):
        _in_proj_tile(cur_ref, lambda: norm(nxt_ref, xn_ref, tile=jnp.minimum(step + 1, last)),
                      cos_ref, sin_ref, w_ref, qkv_ref, pfc_ref, gate_ref, rope=rope)

    _ping_pong(step, body, ha_ref, hb_ref)


def _in_proj_tile(h_ref, prepare_next, cos_ref, sin_ref, w_ref, qkv_ref, pfc_ref, gate_ref, *, rope):
    def mm(t0, n_sub):
        return jnp.dot(h_ref[...], w_ref[0, :, t0 * IN_TILE:(t0 + n_sub) * IN_TILE], preferred_element_type=F32)

    def qk_epilogue(t, acc):
        col0 = t * IN_TILE
        mult = Q_SCALE if t == 0 else None
        if rope:
            cos = cos_ref[...]
            sin = sin_ref[...]
            if mult is not None:
                cos, sin = cos * mult, sin * mult
            lane = lax.broadcasted_iota(jnp.int32, cos.shape, 1)
            first = (lane % (ROPE_AXIS_DIM)) < (ROPE_AXIS_DIM // 2)
            for cc in range(IN_TILE // LANES):
                a = acc[:, cc * LANES:(cc + 1) * LANES]
                partner = jnp.where(first,
                                    pltpu.roll(a, LANES - ROPE_AXIS_DIM // 2, 1),
                                    pltpu.roll(a, ROPE_AXIS_DIM // 2, 1))
                qkv_ref[0, :, col0 + cc * LANES:col0 + (cc + 1) * LANES] = (a * cos + partner * sin).astype(BF16)
        else:
            qkv_ref[0, :, col0:col0 + IN_TILE] = (acc if mult is None else acc * mult).astype(BF16)

    def epilogue(t, acc):
        if t < 2:
            qk_epilogue(t, acc)
        elif t < N_QKV_TILES:
            qkv_ref[0, :, t * IN_TILE:(t + 1) * IN_TILE] = acc.astype(BF16)
        elif t == N_QKV_TILES:
            pfc_ref[0, :, 0:IN_TILE] = acc
        elif t == N_QKV_TILES + 1:
            pfc_ref[0, :, IN_TILE:IN_TILE + CONV_WIDTH] = _glu(acc)
        else:
            c0 = (t - N_QKV_TILES - N_PFC_TILES) * IN_TILE
            gate_ref[0, :, c0:c0 + IN_TILE] = _sigmoid(acc).astype(BF16)

    starts = list(range(0, N_IN_TILES, IN_DOT_TILES))
    width = lambda t0: min(IN_DOT_TILES, N_IN_TILES - t0)
    pending = mm(starts[0], width(starts[0]))
    prepare_next()
    for si, t0 in enumerate(starts):
        nxt = mm(starts[si + 1], width(starts[si + 1])) if si + 1 < len(starts) else None
        for sub in range(width(t0)):
            epilogue(t0 + sub, pending[:, sub * IN_TILE:(sub + 1) * IN_TILE])
        pending = nxt


def _in_proj(x, g_all, ada, l, cos_t, sin_t, w_all, *, rope, mod_row):
    bsz, n, d = x.shape
    rows = bsz * n
    tm = 512
    tiles_per_batch = n // tm if mod_row is None else 1
    n_tiles = rows // tm
    rope_tiles = cos_t.shape[0] // tm
    xf = x.reshape(1, rows, d)
    row_spec = lambda w: pl.BlockSpec((1, tm, w), lambda t: (0, t, 0))
    outs = pl.pallas_call(
        functools.partial(_in_proj_kernel, rope=rope, mod_row=mod_row, tiles_per_batch=tiles_per_batch),
        grid=(n_tiles,),
        in_specs=[
            pl.BlockSpec((1, tm, d), lambda t: (0, 0, 0)),
            pl.BlockSpec((1, tm, d), lambda t: (0, jnp.minimum(t + 1, n_tiles - 1), 0)),
            _layer_spec(g_all, l),
            _ada_spec(ada, l, 0),
            _ada_spec(ada, l, 1),
            pl.BlockSpec((tm, LANES), lambda t: (t % rope_tiles, 0)),
            pl.BlockSpec((tm, LANES), lambda t: (t % rope_tiles, 0)),
            pl.BlockSpec((1,) + w_all.shape[1:], lambda t: (l, 0, 0), pipeline_mode=pl.Buffered(1)),
        ],
        out_specs=[row_spec(QKV_WIDTH), row_spec(MIX_WIDTH), row_spec(GATE_WIDTH)],
        out_shape=[
            jax.ShapeDtypeStruct((1, rows, QKV_WIDTH), BF16),
            jax.ShapeDtypeStruct((1, rows, MIX_WIDTH), F32),
            jax.ShapeDtypeStruct((1, rows, GATE_WIDTH), BF16),
        ],
        scratch_shapes=[pltpu.VMEM((tm, d), BF16), pltpu.VMEM((tm, d), BF16)],
        compiler_params=_params(("arbitrary",), VMEM_LARGE_MIB),
        name="in_proj",
    )(xf, xf, g_all, ada, ada, cos_t, sin_t, w_all)
    return [o.reshape(bsz, n, o.shape[-1]) for o in outs]


def _stack_head_pair(qp):
    lane = lax.broadcasted_iota(jnp.int32, qp.shape, 1)
    zero = jnp.zeros_like(qp)
    return jnp.concatenate([jnp.where(lane < HEAD_DIM, qp, zero), jnp.where(lane < HEAD_DIM, zero, qp)], axis=0)


def _unstack_head_pair(o2):
    m = o2.shape[0] // 2
    lane = lax.broadcasted_iota(jnp.int32, (m, LANES), 1)
    return jnp.where(lane < HEAD_DIM, o2[:m], o2[m:])


_NT = (((1,), (1,)), ((), ()))


def _nattn_kernel(q_ref, k_ref, v_ref, kc_ref, vc_ref, bias_ref, o_ref, *, rows_per_step, n_rows):
    i = pl.program_id(1)
    win = NA_KH * GRID_W
    rows_per_iter = rows_per_step
    lookahead = 2

    def scores(rl, p):
        r = i * rows_per_step + rl
        rs = jnp.clip(r - NA_KH // 2, 0, n_rows - NA_KH)
        var = rs - r + (NA_KH - 1)
        qoff = pl.multiple_of(rl * GRID_W, GRID_W)
        koff = pl.multiple_of(rs * GRID_W, GRID_W)
        cs = slice(p * LANES, (p + 1) * LANES)
        q2 = _stack_head_pair(q_ref[0, pl.ds(qoff, GRID_W), cs])
        kp = k_ref[0, pl.ds(koff, win), cs]
        bias = jnp.concatenate([bias_ref[0, p, var + 2 * j] for j in range(NA_KH // 2)], axis=1)
        s_w = lax.dot_general(q2, kp, _NT, preferred_element_type=F32) + bias
        s_c = lax.dot_general(q2, kc_ref[0, :, cs], _NT, preferred_element_type=F32)
        return s_w, s_c, qoff, koff, cs

    def finish(s_w, s_c, qoff, koff, cs):
        m = jnp.maximum(jnp.max(s_w, axis=-1, keepdims=True), jnp.max(s_c, axis=-1, keepdims=True))
        p_w = jnp.exp2(s_w - m)
        p_c = jnp.exp2(s_c - m)
        l = jnp.sum(p_w, axis=-1, keepdims=True) + jnp.sum(p_c, axis=-1, keepdims=True)
        o2 = (jnp.dot(p_w.astype(BF16), v_ref[0, pl.ds(koff, win), cs], preferred_element_type=F32)
              + jnp.dot(p_c.astype(BF16), vc_ref[0, :, cs], preferred_element_type=F32))
        o2 = o2 / l
        o_ref[0, pl.ds(qoff, GRID_W), cs] = _unstack_head_pair(o2).astype(BF16)

    def body(it, carry):
        units = [(it * rows_per_iter + u, p) for u in range(rows_per_iter) for p in range(NA_HEADS // 2)]
        pending = [scores(*u) for u in units[:lookahead]]
        for idx in range(len(units)):
            if idx + lookahead < len(units):
                pending.append(scores(*units[idx + lookahead]))
            finish(*pending.pop(0))
        return carry

    lax.fori_loop(0, rows_per_step // rows_per_iter, body, 0)


def _nattn(qkv, qkv_ctx, bias_all, l):
    bsz, n, _ = qkv.shape
    n_ctx = qkv_ctx.shape[1]
    n_rows = n // GRID_W
    rows_per_step = 16
    tq = rows_per_step * GRID_W
    single = pl.Buffered(1)
    return pl.pallas_call(
        functools.partial(_nattn_kernel, rows_per_step=rows_per_step, n_rows=n_rows),
        grid=(bsz, n_rows // rows_per_step),
        in_specs=[
            pl.BlockSpec((1, tq, NA_WIDTH), lambda b, i: (b, i, 0)),
            pl.BlockSpec((1, n, NA_WIDTH), lambda b, i: (b, 0, 1)),
            pl.BlockSpec((1, n, NA_WIDTH), lambda b, i: (b, 0, 2)),
            pl.BlockSpec((1, n_ctx, NA_WIDTH), lambda b, i: (b, 0, 1)),
            pl.BlockSpec((1, n_ctx, NA_WIDTH), lambda b, i: (b, 0, 2)),
            pl.BlockSpec((1,) + bias_all.shape[1:], lambda b, i: (l, 0, 0, 0, 0), pipeline_mode=single),
        ],
        out_specs=pl.BlockSpec((1, tq, NA_WIDTH), lambda b, i: (b, i, 0)),
        out_shape=jax.ShapeDtypeStruct((bsz, n, NA_WIDTH), BF16),
        compiler_params=_params(("arbitrary", "arbitrary"), VMEM_LARGE_MIB),
        name="nattn",
    )(qkv, qkv, qkv, qkv_ctx, qkv_ctx, bias_all)


def _ctx_attn_kernel(q_ref, k_ref, v_ref, o_ref):
    for p in range(NA_HEADS // 2):
        cs = slice(p * LANES, (p + 1) * LANES)
        q2 = _stack_head_pair(q_ref[0, :, cs])
        s = lax.dot_general(q2, k_ref[0, :, cs], _NT, preferred_element_type=F32)
        m = jnp.max(s, axis=-1, keepdims=True)
        e = jnp.exp2(s - m)
        l = jnp.sum(e, axis=-1, keepdims=True)
        o2 = jnp.dot(e.astype(BF16), v_ref[0, :, cs], preferred_element_type=F32) / l
        o_ref[0, :, cs] = _unstack_head_pair(o2).astype(BF16)


def _ctx_attn(qkv_ctx):
    bsz, n_ctx, _ = qkv_ctx.shape
    return pl.pallas_call(
        _ctx_attn_kernel,
        grid=(bsz,),
        in_specs=[pl.BlockSpec((1, n_ctx, NA_WIDTH), lambda b, c=c: (b, 0, c)) for c in range(3)],
        out_specs=pl.BlockSpec((1, n_ctx, NA_WIDTH), lambda b: (b, 0, 0)),
        out_shape=jax.ShapeDtypeStruct((bsz, n_ctx, NA_WIDTH), BF16),
        compiler_params=_params(("arbitrary",), VMEM_SMALL_MIB),
        name="ctx_attn",
    )(qkv_ctx, qkv_ctx, qkv_ctx)


def _attn_bias_tables(rpb):
    depth = rpb.shape[0]
    cq = np.arange(GRID_W)
    kc = np.arange(GRID_W)
    col_start = np.clip(cq - NA_KW // 2, 0, GRID_W - NA_KW)
    valid = (kc[None, :] >= col_start[:, None]) & (kc[None, :] < col_start[:, None] + NA_KW)
    dcol = kc[None, :] - cq[:, None] + (NA_KW - 1)
    onehot = (valid[:, :, None] & (dcol[:, :, None] == np.arange(2 * NA_KW - 1)[None, None, :])).astype(np.float32)
    plane = jnp.einsum("lhrd,qkd->lhrqk", rpb, jnp.asarray(onehot), precision=lax.Precision.HIGHEST)
    plane = plane * LOG2E + jnp.asarray(np.where(valid, 0.0, MASK_VALUE).astype(np.float32))
    pairs = jnp.concatenate([plane[:, :, :-1], plane[:, :, 1:]], axis=-1)
    n_d = 2 * NA_KH - 2
    pairs = pairs.reshape(depth, NA_HEADS // 2, 2, n_d, GRID_W, 2 * GRID_W).transpose(0, 1, 3, 2, 4, 5)
    return pairs.reshape(depth, NA_HEADS // 2, n_d, 2 * GRID_W, 2 * GRID_W)


def _halo_specs(tm, n, width, col_block):
    per = tm // HALO
    last = n // HALO - 1
    return [
        pl.BlockSpec((1, HALO, width), lambda b, i: (b, jnp.maximum(i * per - 1, 0), col_block)),
        pl.BlockSpec((1, tm, width), lambda b, i: (b, i, col_block)),
        pl.BlockSpec((1, HALO, width), lambda b, i: (b, jnp.minimum((i + 1) * per, last), col_block)),
    ]


def _pool_fill(ext_ref, prev, cur, nxt, is_first, is_last, tm):
    ext_ref[0:HALO] = jnp.where(is_first, 0.0, prev)
    ext_ref[HALO:HALO + tm] = cur
    ext_ref[HALO + tm:HALO + tm + HALO] = jnp.where(is_last, 0.0, nxt)


def _pool_rows(ext_ref, wbd_ref, ps_ref, o_ref, row0, n_rows, *, tm, n_tok, seq_tile, rc=128):
    i = seq_tile
    rc = min(rc, n_rows)
    row = lax.broadcasted_iota(jnp.int32, (rc, LANES), 0)
    lane = lax.broadcasted_iota(jnp.int32, (rc, LANES), 1)
    first = lane < POOL_GROUP_DIM

    for ci in range(n_rows // rc):
        base = row0 + ci * rc
        t = i * tm + base + row
        halves = []
        for half in range(2):
            cs = slice(half * LANES, (half + 1) * LANES)
            w_a, w_b = POOL_WINDOWS[2 * half], POOL_WINDOWS[2 * half + 1]
            lo_a, lo_b = w_a // 2, w_b // 2
            hi_a, hi_b = w_a - lo_a - 1, w_b - lo_b - 1

            def ld(off, cs=cs):
                return ext_ref[pl.ds(base + (HALO + off), rc), cs]

            x0 = ld(0)
            s_a = x0
            for off in range(-lo_a, hi_a + 1):
                if off != 0:
                    s_a = s_a + ld(off)
            s_b = s_a
            for off in list(range(-lo_b, -lo_a)) + list(range(hi_a + 1, hi_b + 1)):
                s_b = s_b + ld(off)
            lo = jnp.where(first, lo_a, lo_b)
            hi = jnp.where(first, hi_a, hi_b)
            cnt = jnp.minimum(t + hi + 1, n_tok) - jnp.maximum(t - lo, 0)
            halves.append(jnp.where(first, s_a, s_b) / cnt.astype(F32) - x0)
        d = jnp.concatenate(halves, axis=1).astype(BF16)
        y = jnp.dot(d, wbd_ref[0], preferred_element_type=F32) * ps_ref[0]
        o_ref[pl.ds(base, rc), :] = y.astype(BF16)


def _glu(u):
    return u[:, :CONV_WIDTH] * _sigmoid(u[:, CONV_WIDTH:])


def _conv_fill(z_ref, prev, cur, nxt, is_first, is_last, tm):
    ext = tm + 2 * HALO
    z_ref[0, 0:HALO] = jnp.where(is_first, 0.0, prev)
    z_ref[0, HALO:HALO + tm] = cur
    z_ref[0, HALO + tm:ext] = jnp.where(is_last, 0.0, nxt)
    shifted_rows = ext - SUBLANES
    copy_rows = 128
    for s in range(1, SUBLANES):
        for c0 in range(0, shifted_rows, copy_rows):
            size = min(copy_rows, shifted_rows - c0)
            z_ref[s, c0:c0 + size] = z_ref[0, c0 + s:c0 + s + size]


def _conv_rows(z_ref, wdw_ref, bdw_ref, lng_ref, lnb_ref, o_ref, row0, n_rows, *, rc=128):
    rc = min(rc, n_rows)
    first_tap = HALO - CONV_K // 2

    for ci in range(n_rows // rc):
        base = row0 + ci * rc
        acc = jnp.broadcast_to(bdw_ref[0], (rc, CONV_WIDTH))
        for j in range(CONV_K):
            off = first_tap + j
            a0 = base + (off // SUBLANES) * SUBLANES
            acc = acc + z_ref[off % SUBLANES, a0:a0 + rc, :] * wdw_ref[0, j:j + 1, :]
        mu = jnp.mean(acc, axis=-1, keepdims=True)
        cen = acc - mu
        var = jnp.mean(cen * cen, axis=-1, keepdims=True)
        y = cen * lax.rsqrt(var + EPS) * lng_ref[0] + lnb_ref[0]
        o_ref[pl.ds(base, rc), :] = (y * _sigmoid(y)).astype(BF16)


def _channel_dft_mats(n_groups):
    c = np.arange(FOURIER_GROUP_DIM)
    ang = 2.0 * np.pi * ((c[:, None] * c[None, :]) % FOURIER_GROUP_DIM) / FOURIER_GROUP_DIM
    eye = np.eye(n_groups)
    return _bf16_const(np.kron(eye, np.cos(ang))), _bf16_const(np.kron(eye, np.sin(ang)))


def _fft_tables(n):
    assert n == FFT_N1 * FFT_N2
    k1 = np.arange(FFT_N1)
    n1 = np.arange(FFT_N1)
    n2 = np.arange(FFT_N2)
    prod = (k1[None, :, None] * (FFT_N2 * n1[None, None, :] + n2[:, None, None])) % n
    th = 2.0 * np.pi * prod / n
    cs, sn = np.cos(th), np.sin(th)
    m_big = np.concatenate([np.concatenate([cs, -sn], axis=2), np.concatenate([-sn, -cs], axis=2)], axis=1)
    k2 = np.arange(FFT_N2)
    psi = 2.0 * np.pi * ((k2[:, None] * n2[None, :]) % FFT_N2) / FFT_N2
    w_int = np.stack([np.cos(psi), np.sin(psi)], axis=2).reshape(FFT_N2, 2 * FFT_N2)
    return _bf16_const(m_big), _bf16_const(w_int)


def _fourier_kernel(x_ref, cc_ref, sc_ref, mbig_ref, wint_ref, o_ref, ab_ref, t_ref, *, n):
    xb = x_ref[0].astype(BF16)
    ab_ref[0] = jnp.dot(xb, cc_ref[...], preferred_element_type=F32)
    ab_ref[1] = jnp.dot(xb, sc_ref[...], preferred_element_type=F32)

    def stage1(n2, carry):
        za = ab_ref[0, pl.ds(n2, FFT_N1, stride=FFT_N2), :]
        zb = ab_ref[1, pl.ds(n2, FFT_N1, stride=FFT_N2), :]
        z = jnp.concatenate([za, zb], axis=0).astype(BF16)
        off = pl.multiple_of(n2 * (2 * FFT_N1), 2 * FFT_N1)
        t_ref[pl.ds(off, 2 * FFT_N1), :] = jnp.dot(mbig_ref[n2], z, preferred_element_type=F32)
        return carry

    lax.fori_loop(0, FFT_N2, stage1, 0, unroll=16)

    scale = 1.0 / np.sqrt(float(n * FOURIER_GROUP_DIM))

    def stage2(k1, carry):
        t = t_ref[pl.ds(k1, 2 * FFT_N2, stride=FFT_N1), :].astype(BF16)
        y = jnp.dot(wint_ref[...], t, preferred_element_type=F32) * scale
        o_ref[0, pl.ds(k1, FFT_N2, stride=FFT_N1), :] = y
        return carry

    lax.fori_loop(0, FFT_N1, stage2, 0, unroll=16)


def _fourier_mix(pfc):
    bsz, n, _ = pfc.shape
    cc, sc = _channel_dft_mats(LANES // FOURIER_GROUP_DIM)
    m_big, w_int = _fft_tables(n)
    col0 = POOL_WIDTH // LANES
    const2 = lambda b, h: (0, 0)
    return pl.pallas_call(
        functools.partial(_fourier_kernel, n=n),
        grid=(bsz, FOURIER_WIDTH // LANES),
        in_specs=[
            pl.BlockSpec((1, n, LANES), lambda b, h: (b, 0, col0 + h)),
            pl.BlockSpec((LANES, LANES), const2),
            pl.BlockSpec((LANES, LANES), const2),
            pl.BlockSpec(m_big.shape, lambda b, h: (0, 0, 0), pipeline_mode=pl.Buffered(1)),
            pl.BlockSpec(w_int.shape, const2),
        ],
        out_specs=pl.BlockSpec((1, n, LANES), lambda b, h: (b, 0, h)),
        out_shape=jax.ShapeDtypeStruct((bsz, n, FOURIER_WIDTH), F32),
        scratch_shapes=[pltpu.VMEM((2, n, LANES), F32), pltpu.VMEM((2 * n, LANES), F32)],
        compiler_params=_params(("arbitrary", "arbitrary"), VMEM_LARGE_MIB),
        name="fourier",
    )(pfc, cc, sc, m_big, w_int)


def _fourier_small_kernel(x_ref, cc_ref, sc_ref, cn_ref, sn_ref, o_ref, *, n):
    xb = x_ref[0].astype(BF16)
    a = jnp.dot(xb, cc_ref[...], preferred_element_type=F32).astype(BF16)
    b = jnp.dot(xb, sc_ref[...], preferred_element_type=F32).astype(BF16)
    y = (jnp.dot(cn_ref[...], a, preferred_element_type=F32) - jnp.dot(sn_ref[...], b, preferred_element_type=F32))
    o_ref[0] = y * (1.0 / np.sqrt(float(n * FOURIER_GROUP_DIM)))


def _fourier_mix_small(pfc):
    bsz, n, _ = pfc.shape
    cc, sc = _channel_dft_mats(FOURIER_WIDTH // FOURIER_GROUP_DIM)
    k = np.arange(n)
    th = 2.0 * np.pi * ((k[:, None] * k[None, :]) % n) / n
    cn, sn = _bf16_const(np.cos(th)), _bf16_const(np.sin(th))
    const2 = lambda b: (0, 0)
    return pl.pallas_call(
        functools.partial(_fourier_small_kernel, n=n),
        grid=(bsz,),
        in_specs=[
            pl.BlockSpec((1, n, FOURIER_WIDTH), lambda b: (b, 0, POOL_WIDTH // FOURIER_WIDTH)),
            pl.BlockSpec((FOURIER_WIDTH, FOURIER_WIDTH), const2),
            pl.BlockSpec((FOURIER_WIDTH, FOURIER_WIDTH), const2),
            pl.BlockSpec((n, n), const2),
            pl.BlockSpec((n, n), const2),
        ],
        out_specs=pl.BlockSpec((1, n, FOURIER_WIDTH), lambda b: (b, 0, 0)),
        out_shape=jax.ShapeDtypeStruct((bsz, n, FOURIER_WIDTH), F32),
        compiler_params=_params(("arbitrary",), VMEM_SMALL_MIB),
        name="fourier_small",
    )(pfc, cc, sc, cn, sn)


def _merge_kernel(attn_ref, four_ref, gate_ref, x_ref, pp_ref, pc_ref, pn_ref, cp_ref, cc_ref, cn_ref, gt_ref,
                  wbd_ref, ps_ref, wdw_ref, bdw_ref, lng_ref, lnb_ref,
                  wa_ref, wp_ref, wf_ref, wc_ref, wo_ref, o_ref, ext_ref, z_ref, pool_ref, conv_ref,
                  *, mod_row, rc, n_tok):
    d = D_MODEL
    tm = x_ref.shape[1]
    gt = _mod_row(gt_ref, mod_row)
    i = pl.program_id(1)
    is_first, is_last = i == 0, i == pl.num_programs(1) - 1
    _pool_fill(ext_ref, pp_ref[0], pc_ref[0], pn_ref[0], is_first, is_last, tm)
    _conv_fill(z_ref, cp_ref[0], cc_ref[0], cn_ref[0], is_first, is_last, tm)

    def mixers(row0):
        _pool_rows(ext_ref, wbd_ref, ps_ref, pool_ref, row0, rc, tm=tm, n_tok=n_tok, seq_tile=i)
        _conv_rows(z_ref, wdw_ref, bdw_ref, lng_ref, lnb_ref, conv_ref, row0, rc)

    def branch_dots(rows):
        acts = (attn_ref[0, rows, :], pool_ref[rows, :], four_ref[0, rows, :].astype(BF16), conv_ref[rows, :])
        return [jnp.dot(a, w_ref[0], preferred_element_type=F32)
                for a, w_ref in zip(acts, (wa_ref, wp_ref, wf_ref, wc_ref))]

    def finish(rows, ys):
        merged = None
        for bi, y in enumerate(ys):
            term = gate_ref[0, rows, bi * d:(bi + 1) * d] * y.astype(BF16)
            merged = term if merged is None else merged + term
        out = jnp.dot(merged, wo_ref[0], preferred_element_type=F32)
        o_ref[0, rows, :] = x_ref[0, rows, :] + gt * out

    chunks = [slice(c * rc, (c + 1) * rc) for c in range(tm // rc)]
    mixers(0)
    pending = branch_dots(chunks[0])
    for ci, rows in enumerate(chunks):
        nxt = None
        if ci + 1 < len(chunks):
            mixers((ci + 1) * rc)
            nxt = branch_dots(chunks[ci + 1])
        finish(rows, pending)
        pending = nxt


def _merge(attn, four_o, gate, x, pfc, ada, l, p, *, mod_row):
    bsz, n, d = x.shape
    tm = min(n, 512)
    rc = min(tm, 256)
    act = lambda w: pl.BlockSpec((1, tm, w), lambda b, i: (b, i, 0))
    params = [p[k] for k in ("wbd", "pool_scale", "w_dw", "b_dw", "ln_g", "ln_b", "wa", "wp", "wf", "wc", "wo")]
    return pl.pallas_call(
        functools.partial(_merge_kernel, mod_row=mod_row, rc=rc, n_tok=n),
        grid=(bsz, n // tm),
        in_specs=[act(NA_WIDTH), act(FOURIER_WIDTH), act(GATE_WIDTH), act(d)]
        + _halo_specs(tm, n, POOL_WIDTH, 0) + _halo_specs(tm, n, CONV_WIDTH, MIX_WIDTH // CONV_WIDTH - 1)
        + [_ada_spec(ada, l, 2)] + [_layer_spec(w, l) for w in params],
        out_specs=act(d),
        out_shape=jax.ShapeDtypeStruct((bsz, n, d), F32),
        scratch_shapes=[
            pltpu.VMEM((tm + 2 * HALO, POOL_WIDTH), F32),
            pltpu.VMEM((SUBLANES, tm + 2 * HALO, CONV_WIDTH), F32),
            pltpu.VMEM((tm, POOL_WIDTH), BF16),
            pltpu.VMEM((tm, CONV_WIDTH), BF16),
        ],
        compiler_params=_params(("arbitrary", "arbitrary"), VMEM_LARGE_MIB),
        name="merge",
    )(attn, four_o, gate, x, pfc, pfc, pfc, pfc, pfc, pfc, ada, *params)


def _mlp_kernel(x_ref, xn_ref, g_ref, sh_ref, sc_ref, gt_ref, w1_ref, w2_ref, gfin_ref, o_ref, ha_ref, hb_ref,
                *, final_norm, mod_row, tiles_per_batch, tf):
    step = pl.program_id(0)
    last = pl.num_programs(0) - 1
    norm = functools.partial(_norm_next_tile, g_ref=g_ref, sh_ref=sh_ref, sc_ref=sc_ref,
                             mod_row=mod_row, tiles_per_batch=tiles_per_batch)

    @pl.when(step == 0)
    def _():
        norm(ha_ref, x_ref, tile=0)

    def body(cur_ref, nxt_ref):
        def up(f):
            return jnp.dot(cur_ref[...], w1_ref[0, :, f * tf:(f + 1) * tf], preferred_element_type=F32)

        pending = up(0)
        norm(nxt_ref, xn_ref, tile=jnp.minimum(step + 1, last))
        acc = None
        for f in range(D_FF // tf):
            nxt = up(f + 1) if (f + 1) * tf < D_FF else None
            a = jnp.square(jnp.maximum(pending, 0.0)).astype(BF16)
            o = jnp.dot(a, w2_ref[0, f * tf:(f + 1) * tf, :], preferred_element_type=F32)
            acc = o if acc is None else acc + o
            pending = nxt
        row = mod_row if mod_row is not None else step // tiles_per_batch
        y = x_ref[0] + gt_ref[0, pl.ds(row, 1), :] * acc
        if final_norm:
            ms = jnp.mean(y * y, axis=-1, keepdims=True)
            y = y * lax.rsqrt(ms + EPS) * gfin_ref[...]
        o_ref[0] = y

    _ping_pong(step, body, ha_ref, hb_ref)


def _mlp(x, g_all, ada, l, w1_all, w2_all, g_final, *, final_norm, mod_row):
    bsz, n, d = x.shape
    rows = bsz * n
    tm = 512
    tf = 1024
    tiles_per_batch = n // tm if mod_row is None else 1
    n_tiles = rows // tm
    xf = x.reshape(1, rows, d)
    single = pl.Buffered(1)
    out = pl.pallas_call(
        functools.partial(_mlp_kernel, final_norm=final_norm, mod_row=mod_row,
                          tiles_per_batch=tiles_per_batch, tf=tf),
        grid=(n_tiles,),
        in_specs=[
            pl.BlockSpec((1, tm, d), lambda t: (0, t, 0)),
            pl.BlockSpec((1, tm, d), lambda t: (0, jnp.minimum(t + 1, n_tiles - 1), 0)),
            _layer_spec(g_all, l),
            _ada_spec(ada, l, 3), _ada_spec(ada, l, 4), _ada_spec(ada, l, 5),
            pl.BlockSpec((1,) + w1_all.shape[1:], lambda t: (l, 0, 0), pipeline_mode=single),
            pl.BlockSpec((1,) + w2_all.shape[1:], lambda t: (l, 0, 0), pipeline_mode=single),
            pl.BlockSpec((1, d), lambda t: (0, 0)),
        ],
        out_specs=pl.BlockSpec((1, tm, d), lambda t: (0, t, 0)),
        out_shape=jax.ShapeDtypeStruct((1, rows, d), F32),
        scratch_shapes=[pltpu.VMEM((tm, d), BF16), pltpu.VMEM((tm, d), BF16)],
        compiler_params=_params(("arbitrary",), VMEM_LARGE_MIB),
        name="mlp",
    )(xf, xf, g_all, ada, ada, ada, w1_all, w2_all, g_final.reshape(1, d))
    return out.reshape(bsz, n, d)


def _rope_tables(n):
    t = jnp.arange(n)
    row = (t // GRID_W).astype(F32)
    col = (t % GRID_W).astype(F32)
    inv = ROPE_THETA ** (-jnp.arange(0, ROPE_AXIS_DIM, 2, dtype=F32) / ROPE_AXIS_DIM)
    ang_r = row[:, None] * inv
    ang_c = col[:, None] * inv
    cr, sr, cc, sc = jnp.cos(ang_r), jnp.sin(ang_r), jnp.cos(ang_c), jnp.sin(ang_c)
    cos64 = jnp.concatenate([cr, cr, cc, cc], axis=1)
    sin64 = jnp.concatenate([-sr, sr, -sc, sc], axis=1)
    return jnp.tile(cos64, (1, LANES // HEAD_DIM)), jnp.tile(sin64, (1, LANES // HEAD_DIM))


def _pool_block_diag(w_pool):
    eye = jnp.asarray(np.eye(POOL_GROUPS, dtype=np.float32))
    bd = w_pool[:, :, :, None, :] * eye[None, :, None, :, None]
    return bd.reshape(w_pool.shape[0], POOL_WIDTH, POOL_WIDTH)


def kernel(x, c, ctx, c_ctx, w_mod, b_mod, g_mix, g_ff, w_in, rpb, w_pool, pool_scale, w_dw, b_dw, conv_ln_g,
           conv_ln_b, w_br_attn, w_br_pool, w_br_fourier, w_br_conv, w_out, w_ff1, w_ff2, g_final):
    bsz, n, d = x.shape
    depth = w_in.shape[0]
    cos_t, sin_t = _rope_tables(n)

    ada_rows = SUBLANES
    cvecs = jnp.concatenate([c, c_ctx[None, :], jnp.zeros((ada_rows - bsz - 1, d), F32)], axis=0)
    ada = _ada_all(cvecs, w_mod, b_mod)

    vec = lambda a: a.reshape(depth, 1, a.shape[-1])
    p = dict(
        wbd=_pool_block_diag(w_pool).astype(BF16), pool_scale=vec(pool_scale),
        w_dw=w_dw, b_dw=vec(b_dw), ln_g=vec(conv_ln_g), ln_b=vec(conv_ln_b),
        wa=w_br_attn.astype(BF16), wp=w_br_pool.astype(BF16), wf=w_br_fourier.astype(BF16),
        wc=w_br_conv.astype(BF16), wo=w_out.astype(BF16),
    )
    w_in_bf = w_in.astype(BF16)
    w1_bf = w_ff1.astype(BF16)
    w2_bf = w_ff2.astype(BF16)
    g_mix_v, g_ff_v = vec(g_mix), vec(g_ff)
    bias_all = _attn_bias_tables(rpb)

    h_ctx = ctx
    for l in range(depth):
        last = l == depth - 1
        qkv_c, pfc_c, gate_c = _in_proj(h_ctx, g_mix_v, ada, l, cos_t, sin_t, w_in_bf, rope=False, mod_row=bsz)
        if not last:
            attn_c = _ctx_attn(qkv_c)
            ctx_mid = _merge(attn_c, _fourier_mix_small(pfc_c), gate_c, h_ctx, pfc_c, ada, l, p, mod_row=bsz)
            ctx_next = _mlp(ctx_mid, g_ff_v, ada, l, w1_bf, w2_bf, g_final, final_norm=False, mod_row=bsz)

        qkv, pfc, gate = _in_proj(x, g_mix_v, ada, l, cos_t, sin_t, w_in_bf, rope=True, mod_row=None)
        attn = _nattn(qkv, qkv_c, bias_all, l)
        x = _merge(attn, _fourier_mix(pfc), gate, x, pfc, ada, l, p, mod_row=None)
        x = _mlp(x, g_ff_v, ada, l, w1_bf, w2_bf, g_final, final_norm=last, mod_row=None)
        if not last:
            h_ctx = ctx_next
    return x
```
